```python
import math
import jax
import jax.numpy as jnp
from jax import lax
import numpy as np

D_MODEL = 4096
BATCH = 4
SEQ = 2048
DEPTH = 4
DEC_BATCH = 32
DEC_SEQ = 4
PAST_LEN = 8192
PAGE_SIZE = 128

N_HEADS = 32
N_KV_HEADS = 4
Q_PER_KV = N_HEADS // N_KV_HEADS
HEAD_DIM = 64
WINDOW = 128
ATT_BLOCK = WINDOW
D_ATT = N_HEADS * HEAD_DIM
D_KV = N_KV_HEADS * HEAD_DIM
D_CONV = D_MODEL // 2
CONV_W = 3
D_MIX_EVEN = D_ATT + D_CONV
SPLIT_IDX = (D_ATT, D_ATT + D_KV, D_ATT + 2 * D_KV, D_ATT + 2 * D_KV + D_CONV, D_ATT + 2 * D_KV + 2 * D_CONV)
D_IN_EVEN = D_ATT + 2 * D_KV + 3 * D_CONV
SSM_GROUP = 16
N_SSM_GROUPS = D_MODEL // SSM_GROUP
SSM_STATE = 64
DT_MIN = 0.001
DT_MAX = 0.1
N_EXPERT_GROUPS = 4
EXPERTS_PER_GROUP = 8
N_EXPERTS = N_EXPERT_GROUPS * EXPERTS_PER_GROUP
TOP_K = 2
D_EXPERT = D_MODEL // 4
MOE_BLOCK = 128
N_EVEN = (DEPTH + 1) // 2
N_ODD = DEPTH // 2
ALPHA = (2 * DEPTH) ** 0.25
BETA = (8 * DEPTH) ** -0.25
LN_EPS = 1e-5

kernel_name = 'hybrid_swa_shortconv_s5_hmoe_step'


def layer_norm(x, g, b):
    xf = x.astype(jnp.float32)
    mu = jnp.mean(xf, axis=-1, keepdims=True)
    var = jnp.mean(jnp.square(xf - mu), axis=-1, keepdims=True)
    return ((xf - mu) * lax.rsqrt(var + LN_EPS) * g.astype(jnp.float32) + b.astype(jnp.float32)).astype(x.dtype)


def sink_attention(q, k, v, mask, sinks):
    b, nb, nq = q.shape[:3]
    qg = q.reshape(b, nb, nq, N_KV_HEADS, Q_PER_KV, HEAD_DIM)
    s = jnp.einsum('bnqkgd,bnskd->bnkgqs', qg, k).astype(jnp.float32) * (HEAD_DIM ** -0.5)
    s = jnp.where(mask[None, :, None, None], s, -jnp.inf)
    sink = jnp.broadcast_to(sinks.astype(jnp.float32).reshape(1, 1, N_KV_HEADS, Q_PER_KV, 1, 1), s.shape[:-1] + (1,))
    p = jax.nn.softmax(jnp.concatenate([s, sink], axis=-1), axis=-1)[..., :-1]
    o = jnp.einsum('bnkgqs,bnskd->bnqkgd', p.astype(v.dtype), v)
    return o.reshape(b, nb, nq, N_HEADS, HEAD_DIM)


def window_attention_prompt(q, k, v, sinks):
    b, s = q.shape[:2]
    nb = s // ATT_BLOCK
    pad = jnp.zeros((b, ATT_BLOCK, N_KV_HEADS, HEAD_DIM), k.dtype)

    def band(t):
        tp = jnp.concatenate([pad, t], axis=1)
        prev = tp[:, :s].reshape(b, nb, ATT_BLOCK, N_KV_HEADS, HEAD_DIM)
        cur = tp[:, ATT_BLOCK:].reshape(b, nb, ATT_BLOCK, N_KV_HEADS, HEAD_DIM)
        return jnp.concatenate([prev, cur], axis=2)

    qpos = jnp.arange(s).reshape(nb, ATT_BLOCK)
    kpos = qpos[:, :1] - ATT_BLOCK + jnp.arange(2 * ATT_BLOCK)[None]
    diff = qpos[:, :, None] - kpos[:, None, :]
    mask = (diff >= 0) & (diff < WINDOW) & (kpos[:, None, :] >= 0)
    o = sink_attention(q.reshape(b, nb, ATT_BLOCK, N_HEADS, HEAD_DIM), band(k), band(v), mask, sinks)
    return o.reshape(b, s, D_ATT)


def window_attention_sample(q, k_new, v_new, k_buf, v_buf, sinks):
    db, t = q.shape[:2]
    w = k_buf.shape[1]
    kc = jnp.concatenate([k_buf, k_new], axis=1)
    vc = jnp.concatenate([v_buf, v_new], axis=1)
    qpos = PAST_LEN + jnp.arange(t)
    kpos = PAST_LEN - w + jnp.arange(w + t)
    diff = qpos[:, None] - kpos[None, :]
    mask = ((diff >= 0) & (diff < WINDOW))[None]
    o = sink_attention(q[:, None], kc[:, None], vc[:, None], mask, sinks)
    return o.reshape(db, t, D_ATT), kc[:, -w:], vc[:, -w:]


def short_conv(u, buf, w):
    t = u.shape[1]
    full = jnp.concatenate([buf, u], axis=1)
    y = full[:, 0:t] * w[0]
    for j in range(1, CONV_W):
        y = y + full[:, j:j + t] * w[j]
    return y, full[:, -(CONV_W - 1):]


def even_mixer(x, k_buf, v_buf, conv_buf, w_in, w_out, sinks, conv_w):
    bsz, t, _ = x.shape
    h = x @ w_in
    q, k, v, gate_b, gate_c, u = jnp.split(h, SPLIT_IDX, axis=-1)
    q = q.reshape(bsz, t, N_HEADS, HEAD_DIM)
    k = k.reshape(bsz, t, N_KV_HEADS, HEAD_DIM)
    v = v.reshape(bsz, t, N_KV_HEADS, HEAD_DIM)
    if k_buf is None:
        attn = window_attention_prompt(q, k, v, sinks)
        new_k, new_v = k[:, -WINDOW:], v[:, -WINDOW:]
        conv_buf = jnp.zeros((bsz, CONV_W - 1, D_CONV), x.dtype)
    else:
        attn, new_k, new_v = window_attention_sample(q, k, v, k_buf, v_buf, sinks)
    conv_out, new_conv = short_conv(gate_c * u, conv_buf, conv_w)
    mix = jnp.concatenate([attn, gate_b * conv_out], axis=-1)
    return mix @ w_out, new_k, new_v, new_conv


def s5_mixer(x, h0_re, h0_im, lam_re, lam_im, log_dt, b_re, b_im, c_re, c_im, d_skip, w_glu, w_o):
    bsz, t, _ = x.shape
    f32 = jnp.float32
    lam = lax.complex(lam_re.astype(f32), lam_im.astype(f32))
    dt = jnp.exp(log_dt.astype(f32))[:, None]
    a_bar = jnp.exp(lam * dt)
    b_bar = ((a_bar - 1.0) / lam)[..., None] * lax.complex(b_re.astype(f32), b_im.astype(f32))
    u = x.astype(f32).reshape(bsz, t, N_SSM_GROUPS, SSM_GROUP).astype(jnp.complex64)
    bu = jnp.einsum('btgc,gpc->tbgp', u, b_bar)
    a_elems = jnp.broadcast_to(a_bar, (t, 1) + a_bar.shape)

    def combine(e1, e2):
        a1, v1 = e1
        a2, v2 = e2
        return a1 * a2, a2 * v1 + v2

    a_cum, h = lax.associative_scan(combine, (a_elems, bu), axis=0)
    h0 = lax.complex(h0_re.astype(f32), h0_im.astype(f32))
    h = h + a_cum * h0[None]
    y = jnp.einsum('tbgp,gcp->btgc', h.real, c_re.astype(f32)) - jnp.einsum('tbgp,gcp->btgc', h.imag, c_im.astype(f32))
    y = y.reshape(bsz, t, D_MODEL) + d_skip.astype(f32) * x.astype(f32)
    g = jax.nn.gelu(y).astype(x.dtype)
    z = g * jax.nn.sigmoid(g @ w_glu)
    return z @ w_o, h[-1].real, h[-1].imag


def hier_moe(x, w_rg, w_re, w_gate, w_up, w_down):
    shp = x.shape
    xt = x.reshape(-1, D_MODEL)
    n = xt.shape[0]
    lg = (xt @ w_rg).astype(jnp.float32)
    grp = jnp.argmax(lg, axis=-1)
    gate_g = jnp.take_along_axis(jax.nn.softmax(lg, axis=-1), grp[:, None], axis=1)
    le = (xt @ w_re).astype(jnp.float32).reshape(n, N_EXPERT_GROUPS, EXPERTS_PER_GROUP)
    le = jnp.take_along_axis(le, grp[:, None, None], axis=1)[:, 0]
    top_l, top_i = lax.top_k(le, TOP_K)
    gate = (gate_g * jax.nn.softmax(top_l, axis=-1)).reshape(-1)
    eid = (grp[:, None] * EXPERTS_PER_GROUP + top_i).reshape(-1)
    tok = jnp.arange(n * TOP_K) // TOP_K
    order = jnp.argsort(eid)
    eid_s, tok_s, gate_s = eid[order], tok[order], gate[order]
    counts = jnp.zeros((N_EXPERTS,), jnp.int32).at[eid].add(1)
    padded = (counts + MOE_BLOCK - 1) // MOE_BLOCK * MOE_BLOCK
    start = jnp.cumsum(counts) - counts
    pend = jnp.cumsum(padded)
    pstart = pend - padded
    dest = pstart[eid_s] + jnp.arange(n * TOP_K) - start[eid_s]
    n_blocks = -(-(n * TOP_K) // MOE_BLOCK) + N_EXPERTS
    xs = jnp.zeros((n_blocks * MOE_BLOCK, D_MODEL), x.dtype).at[dest].set(xt[tok_s])
    blk_e = jnp.minimum(jnp.searchsorted(pend, jnp.arange(n_blocks) * MOE_BLOCK, side='right'), N_EXPERTS - 1)

    def expert_block(args):
        xb, e = args
        hid = jax.nn.silu(xb @ w_gate[e]) * (xb @ w_up[e])
        return hid @ w_down[e]

    ys = lax.map(expert_block, (xs.reshape(n_blocks, MOE_BLOCK, D_MODEL), blk_e)).reshape(-1, D_MODEL)
    upd = (ys[dest] * gate_s[:, None]).astype(xt.dtype)
    out = jnp.zeros_like(xt).at[tok_s].add(upd)
    return out.reshape(shp)


def run_group(x, cache_k, cache_v, state_conv, state_re, state_im,
              w_in, w_out, attn_sinks, conv_w, ssm_lambda_re, ssm_lambda_im, ssm_log_dt,
              ssm_b_re, ssm_b_im, ssm_c_re, ssm_c_im, ssm_d, w_glu, w_o,
              ln1_g, ln1_b, ln2_g, ln2_b, w_router_group, w_router_expert, w_gate, w_up, w_down):
    prompt = cache_k is None
    bsz = x.shape[0]
    ks, vs, convs, res, ims = [], [], [], [], []
    for layer in range(DEPTH):
        i = layer // 2
        if layer % 2 == 0:
            if prompt:
                kb, vb, cb = None, None, None
            else:
                kb, vb, cb = cache_k[i], cache_v[i], state_conv[i]
            mix, nk, nv, nc = even_mixer(x, kb, vb, cb, w_in[i], w_out[i], attn_sinks[i], conv_w[i])
            ks.append(nk)
            vs.append(nv)
            convs.append(nc)
        else:
            if prompt:
                h_re = jnp.zeros((bsz, N_SSM_GROUPS, SSM_STATE), jnp.float32)
                h_im = jnp.zeros((bsz, N_SSM_GROUPS, SSM_STATE), jnp.float32)
            else:
                h_re, h_im = state_re[i], state_im[i]
            mix, nre, nim = s5_mixer(x, h_re, h_im, ssm_lambda_re[i], ssm_lambda_im[i], ssm_log_dt[i],
                                     ssm_b_re[i], ssm_b_im[i], ssm_c_re[i], ssm_c_im[i], ssm_d[i], w_glu[i], w_o[i])
            res.append(nre)
            ims.append(nim)
        x = layer_norm(ALPHA * x + mix, ln1_g[layer], ln1_b[layer])
        ffn = hier_moe(x, w_router_group[layer], w_router_expert[layer], w_gate[layer], w_up[layer], w_down[layer])
        x = layer_norm(ALPHA * x + ffn, ln2_g[layer], ln2_b[layer])
    return x, jnp.stack(ks), jnp.stack(vs), jnp.stack(convs), jnp.stack(res), jnp.stack(ims)


def setup_inputs(seed: int = 0) -> dict:
    key = jax.random.key(seed)
    ks = iter(list(jax.random.split(key, 40)))
    f32 = jnp.float32

    def nrm(shape, scale):
        return jax.random.normal(next(ks), shape, f32) * scale

    lam_im_base = math.pi * jnp.arange(SSM_STATE, dtype=f32)
    return {
        'x_prompt': nrm((BATCH, SEQ, D_MODEL), 1.0),
        'x_sample': nrm((DEC_BATCH, DEC_SEQ, D_MODEL), 1.0),
        'cache_k': nrm((N_EVEN, DEC_BATCH, WINDOW, N_KV_HEADS, HEAD_DIM), 1.0),
        'cache_v': nrm((N_EVEN, DEC_BATCH, WINDOW, N_KV_HEADS, HEAD_DIM), 1.0),
        'state_conv': nrm((N_EVEN, DEC_BATCH, CONV_W - 1, D_CONV), 1.0),
        'state_ssm_re': nrm((N_ODD, DEC_BATCH, N_SSM_GROUPS, SSM_STATE), 1.0),
        'state_ssm_im': nrm((N_ODD, DEC_BATCH, N_SSM_GROUPS, SSM_STATE), 1.0),
        'w_in': nrm((N_EVEN, D_MODEL, D_IN_EVEN), D_MODEL ** -0.5),
        'w_out': nrm((N_EVEN, D_MIX_EVEN, D_MODEL), BETA * D_MIX_EVEN ** -0.5),
        'attn_sinks': nrm((N_EVEN, N_HEADS), 1.0),
        'conv_w': nrm((N_EVEN, CONV_W, D_CONV), CONV_W ** -0.5),
        'ssm_lambda_re': -0.5 + nrm((N_ODD, N_SSM_GROUPS, SSM_STATE), 0.01),
        'ssm_lambda_im': lam_im_base + nrm((N_ODD, N_SSM_GROUPS, SSM_STATE), 0.01),
        'ssm_log_dt': jax.random.uniform(next(ks), (N_ODD, N_SSM_GROUPS), f32, math.log(DT_MIN), math.log(DT_MAX)),
        'ssm_b_re': nrm((N_ODD, N_SSM_GROUPS, SSM_STATE, SSM_GROUP), (2 * SSM_GROUP) ** -0.5),
        'ssm_b_im': nrm((N_ODD, N_SSM_GROUPS, SSM_STATE, SSM_GROUP), (2 * SSM_GROUP) ** -0.5),
        'ssm_c_re': nrm((N_ODD, N_SSM_GROUPS, SSM_GROUP, SSM_STATE), SSM_STATE ** -0.5),
        'ssm_c_im': nrm((N_ODD, N_SSM_GROUPS, SSM_GROUP, SSM_STATE), SSM_STATE ** -0.5),
        'ssm_d': nrm((N_ODD, D_MODEL), 1.0),
        'w_glu': nrm((N_ODD, D_MODEL, D_MODEL), D_MODEL ** -0.5),
        'w_o': nrm((N_ODD, D_MODEL, D_MODEL), BETA * D_MODEL ** -0.5),
        'ln1_g': 1.0 + nrm((DEPTH, D_MODEL), 0.01),
        'ln1_b': nrm((DEPTH, D_MODEL), 0.01),
        'ln2_g': 1.0 + nrm((DEPTH, D_MODEL), 0.01),
        'ln2_b': nrm((DEPTH, D_MODEL), 0.01),
        'w_router_group': nrm((DEPTH, D_MODEL, N_EXPERT_GROUPS), D_MODEL ** -0.5),
        'w_router_expert': nrm((DEPTH, D_MODEL, N_EXPERTS), D_MODEL ** -0.5),
        'w_gate': nrm((DEPTH, N_EXPERTS, D_MODEL, D_EXPERT), D_MODEL ** -0.5),
        'w_up': nrm((DEPTH, N_EXPERTS, D_MODEL, D_EXPERT), D_MODEL ** -0.5),
        'w_down': nrm((DEPTH, N_EXPERTS, D_EXPERT, D_MODEL), BETA * D_EXPERT ** -0.5),
    }


def reference(x_prompt, x_sample, cache_k, cache_v, state_conv, state_ssm_re, state_ssm_im,
              w_in, w_out, attn_sinks, conv_w, ssm_lambda_re, ssm_lambda_im, ssm_log_dt,
              ssm_b_re, ssm_b_im, ssm_c_re, ssm_c_im, ssm_d, w_glu, w_o,
              ln1_g, ln1_b, ln2_g, ln2_b, w_router_group, w_router_expert, w_gate, w_up, w_down):
    params = (w_in, w_out, attn_sinks, conv_w, ssm_lambda_re, ssm_lambda_im, ssm_log_dt,
              ssm_b_re, ssm_b_im, ssm_c_re, ssm_c_im, ssm_d, w_glu, w_o,
              ln1_g, ln1_b, ln2_g, ln2_b, w_router_group, w_router_expert, w_gate, w_up, w_down)
    y_prompt, k_p, v_p, conv_p, re_p, im_p = run_group(x_prompt, None, None, None, None, None, *params)
    y_sample, k_s, v_s, conv_s, re_s, im_s = run_group(x_sample, cache_k, cache_v, state_conv,
                                                       state_ssm_re, state_ssm_im, *params)
    return (y_prompt, y_sample, k_p, v_p, conv_p, re_p, im_p, k_s, v_s, conv_s, re_s, im_s)
```

```python
import functools
import math

import jax
import jax.numpy as jnp
from jax import lax
from jax.experimental import pallas as pl
from jax.experimental.pallas import tpu as pltpu

F32 = jnp.float32
BF16 = jnp.bfloat16

D_MODEL = 4096
DEPTH = 4
N_HEADS = 32
N_KV_HEADS = 4
Q_PER_KV = N_HEADS // N_KV_HEADS
HEAD_DIM = 64
WINDOW = 128
D_ATT = N_HEADS * HEAD_DIM
D_KV = N_KV_HEADS * HEAD_DIM
D_CONV = D_MODEL // 2
CONV_W = 3
D_IN_EVEN = D_ATT + 2 * D_KV + 3 * D_CONV
SSM_GROUP = 16
N_SSM_GROUPS = D_MODEL // SSM_GROUP
SSM_STATE = 64
N_EXPERT_GROUPS = 4
EXPERTS_PER_GROUP = 8
N_EXPERTS = N_EXPERT_GROUPS * EXPERTS_PER_GROUP
D_EXPERT = D_MODEL // 4
ALPHA = (2 * DEPTH) ** 0.25
LN_EPS = 1e-5

LANES = 128
VMEM_LIMIT = 58 * 1024 * 1024

COL_BLK = 256
K_BLK = D_ATT // COL_BLK
V_BLK = (D_ATT + D_KV) // COL_BLK
B_BLK = (D_ATT + 2 * D_KV) // COL_BLK
C_BLK = B_BLK + D_CONV // COL_BLK
U_BLK = C_BLK + D_CONV // COL_BLK


def _params(sem):
    return pltpu.CompilerParams(dimension_semantics=sem, vmem_limit_bytes=VMEM_LIMIT)


def _sigmoid(x):
    return 1.0 / (1.0 + jnp.exp(-x))


def _mm_kernel(*refs, nk, glu):
    if glu:
        a_ref, w_ref, g_ref, o_ref, acc_ref = refs
    else:
        a_ref, w_ref, o_ref, acc_ref = refs
    k = pl.program_id(2)

    @pl.when(k == 0)
    def _():
        acc_ref[...] = jnp.zeros_like(acc_ref)

    acc_ref[...] += jnp.dot(a_ref[...].astype(BF16), w_ref[...].astype(BF16), preferred_element_type=F32)

    @pl.when(k == nk - 1)
    def _():
        acc = acc_ref[...]
        if glu:
            acc = g_ref[...] * _sigmoid(acc)
        o_ref[...] = acc.astype(o_ref.dtype)


def matmul(a, w, layer, *, tm, tn, tk, out_dtype, glu_gate=None):
    m, kdim = a.shape
    n = w.shape[-1]
    assert m % tm == 0 and n % tn == 0 and kdim % tk == 0
    nk = kdim // tk
    in_specs = [
        pl.BlockSpec((tm, tk), lambda i, j, k: (i, k)),
        pl.BlockSpec((None, tk, tn), lambda i, j, k: (layer, k, j)),
    ]
    args = [a, w]
    if glu_gate is not None:
        in_specs.append(pl.BlockSpec((tm, tn), lambda i, j, k: (i, j)))
        args.append(glu_gate)
    return pl.pallas_call(
        functools.partial(_mm_kernel, nk=nk, glu=glu_gate is not None),
        out_shape=jax.ShapeDtypeStruct((m, n), out_dtype),
        grid=(m // tm, n // tn, nk),
        in_specs=in_specs,
        out_specs=pl.BlockSpec((tm, tn), lambda i, j, k: (i, j)),
        scratch_shapes=[pltpu.VMEM((tm, tn), F32)],
        compiler_params=_params(("parallel", "parallel", "arbitrary")),
        name="matmul",
    )(*args)


def _layer_norm_rows(v, g, b):
    mu = jnp.mean(v, axis=-1, keepdims=True)
    d = v - mu
    var = jnp.mean(d * d, axis=-1, keepdims=True)
    return d * lax.rsqrt(var + LN_EPS) * g + b


def _ln_kernel(x_ref, y_ref, g_ref, b_ref, o_ref):
    o_ref[...] = _layer_norm_rows(ALPHA * x_ref[...] + y_ref[...], g_ref[...], b_ref[...])


def ln_residual(x, y, g, b, layer, *, tm):
    n, d = x.shape
    row = pl.BlockSpec((tm, d), lambda i: (i, 0))
    par = pl.BlockSpec((None, 1, d), lambda i: (layer, 0, 0))
    return pl.pallas_call(
        _ln_kernel,
        out_shape=jax.ShapeDtypeStruct((n, d), F32),
        grid=(n // tm,),
        in_specs=[row, row, par, par],
        out_specs=row,
        compiler_params=_params(("parallel",)),
        name="ln_residual",
    )(x, y, g.reshape(DEPTH, 1, d), b.reshape(DEPTH, 1, d))


def _ln_moe_kernel(x_ref, y1_ref, y2_ref, gate_ref, g_ref, b_ref, o_ref):
    gate = gate_ref[...]
    ffn = y1_ref[...] * gate[:, 0:1] + y2_ref[...] * gate[:, 1:2]
    o_ref[...] = _layer_norm_rows(ALPHA * x_ref[...] + ffn, g_ref[...], b_ref[...])


def ln_moe_combine(x, y1, y2, gates, g, b, layer, *, tm):
    n, d = x.shape
    row = pl.BlockSpec((tm, d), lambda i: (i, 0))
    par = pl.BlockSpec((None, 1, d), lambda i: (layer, 0, 0))
    return pl.pallas_call(
        _ln_moe_kernel,
        out_shape=jax.ShapeDtypeStruct((n, d), F32),
        grid=(n // tm,),
        in_specs=[row, row, row, pl.BlockSpec((tm, LANES), lambda i: (i, 0)), par, par],
        out_specs=row,
        compiler_params=_params(("parallel",)),
        name="ln_moe_combine",
    )(x, y1, y2, gates, g.reshape(DEPTH, 1, d), b.reshape(DEPTH, 1, d))


def _softmax_sink_pv(s, valid, sink_col, vband):
    s = jnp.where(valid, s, -jnp.inf)
    m = jnp.maximum(jnp.max(s, axis=1, keepdims=True), sink_col)
    p = jnp.exp(s - m)
    denom = jnp.sum(p, axis=1, keepdims=True) + jnp.exp(sink_col - m)
    o = jnp.dot(p.astype(BF16), vband, preferred_element_type=F32)
    return o / denom


def _attn_prompt_kernel(sink_ref, q_ref, kc_ref, kp_ref, vc_ref, vp_ref, o_ref):
    nb = pl.program_id(1)
    rows = Q_PER_KV * WINDOW
    r = lax.broadcasted_iota(jnp.int32, (rows, 2 * WINDOW), 0) & (WINDOW - 1)
    c = lax.broadcasted_iota(jnp.int32, (rows, 2 * WINDOW), 1)
    valid = (c > r) & (c <= r + WINDOW) & ((nb > 0) | (c >= WINDOW))
    for j in range(N_KV_HEADS):
        heads = range(j * Q_PER_KV, (j + 1) * Q_PER_KV)
        qg = jnp.concatenate([q_ref[:, h * HEAD_DIM:(h + 1) * HEAD_DIM] for h in heads], axis=0).astype(BF16)
        cols = slice(j * HEAD_DIM, (j + 1) * HEAD_DIM)
        kband = jnp.concatenate([kp_ref[:, cols], kc_ref[:, cols]], axis=0).astype(BF16)
        vband = jnp.concatenate([vp_ref[:, cols], vc_ref[:, cols]], axis=0).astype(BF16)
        s = lax.dot_general(qg, kband, (((1,), (1,)), ((), ())), preferred_element_type=F32) * (HEAD_DIM ** -0.5)
        sink_col = jnp.concatenate([jnp.full((WINDOW, 1), sink_ref[h], F32) for h in heads], axis=0)
        o = _softmax_sink_pv(s, valid, sink_col, vband)
        o = jnp.concatenate([o[i * WINDOW:(i + 1) * WINDOW] for i in range(Q_PER_KV)], axis=1)
        o_ref[:, j * Q_PER_KV * HEAD_DIM:(j + 1) * Q_PER_KV * HEAD_DIM] = o.astype(o_ref.dtype)


def attn_prompt(h, sinks, batch, seq):
    nb = seq // WINDOW
    cur = lambda col: pl.BlockSpec((WINDOW, COL_BLK), lambda b, n, s: (b * nb + n, col))
    prev = lambda col: pl.BlockSpec((WINDOW, COL_BLK), lambda b, n, s: (b * nb + jnp.maximum(n - 1, 0), col))
    return pl.pallas_call(
        _attn_prompt_kernel,
        out_shape=jax.ShapeDtypeStruct((batch * seq, D_ATT), BF16),
        grid_spec=pltpu.PrefetchScalarGridSpec(
            num_scalar_prefetch=1,
            grid=(batch, nb),
            in_specs=[pl.BlockSpec((WINDOW, D_ATT), lambda b, n, s: (b * nb + n, 0)),
                      cur(K_BLK), prev(K_BLK), cur(V_BLK), prev(V_BLK)],
            out_specs=pl.BlockSpec((WINDOW, D_ATT), lambda b, n, s: (b * nb + n, 0)),
        ),
        compiler_params=_params(("parallel", "arbitrary")),
        name="attn_prompt",
    )(sinks, h, h, h, h, h)


T_PAD = 8
KC_PAD = WINDOW + T_PAD


def _attn_sample_kernel(sink_ref, q_ref, kc_ref, vc_ref, o_ref, *, n_new):
    rows = Q_PER_KV * T_PAD
    t = lax.broadcasted_iota(jnp.int32, (rows, KC_PAD), 0) & (T_PAD - 1)
    c = lax.broadcasted_iota(jnp.int32, (rows, KC_PAD), 1)
    valid = (c > t) & (c <= t + WINDOW) & (c < WINDOW + n_new)
    for j in range(N_KV_HEADS):
        heads = range(j * Q_PER_KV, (j + 1) * Q_PER_KV)
        qg = jnp.concatenate([q_ref[:, h * HEAD_DIM:(h + 1) * HEAD_DIM] for h in heads], axis=0).astype(BF16)
        cols = slice(j * HEAD_DIM, (j + 1) * HEAD_DIM)
        kband = kc_ref[:, cols].astype(BF16)
        vband = vc_ref[:, cols].astype(BF16)
        s = lax.dot_general(qg, kband, (((1,), (1,)), ((), ())), preferred_element_type=F32) * (HEAD_DIM ** -0.5)
        sink_col = jnp.concatenate([jnp.full((T_PAD, 1), sink_ref[h], F32) for h in heads], axis=0)
        o = _softmax_sink_pv(s, valid, sink_col, vband)
        o = jnp.concatenate([o[i * T_PAD:(i + 1) * T_PAD] for i in range(Q_PER_KV)], axis=1)
        o_ref[:, j * Q_PER_KV * HEAD_DIM:(j + 1) * Q_PER_KV * HEAD_DIM] = o.astype(o_ref.dtype)


def attn_sample(q, kc, vc, sinks, n_new):
    db = q.shape[0]
    return pl.pallas_call(
        functools.partial(_attn_sample_kernel, n_new=n_new),
        out_shape=jax.ShapeDtypeStruct((db, T_PAD, D_ATT), BF16),
        grid_spec=pltpu.PrefetchScalarGridSpec(
            num_scalar_prefetch=1,
            grid=(db,),
            in_specs=[pl.BlockSpec((None, T_PAD, D_ATT), lambda b, s: (b, 0, 0)),
                      pl.BlockSpec((None, KC_PAD, D_KV), lambda b, s: (b, 0, 0)),
                      pl.BlockSpec((None, KC_PAD, D_KV), lambda b, s: (b, 0, 0))],
            out_specs=pl.BlockSpec((None, T_PAD, D_ATT), lambda b, s: (b, 0, 0)),
        ),
        compiler_params=_params(("parallel",)),
        name="attn_sample",
    )(sinks, q, kc, vc)


def _conv_prompt_kernel(b_ref, c_ref, u_ref, w_ref, o_ref, st_ref):
    cu = c_ref[...] * u_ref[...]
    t = lax.broadcasted_iota(jnp.int32, cu.shape, 0)
    sh1 = jnp.where(t >= 1, pltpu.roll(cu, 1, axis=0), 0.0)
    sh2 = jnp.where(t >= 2, pltpu.roll(cu, 2, axis=0), 0.0)
    w = w_ref[...]
    y = sh2 * w[0:1] + sh1 * w[1:2] + cu * w[2:3]
    o_ref[...] = (b_ref[...] * y).astype(o_ref.dtype)
    st_ref[...] = cu[cu.shape[0] - 8:]


def conv_prompt(h, conv_w, layer_i, batch, seq):
    nj = D_CONV // COL_BLK
    col = lambda base: pl.BlockSpec((seq, COL_BLK), lambda b, j: (b, base + j))
    return pl.pallas_call(
        _conv_prompt_kernel,
        out_shape=(jax.ShapeDtypeStruct((batch * seq, D_CONV), BF16),
                   jax.ShapeDtypeStruct((batch * 8, D_CONV), F32)),
        grid=(batch, nj),
        in_specs=[col(B_BLK), col(C_BLK), col(U_BLK),
                  pl.BlockSpec((None, CONV_W, COL_BLK), lambda b, j: (layer_i, 0, j))],
        out_specs=(pl.BlockSpec((seq, COL_BLK), lambda b, j: (b, j)),
                   pl.BlockSpec((8, COL_BLK), lambda b, j: (b, j))),
        compiler_params=_params(("parallel", "parallel")),
        name="conv_prompt",
    )(h, h, h, conv_w)


def _conv_sample_kernel(b_ref, c_ref, u_ref, buf_ref, w_ref, o_ref, st_ref, *, n_new):
    w = w_ref[...]
    full = [buf_ref[i] for i in range(CONV_W - 1)] + [c_ref[t] * u_ref[t] for t in range(n_new)]
    for t in range(n_new):
        y = full[t] * w[0:1] + full[t + 1] * w[1:2] + full[t + 2] * w[2:3]
        o_ref[t] = (b_ref[t] * y).astype(o_ref.dtype)
    for i in range(CONV_W - 1):
        st_ref[i] = full[n_new + i]


def conv_sample(bg, cg, ug, buf, conv_w, layer_i):
    n_new, db, _ = bg.shape
    full = lambda a: pl.BlockSpec(a.shape, lambda g: (0,) * a.ndim)
    return pl.pallas_call(
        functools.partial(_conv_sample_kernel, n_new=n_new),
        out_shape=(jax.ShapeDtypeStruct((n_new, db, D_CONV), BF16),
                   jax.ShapeDtypeStruct((CONV_W - 1, db, D_CONV), F32)),
        grid=(1,),
        in_specs=[full(bg), full(cg), full(ug), full(buf),
                  pl.BlockSpec((None, CONV_W, D_CONV), lambda g: (layer_i, 0, 0))],
        out_specs=(pl.BlockSpec((n_new, db, D_CONV), lambda g: (0, 0, 0)),
                   pl.BlockSpec((CONV_W - 1, db, D_CONV), lambda g: (0, 0, 0))),
        compiler_params=_params(("arbitrary",)),
        name="conv_sample",
    )(bg, cg, ug, buf, conv_w)


PAIR = 2
PAIR_STATE = PAIR * SSM_STATE


def _s5_kernel(u_ref, w1_ref, w2_ref, ap_ref, h0_ref, y_ref, hn_ref, *, nbatch, nchunk, width):
    rows = nbatch * nchunk
    z = jnp.dot(u_ref[...], w1_ref[...], preferred_element_type=F32)
    yi = z[:, :width]
    vr = z[:, width:width + PAIR_STATE]
    vi = z[:, width + PAIR_STATE:]
    h0r = h0_ref[0]
    h0i = h0_ref[1]
    if nchunk == 1:
        h0r_rows, h0i_rows = h0r, h0i
    else:
        h0r_rows = jnp.concatenate([jnp.broadcast_to(h0r[b:b + 1], (nchunk, PAIR_STATE)) for b in range(nbatch)], 0)
        h0i_rows = jnp.concatenate([jnp.broadcast_to(h0i[b:b + 1], (nchunk, PAIR_STATE)) for b in range(nbatch)], 0)
    krow = lax.broadcasted_iota(jnp.int32, (rows, PAIR_STATE), 0) & (nchunk - 1)
    ar = ap_ref[0, 0:1]
    ai = ap_ref[0, 1:2]
    first = krow == 0
    vr = vr + jnp.where(first, ar * h0r_rows - ai * h0i_rows, 0.0)
    vi = vi + jnp.where(first, ar * h0i_rows + ai * h0r_rows, 0.0)
    step = 0
    while (1 << step) < nchunk:
        d = 1 << step
        ar = ap_ref[step, 0:1]
        ai = ap_ref[step, 1:2]
        keep = krow >= d
        sr = jnp.where(keep, pltpu.roll(vr, d, axis=0), 0.0)
        si = jnp.where(keep, pltpu.roll(vi, d, axis=0), 0.0)
        vr, vi = vr + (ar * sr - ai * si), vi + (ar * si + ai * sr)
        step += 1
    if nchunk == 1:
        hr, hi = h0r_rows, h0i_rows
    else:
        hr = jnp.where(first, h0r_rows, pltpu.roll(vr, 1, axis=0))
        hi = jnp.where(first, h0i_rows, pltpu.roll(vi, 1, axis=0))
    hcat = jnp.concatenate([hr, hi], axis=1).astype(BF16)
    y_ref[...] = yi + jnp.dot(hcat, w2_ref[...], preferred_element_type=F32)
    if nchunk == 1:
        hn_ref[0] = vr
        hn_ref[1] = vi
    else:
        last = [b * nchunk + nchunk - 1 for b in range(nbatch)]
        hn_ref[0] = jnp.concatenate([vr[i:i + 1] for i in last], axis=0)
        hn_ref[1] = jnp.concatenate([vi[i:i + 1] for i in last], axis=0)


def s5_scan(u, w1, w2, ap, h0, nbatch, nchunk):
    g2, rows, width = u.shape
    nstep = ap.shape[1]
    return pl.pallas_call(
        functools.partial(_s5_kernel, nbatch=nbatch, nchunk=nchunk, width=width),
        out_shape=(jax.ShapeDtypeStruct((g2, rows, width), F32),
                   jax.ShapeDtypeStruct((g2, 2, nbatch, PAIR_STATE), F32)),
        grid=(g2,),
        in_specs=[pl.BlockSpec((None, rows, width), lambda g: (g, 0, 0)),
                  pl.BlockSpec((None, width, width + 2 * PAIR_STATE), lambda g: (g, 0, 0)),
                  pl.BlockSpec((None, 2 * PAIR_STATE, width), lambda g: (g, 0, 0)),
                  pl.BlockSpec((None, nstep, 2, PAIR_STATE), lambda g: (g, 0, 0, 0)),
                  pl.BlockSpec((None, 2, nbatch, PAIR_STATE), lambda g: (g, 0, 0, 0))],
        out_specs=(pl.BlockSpec((None, rows, width), lambda g: (g, 0, 0)),
                   pl.BlockSpec((None, 2, nbatch, PAIR_STATE), lambda g: (g, 0, 0, 0))),
        compiler_params=_params(("parallel",)),
        name="s5_scan",
    )(u, w1, w2, ap, h0)


def _s5_operators(lam_re, lam_im, log_dt, b_re, b_im, c_re, c_im, chunk, nchunk):
    hp = lax.Precision.HIGHEST
    g, p = lam_re.shape
    lam = lax.complex(lam_re, lam_im)
    dt = jnp.exp(log_dt)[:, None]
    a_bar = jnp.exp(lam * dt)
    b_bar = ((a_bar - 1.0) / lam)[..., None] * lax.complex(b_re, b_im)
    steps = jnp.arange(chunk + 1, dtype=F32)
    pw = jnp.exp((lam * dt)[:, None, :] * steps[None, :, None])
    pwr, pwi = pw.real, pw.imag
    bbr, bbi = b_bar.real, b_bar.imag
    mr = c_re[:, None] * pwr[:, :chunk, None] - c_im[:, None] * pwi[:, :chunk, None]
    mi = c_re[:, None] * pwi[:, :chunk, None] + c_im[:, None] * pwr[:, :chunk, None]
    kd = (jnp.einsum('gdcp,gpe->gdec', mr, bbr, precision=hp)
          - jnp.einsum('gdcp,gpe->gdec', mi, bbi, precision=hp))
    s_idx = jnp.arange(chunk)
    lag = s_idx[None, :] - s_idx[:, None]
    ky = jnp.where((lag >= 0)[None, :, :, None, None], kd[:, jnp.maximum(lag, 0)], 0.0)
    ky = ky.transpose(0, 1, 3, 2, 4).reshape(g, chunk * SSM_GROUP, chunk * SSM_GROUP)
    rev_r, rev_i = pwr[:, chunk - 1 - s_idx], pwi[:, chunk - 1 - s_idx]
    ks_r = (rev_r[..., None] * bbr[:, None] - rev_i[..., None] * bbi[:, None]).transpose(0, 1, 3, 2)
    ks_i = (rev_r[..., None] * bbi[:, None] + rev_i[..., None] * bbr[:, None]).transpose(0, 1, 3, 2)
    ks_r = ks_r.reshape(g, chunk * SSM_GROUP, p)
    ks_i = ks_i.reshape(g, chunk * SSM_GROUP, p)
    nr, ni = pwr[:, 1:], pwi[:, 1:]
    kh_r = (c_re[:, None] * nr[:, :, None] - c_im[:, None] * ni[:, :, None])
    kh_i = -(c_re[:, None] * ni[:, :, None] + c_im[:, None] * nr[:, :, None])
    kh_r = kh_r.transpose(0, 3, 1, 2).reshape(g, p, chunk * SSM_GROUP)
    kh_i = kh_i.transpose(0, 3, 1, 2).reshape(g, p, chunk * SSM_GROUP)

    width = chunk * SSM_GROUP
    g2 = g // PAIR
    z_y = jnp.zeros((g2, width, width), F32)
    z_s = jnp.zeros((g2, width, p), F32)
    ky2, ksr2, ksi2 = (t.reshape(g2, PAIR, *t.shape[1:]) for t in (ky, ks_r, ks_i))
    top = jnp.concatenate([ky2[:, 0], z_y, ksr2[:, 0], z_s, ksi2[:, 0], z_s], axis=2)
    bot = jnp.concatenate([z_y, ky2[:, 1], z_s, ksr2[:, 1], z_s, ksi2[:, 1]], axis=2)
    w1 = jnp.concatenate([top, bot], axis=1).astype(BF16)
    khr2, khi2 = (t.reshape(g2, PAIR, p, width) for t in (kh_r, kh_i))
    z_h = jnp.zeros((g2, p, width), F32)
    w2 = jnp.concatenate([
        jnp.concatenate([khr2[:, 0], z_h], axis=2), jnp.concatenate([z_h, khr2[:, 1]], axis=2),
        jnp.concatenate([khi2[:, 0], z_h], axis=2), jnp.concatenate([z_h, khi2[:, 1]], axis=2)], axis=1).astype(BF16)
    nstep = max(1, (nchunk - 1).bit_length())
    mult = (chunk * (2 ** jnp.arange(nstep))).astype(F32)
    ap = jnp.exp((lam * dt)[:, None, :] * mult[None, :, None])
    ap = jnp.stack([ap.real.reshape(g2, PAIR, nstep, p).transpose(0, 2, 1, 3).reshape(g2, nstep, PAIR_STATE),
                    ap.imag.reshape(g2, PAIR, nstep, p).transpose(0, 2, 1, 3).reshape(g2, nstep, PAIR_STATE)], axis=2)
    return w1, w2, ap


def s5_apply(x, h0_re, h0_im, ssm, chunk):
    bsz, t, _ = x.shape
    nchunk = t // chunk
    g2 = N_SSM_GROUPS // PAIR
    width = PAIR * chunk * SSM_GROUP
    w1, w2, ap = _s5_operators(*ssm, chunk, nchunk)
    u = x.astype(BF16).reshape(bsz, nchunk, chunk, g2, PAIR, SSM_GROUP)
    u = u.transpose(3, 0, 1, 4, 2, 5).reshape(g2, bsz * nchunk, width)
    pair_state = lambda s: s.reshape(bsz, g2, PAIR_STATE).transpose(1, 0, 2)
    h0 = jnp.stack([pair_state(h0_re), pair_state(h0_im)], axis=1)
    y, hn = s5_scan(u, w1, w2, ap, h0, bsz, nchunk)
    y = y.reshape(g2, bsz, nchunk, PAIR, chunk, SSM_GROUP).transpose(1, 2, 4, 0, 3, 5).reshape(bsz, t, D_MODEL)
    unpair = lambda s: s.transpose(1, 0, 2).reshape(bsz, N_SSM_GROUPS, SSM_STATE)
    return y, unpair(hn[:, 0]), unpair(hn[:, 1])


def _gelu_skip_kernel(y_ref, x_ref, d_ref, o_ref):
    v = y_ref[...] + d_ref[...] * x_ref[...]
    inner = math.sqrt(2.0 / math.pi) * (v + 0.044715 * (v * v * v))
    o_ref[...] = 0.5 * v * (1.0 + jnp.tanh(inner))


def gelu_skip(y, x, d, layer_i, *, tm):
    n, dm = x.shape
    row = pl.BlockSpec((tm, dm), lambda i: (i, 0))
    return pl.pallas_call(
        _gelu_skip_kernel,
        out_shape=jax.ShapeDtypeStruct((n, dm), F32),
        grid=(n // tm,),
        in_specs=[row, row, pl.BlockSpec((None, 1, dm), lambda i: (layer_i, 0, 0))],
        out_specs=row,
        compiler_params=_params(("parallel",)),
        name="gelu_skip",
    )(y, x, d.reshape(d.shape[0], 1, dm))


ROUTE_COLS = LANES


def _split_bf16(v):
    hi = v.astype(BF16)
    lo = (v - hi.astype(F32)).astype(BF16)
    return hi, lo


def _router_kernel(x_ref, w_ref, gate_ref, eid_ref):
    xh, xl = _split_bf16(x_ref[...])
    wh, wl = _split_bf16(w_ref[...])
    dot = lambda a, b: jnp.dot(a, b, preferred_element_type=F32)
    logits = dot(xh, wh) + (dot(xh, wl) + dot(xl, wh))
    lane = lax.broadcasted_iota(jnp.int32, logits.shape, 1).astype(F32)
    neg = -jnp.inf
    big = float(ROUTE_COLS)
    lg = jnp.where(lane < N_EXPERT_GROUPS, logits, neg)
    m = jnp.max(lg, axis=1, keepdims=True)
    grp = jnp.min(jnp.where(lg == m, lane, big), axis=1, keepdims=True)
    gate_g = 1.0 / jnp.sum(jnp.exp(lg - m), axis=1, keepdims=True)
    lo = N_EXPERT_GROUPS + grp * EXPERTS_PER_GROUP
    le = jnp.where((lane >= lo) & (lane < lo + EXPERTS_PER_GROUP), logits, neg)
    t1 = jnp.max(le, axis=1, keepdims=True)
    i1 = jnp.min(jnp.where(le == t1, lane, big), axis=1, keepdims=True)
    le2 = jnp.where(lane == i1, neg, le)
    t2 = jnp.max(le2, axis=1, keepdims=True)
    i2 = jnp.min(jnp.where(le2 == t2, lane, big), axis=1, keepdims=True)
    e = jnp.exp(t2 - t1)
    p1 = 1.0 / (1.0 + e)
    p2 = e / (1.0 + e)
    gate_ref[...] = jnp.where(lane == 0.0, gate_g * p1, jnp.where(lane == 1.0, gate_g * p2, 0.0))
    eid = jnp.where(lane == 0.0, i1, jnp.where(lane == 1.0, i2, float(N_EXPERT_GROUPS))) - N_EXPERT_GROUPS
    eid_ref[...] = eid.astype(jnp.int32)


def router(x, w_route, layer, *, tm):
    n, d = x.shape
    return pl.pallas_call(
        _router_kernel,
        out_shape=(jax.ShapeDtypeStruct((n, ROUTE_COLS), F32), jax.ShapeDtypeStruct((n, ROUTE_COLS), jnp.int32)),
        grid=(n // tm,),
        in_specs=[pl.BlockSpec((tm, d), lambda i: (i, 0)),
                  pl.BlockSpec((None, d, ROUTE_COLS), lambda i: (layer, 0, 0))],
        out_specs=(pl.BlockSpec((tm, ROUTE_COLS), lambda i: (i, 0)), pl.BlockSpec((tm, ROUTE_COLS), lambda i: (i, 0))),
        compiler_params=_params(("parallel",)),
        name="router",
    )(x, w_route)


MOE_TM = 512
MOE_TF = 256


def _moe_kernel(be_ref, bv_ref, x_ref, wg_ref, wu_ref, wd_ref, o_ref):
    blk = pl.program_id(0)
    f = pl.program_id(1)

    @pl.when(f == 0)
    def _():
        o_ref[...] = jnp.zeros_like(o_ref)

    @pl.when(bv_ref[blk] == 1)
    def _():
        x = x_ref[...]
        gate = jnp.dot(x, wg_ref[...].astype(BF16), preferred_element_type=F32)
        up = jnp.dot(x, wu_ref[...].astype(BF16), preferred_element_type=F32)
        hid = (gate * _sigmoid(gate)) * up
        o_ref[...] += jnp.dot(hid.astype(BF16), wd_ref[...].astype(BF16), preferred_element_type=F32)


def moe_experts(xs, blk_e, blk_valid, w_gate, w_up, w_down, layer):
    p, d = xs.shape
    nf = D_EXPERT // MOE_TF
    fidx = lambda f, bv, b: jnp.where(bv[b] == 1, f, nf - 1)
    return pl.pallas_call(
        _moe_kernel,
        out_shape=jax.ShapeDtypeStruct((p, d), F32),
        grid_spec=pltpu.PrefetchScalarGridSpec(
            num_scalar_prefetch=2,
            grid=(p // MOE_TM, nf),
            in_specs=[pl.BlockSpec((MOE_TM, d), lambda b, f, be, bv: (b, 0)),
                      pl.BlockSpec((None, None, d, MOE_TF), lambda b, f, be, bv: (layer, be[b], 0, fidx(f, bv, b))),
                      pl.BlockSpec((None, None, d, MOE_TF), lambda b, f, be, bv: (layer, be[b], 0, fidx(f, bv, b))),
                      pl.BlockSpec((None, None, MOE_TF, d), lambda b, f, be, bv: (layer, be[b], fidx(f, bv, b), 0))],
            out_specs=pl.BlockSpec((MOE_TM, d), lambda b, f, be, bv: (b, 0)),
        ),
        compiler_params=_params(("arbitrary", "arbitrary")),
        name="moe_experts",
    )(blk_e, blk_valid, xs, w_gate, w_up, w_down)


def moe_dispatch(eids, n):
    nslot = n * 2
    eid = eids.reshape(-1)
    order = jnp.argsort(eid, stable=True)
    eid_s = eid[order]
    tok_s = order // 2
    counts = jnp.zeros((N_EXPERTS,), jnp.int32).at[eid].add(1)
    padded = (counts + MOE_TM - 1) // MOE_TM * MOE_TM
    start = jnp.cumsum(counts) - counts
    pend = jnp.cumsum(padded)
    pstart = pend - padded
    dest = pstart[eid_s] + jnp.arange(nslot, dtype=jnp.int32) - start[eid_s]
    n_blocks = -(-nslot // MOE_TM) + N_EXPERTS
    row_tok = jnp.full((n_blocks * MOE_TM,), n, jnp.int32).at[dest].set(tok_s.astype(jnp.int32))
    pos = jnp.zeros((nslot,), jnp.int32).at[order].set(dest).reshape(n, 2)
    blk_start = jnp.arange(n_blocks, dtype=jnp.int32) * MOE_TM
    blk_valid = (blk_start < pend[-1]).astype(jnp.int32)
    last_e = jnp.searchsorted(pend, pend[-1] - 1, side='right').astype(jnp.int32)
    blk_e = jnp.minimum(jnp.searchsorted(pend, blk_start, side='right').astype(jnp.int32), last_e)
    return pos, row_tok, blk_e, blk_valid


def hier_moe_ln(x, layer, w_route, w_gate, w_up, w_down, ln_g, ln_b, *, tm):
    n = x.shape[0]
    gates, eids = router(x, w_route, layer, tm=tm)
    pos, row_tok, blk_e, blk_valid = moe_dispatch(eids[:, :2], n)
    x_pad = jnp.concatenate([x.astype(BF16), jnp.zeros((1, D_MODEL), BF16)], axis=0)
    xs = x_pad[row_tok]
    ys = moe_experts(xs, blk_e, blk_valid, w_gate, w_up, w_down, layer)
    return ln_moe_combine(x, ys[pos[:, 0]], ys[pos[:, 1]], gates, ln_g, ln_b, layer, tm=tm)


ROW_TILE = 320
MM_TM = 1040
MM_TK = 512


def kernel(x_prompt, x_sample, cache_k, cache_v, state_conv, state_ssm_re, state_ssm_im, w_in, w_out, attn_sinks, conv_w, ssm_lambda_re, ssm_lambda_im, ssm_log_dt, ssm_b_re, ssm_b_im, ssm_c_re, ssm_c_im, ssm_d, w_glu, w_o, ln1_g, ln1_b, ln2_g, ln2_b, w_router_group, w_router_expert, w_gate, w_up, w_down):
    batch, seq, _ = x_prompt.shape
    db, n_new, _ = x_sample.shape
    n_p = batch * seq
    n_s = db * n_new
    n = n_p + n_s
    assert n % ROW_TILE == 0 and n % MM_TM == 0 and n_new <= T_PAD

    w_route = jnp.concatenate(
        [w_router_group, w_router_expert,
         jnp.zeros((DEPTH, D_MODEL, ROUTE_COLS - N_EXPERT_GROUPS - N_EXPERTS), F32)], axis=-1)

    x = jnp.concatenate([x_prompt.reshape(n_p, D_MODEL), x_sample.reshape(n_s, D_MODEL)], axis=0)
    k_p, v_p, conv_p, re_p, im_p = [], [], [], [], []
    k_s, v_s, conv_s, re_s, im_s = [], [], [], [], []
    for layer in range(DEPTH):
        i = layer // 2
        if layer % 2 == 0:
            h = matmul(x, w_in, i, tm=MM_TM, tn=2176, tk=256, out_dtype=F32)
            attn_p = attn_prompt(h, attn_sinks[i], batch, seq)
            gconv_p, cu_tail = conv_prompt(h, conv_w, i, batch, seq)
            hp = h[:n_p].reshape(batch, seq, D_IN_EVEN)
            k_p.append(hp[:, seq - WINDOW:, D_ATT:D_ATT + D_KV].reshape(batch, WINDOW, N_KV_HEADS, HEAD_DIM))
            v_p.append(hp[:, seq - WINDOW:, D_ATT + D_KV:D_ATT + 2 * D_KV].reshape(batch, WINDOW, N_KV_HEADS, HEAD_DIM))
            conv_p.append(cu_tail.reshape(batch, 8, D_CONV)[:, 8 - (CONV_W - 1):])
            hs = h[n_p:].reshape(db, n_new, D_IN_EVEN)
            k_new = hs[:, :, D_ATT:D_ATT + D_KV]
            v_new = hs[:, :, D_ATT + D_KV:D_ATT + 2 * D_KV]
            pad_rows = jnp.zeros((db, KC_PAD - WINDOW - n_new, D_KV), F32)
            kc = jnp.concatenate([cache_k[i].reshape(db, WINDOW, D_KV), k_new, pad_rows], axis=1)
            vc = jnp.concatenate([cache_v[i].reshape(db, WINDOW, D_KV), v_new, pad_rows], axis=1)
            q_s = jnp.pad(hs[:, :, :D_ATT], ((0, 0), (0, T_PAD - n_new), (0, 0)))
            attn_s = attn_sample(q_s, kc, vc, attn_sinks[i], n_new)[:, :n_new].reshape(n_s, D_ATT)
            k_s.append(kc[:, n_new:n_new + WINDOW].reshape(db, WINDOW, N_KV_HEADS, HEAD_DIM))
            v_s.append(vc[:, n_new:n_new + WINDOW].reshape(db, WINDOW, N_KV_HEADS, HEAD_DIM))
            off = D_ATT + 2 * D_KV
            tmaj = lambda a: a.transpose(1, 0, 2)
            gconv_s, st_s = conv_sample(tmaj(hs[:, :, off:off + D_CONV]),
                                        tmaj(hs[:, :, off + D_CONV:off + 2 * D_CONV]),
                                        tmaj(hs[:, :, off + 2 * D_CONV:]),
                                        tmaj(state_conv[i]), conv_w, i)
            conv_s.append(tmaj(st_s))
            mix = jnp.concatenate([
                jnp.concatenate([attn_p, gconv_p], axis=1),
                jnp.concatenate([attn_s, tmaj(gconv_s).reshape(n_s, D_CONV)], axis=1)], axis=0)
            mixed = matmul(mix, w_out, i, tm=MM_TM, tn=1024, tk=MM_TK, out_dtype=F32)
        else:
            ssm = (ssm_lambda_re[i], ssm_lambda_im[i], ssm_log_dt[i], ssm_b_re[i], ssm_b_im[i],
                   ssm_c_re[i], ssm_c_im[i])
            zeros = jnp.zeros((batch, N_SSM_GROUPS, SSM_STATE), F32)
            y_p, nre_p, nim_p = s5_apply(x[:n_p].reshape(batch, seq, D_MODEL), zeros, zeros, ssm, 16)
            y_s, nre_s, nim_s = s5_apply(x[n_p:].reshape(db, n_new, D_MODEL), state_ssm_re[i], state_ssm_im[i],
                                         ssm, n_new)
            re_p.append(nre_p)
            im_p.append(nim_p)
            re_s.append(nre_s)
            im_s.append(nim_s)
            y = jnp.concatenate([y_p.reshape(n_p, D_MODEL), y_s.reshape(n_s, D_MODEL)], axis=0)
            g = gelu_skip(y, x, ssm_d, i, tm=ROW_TILE)
            z = matmul(g, w_glu, i, tm=MM_TM, tn=1024, tk=MM_TK, out_dtype=BF16, glu_gate=g)
            mixed = matmul(z, w_o, i, tm=MM_TM, tn=1024, tk=MM_TK, out_dtype=F32)
        x = ln_residual(x, mixed, ln1_g, ln1_b, layer, tm=ROW_TILE)
        x = hier_moe_ln(x, layer, w_route, w_gate, w_up, w_down, ln2_g, ln2_b, tm=ROW_TILE)

    y_prompt = x[:n_p].reshape(batch, seq, D_MODEL)
    y_sample = x[n_p:].reshape(db, n_new, D_MODEL)
    st = jnp.stack
    return (y_prompt, y_sample, st(k_p), st(v_p), st(conv_p), st(re_p), st(im_p),
            st(k_s), st(v_s), st(conv_s), st(re_s), st(im_s))
```

```python
import functools
import math

import jax
import jax.numpy as jnp
from jax import lax
from jax.experimental import pallas as pl
from jax.experimental.pallas import tpu as pltpu

F32 = jnp.float32
BF16 = jnp.bfloat16

D_MODEL = 4096
DEPTH = 4
N_HEADS = 32
N_KV_HEADS = 4
Q_PER_KV = N_HEADS // N_KV_HEADS
HEAD_DIM = 64
WINDOW = 128
D_ATT = N_HEADS * HEAD_DIM
D_KV = N_KV_HEADS * HEAD_DIM
D_CONV = D_MODEL // 2
CONV_W = 3
D_IN_EVEN = D_ATT + 2 * D_KV + 3 * D_CONV
SSM_GROUP = 16
N_SSM_GROUPS = D_MODEL // SSM_GROUP
SSM_STATE = 64
N_EXPERT_GROUPS = 4
EXPERTS_PER_GROUP = 8
N_EXPERTS = N_EXPERT_GROUPS * EXPERTS_PER_GROUP
D_EXPERT = D_MODEL // 4
ALPHA = (2 * DEPTH) ** 0.25
LN_EPS = 1e-5

LANES = 128
VMEM_LIMIT = 58 * 1024 * 1024

COL_BLK = 256
K_BLK = D_ATT // COL_BLK
V_BLK = (D_ATT + D_KV) // COL_BLK
B_BLK = (D_ATT + 2 * D_KV) // COL_BLK
C_BLK = B_BLK + D_CONV // COL_BLK
U_BLK = C_BLK + D_CONV // COL_BLK


def _params(sem):
    return pltpu.CompilerParams(dimension_semantics=sem, vmem_limit_bytes=VMEM_LIMIT)


def _sigmoid(x):
    return 1.0 / (1.0 + jnp.exp(-x))


def _mm_kernel(*refs, n_a, glu):
    a_refs = refs[:n_a]
    w_ref = refs[n_a]
    g_ref = refs[n_a + 1] if glu else None
    o_ref, wb_ref = refs[-2], refs[-1]

    @pl.when(pl.program_id(1) == 0)
    def _():
        wb_ref[...] = w_ref[...].astype(BF16)

    acc = None
    off = 0
    for a_ref in a_refs:
        kp = a_ref.shape[1]
        part = jnp.dot(a_ref[...], wb_ref[off:off + kp, :], preferred_element_type=F32)
        acc = part if acc is None else acc + part
        off += kp
    if glu:
        acc = g_ref[...] * _sigmoid(acc)
    o_ref[...] = acc.astype(o_ref.dtype)


def matmul(a_parts, w, layer, *, tm, tn, out_dtype, glu_gate=None):
    m = a_parts[0].shape[0]
    kdim, n = w.shape[-2:]
    assert m % tm == 0 and n % tn == 0 and sum(a.shape[1] for a in a_parts) == kdim
    in_specs = [pl.BlockSpec((tm, a.shape[1]), lambda j, i: (i, 0)) for a in a_parts]
    in_specs.append(pl.BlockSpec((None, kdim, tn), lambda j, i: (layer, 0, j)))
    args = list(a_parts) + [w]
    if glu_gate is not None:
        in_specs.append(pl.BlockSpec((tm, tn), lambda j, i: (i, j)))
        args.append(glu_gate)
    return pl.pallas_call(
        functools.partial(_mm_kernel, n_a=len(a_parts), glu=glu_gate is not None),
        out_shape=jax.ShapeDtypeStruct((m, n), out_dtype),
        grid=(n // tn, m // tm),
        in_specs=in_specs,
        out_specs=pl.BlockSpec((tm, tn), lambda j, i: (i, j)),
        scratch_shapes=[pltpu.VMEM((kdim, tn), BF16)],
        compiler_params=_params(("arbitrary", "arbitrary")),
        name="matmul",
    )(*args)


def _layer_norm_rows(v, g, b):
    mu = jnp.mean(v, axis=-1, keepdims=True)
    d = v - mu
    var = jnp.mean(d * d, axis=-1, keepdims=True)
    return d * lax.rsqrt(var + LN_EPS) * g + b


HALF_D = D_MODEL // 2


def _ln_kernel(x_ref, y_ref, g_ref, b_ref, o_ref, op_ref):
    out = _layer_norm_rows(ALPHA * x_ref[...] + y_ref[...], g_ref[...], b_ref[...])
    o_ref[...] = out
    bits = pltpu.bitcast(out.astype(BF16).astype(F32), jnp.int32)
    op_ref[...] = bits[:, HALF_D:] | lax.shift_right_logical(bits[:, :HALF_D], 16)


def ln_residual(x, y, g, b, layer, *, tm):
    n, d = x.shape
    row = pl.BlockSpec((tm, d), lambda i: (i, 0))
    par = pl.BlockSpec((None, 1, d), lambda i: (layer, 0, 0))
    return pl.pallas_call(
        _ln_kernel,
        out_shape=(jax.ShapeDtypeStruct((n, d), F32), jax.ShapeDtypeStruct((n, HALF_D), jnp.int32)),
        grid=(n // tm,),
        in_specs=[row, row, par, par],
        out_specs=(row, pl.BlockSpec((tm, HALF_D), lambda i: (i, 0))),
        compiler_params=_params(("parallel",)),
        name="ln_residual",
    )(x, y, g.reshape(DEPTH, 1, d), b.reshape(DEPTH, 1, d))


def _ln_moe_kernel(x_ref, y1_ref, y2_ref, gate_ref, g_ref, b_ref, o_ref, ob_ref):
    gate = gate_ref[...]
    ffn = y1_ref[...] * gate[:, 0:1] + y2_ref[...] * gate[:, 1:2]
    out = _layer_norm_rows(ALPHA * x_ref[...] + ffn, g_ref[...], b_ref[...])
    o_ref[...] = out
    ob_ref[...] = out.astype(BF16)


def ln_moe_combine(x, y1, y2, gates, g, b, layer, *, tm):
    n, d = x.shape
    row = pl.BlockSpec((tm, d), lambda i: (i, 0))
    par = pl.BlockSpec((None, 1, d), lambda i: (layer, 0, 0))
    return pl.pallas_call(
        _ln_moe_kernel,
        out_shape=(jax.ShapeDtypeStruct((n, d), F32), jax.ShapeDtypeStruct((n, d), BF16)),
        grid=(n // tm,),
        in_specs=[row, row, row, pl.BlockSpec((tm, LANES), lambda i: (i, 0)), par, par],
        out_specs=(row, row),
        compiler_params=_params(("parallel",)),
        name="ln_moe_combine",
    )(x, y1, y2, gates, g.reshape(DEPTH, 1, d), b.reshape(DEPTH, 1, d))


def _softmax_sink_pv(s, valid, sink_col, vband):
    s = jnp.where(valid, s, -jnp.inf)
    m = jnp.maximum(jnp.max(s, axis=1, keepdims=True), sink_col)
    p = jnp.exp(s - m)
    denom = jnp.sum(p, axis=1, keepdims=True) + jnp.exp(sink_col - m)
    o = jnp.dot(p.astype(BF16), vband, preferred_element_type=F32)
    return o / denom


def _attn_prompt_kernel(sink_ref, q_ref, kc_ref, kp_ref, vc_ref, vp_ref, o_ref):
    nb = pl.program_id(1)
    rows = Q_PER_KV * WINDOW
    r = lax.broadcasted_iota(jnp.int32, (rows, 2 * WINDOW), 0) & (WINDOW - 1)
    c = lax.broadcasted_iota(jnp.int32, (rows, 2 * WINDOW), 1)
    valid = (c > r) & (c <= r + WINDOW) & ((nb > 0) | (c >= WINDOW))
    for j in range(N_KV_HEADS):
        heads = range(j * Q_PER_KV, (j + 1) * Q_PER_KV)
        qg = jnp.concatenate([q_ref[:, h * HEAD_DIM:(h + 1) * HEAD_DIM] for h in heads], axis=0).astype(BF16)
        cols = slice(j * HEAD_DIM, (j + 1) * HEAD_DIM)
        kband = jnp.concatenate([kp_ref[:, cols], kc_ref[:, cols]], axis=0).astype(BF16)
        vband = jnp.concatenate([vp_ref[:, cols], vc_ref[:, cols]], axis=0).astype(BF16)
        s = lax.dot_general(qg, kband, (((1,), (1,)), ((), ())), preferred_element_type=F32) * (HEAD_DIM ** -0.5)
        sink_col = jnp.concatenate([jnp.full((WINDOW, 1), sink_ref[h], F32) for h in heads], axis=0)
        o = _softmax_sink_pv(s, valid, sink_col, vband)
        o = jnp.concatenate([o[i * WINDOW:(i + 1) * WINDOW] for i in range(Q_PER_KV)], axis=1)
        o_ref[:, j * Q_PER_KV * HEAD_DIM:(j + 1) * Q_PER_KV * HEAD_DIM] = o.astype(o_ref.dtype)


def attn_prompt(h, sinks, batch, seq):
    nb = seq // WINDOW
    cur = lambda col: pl.BlockSpec((WINDOW, COL_BLK), lambda b, n, s: (b * nb + n, col))
    prev = lambda col: pl.BlockSpec((WINDOW, COL_BLK), lambda b, n, s: (b * nb + jnp.maximum(n - 1, 0), col))
    return pl.pallas_call(
        _attn_prompt_kernel,
        out_shape=jax.ShapeDtypeStruct((h.shape[0], D_ATT), BF16),
        grid_spec=pltpu.PrefetchScalarGridSpec(
            num_scalar_prefetch=1,
            grid=(batch, nb),
            in_specs=[pl.BlockSpec((WINDOW, D_ATT), lambda b, n, s: (b * nb + n, 0)),
                      cur(K_BLK), prev(K_BLK), cur(V_BLK), prev(V_BLK)],
            out_specs=pl.BlockSpec((WINDOW, D_ATT), lambda b, n, s: (b * nb + n, 0)),
        ),
        compiler_params=_params(("parallel", "arbitrary")),
        name="attn_prompt",
    )(sinks, h, h, h, h, h)


T_PAD = 8
KC_PAD = WINDOW + T_PAD


def _attn_sample_kernel(sink_ref, q_ref, kc_ref, vc_ref, o_ref, *, n_new):
    rows = Q_PER_KV * T_PAD
    t = lax.broadcasted_iota(jnp.int32, (rows, KC_PAD), 0) & (T_PAD - 1)
    c = lax.broadcasted_iota(jnp.int32, (rows, KC_PAD), 1)
    valid = (c > t) & (c <= t + WINDOW) & (c < WINDOW + n_new)
    for j in range(N_KV_HEADS):
        heads = range(j * Q_PER_KV, (j + 1) * Q_PER_KV)
        qg = jnp.concatenate([q_ref[:, h * HEAD_DIM:(h + 1) * HEAD_DIM] for h in heads], axis=0).astype(BF16)
        cols = slice(j * HEAD_DIM, (j + 1) * HEAD_DIM)
        kband = kc_ref[:, cols].astype(BF16)
        vband = vc_ref[:, cols].astype(BF16)
        s = lax.dot_general(qg, kband, (((1,), (1,)), ((), ())), preferred_element_type=F32) * (HEAD_DIM ** -0.5)
        sink_col = jnp.concatenate([jnp.full((T_PAD, 1), sink_ref[h], F32) for h in heads], axis=0)
        o = _softmax_sink_pv(s, valid, sink_col, vband)
        o = jnp.concatenate([o[i * T_PAD:(i + 1) * T_PAD] for i in range(Q_PER_KV)], axis=1)
        o_ref[:, j * Q_PER_KV * HEAD_DIM:(j + 1) * Q_PER_KV * HEAD_DIM] = o.astype(o_ref.dtype)


def attn_sample(q, kc, vc, sinks, n_new):
    db = q.shape[0]
    return pl.pallas_call(
        functools.partial(_attn_sample_kernel, n_new=n_new),
        out_shape=jax.ShapeDtypeStruct((db, T_PAD, D_ATT), BF16),
        grid_spec=pltpu.PrefetchScalarGridSpec(
            num_scalar_prefetch=1,
            grid=(db,),
            in_specs=[pl.BlockSpec((None, T_PAD, D_ATT), lambda b, s: (b, 0, 0)),
                      pl.BlockSpec((None, KC_PAD, D_KV), lambda b, s: (b, 0, 0)),
                      pl.BlockSpec((None, KC_PAD, D_KV), lambda b, s: (b, 0, 0))],
            out_specs=pl.BlockSpec((None, T_PAD, D_ATT), lambda b, s: (b, 0, 0)),
        ),
        compiler_params=_params(("parallel",)),
        name="attn_sample",
    )(sinks, q, kc, vc)


def _conv_prompt_kernel(b_ref, c_ref, u_ref, w_ref, o_ref, st_ref):
    cu = c_ref[...] * u_ref[...]
    t = lax.broadcasted_iota(jnp.int32, cu.shape, 0)
    sh1 = jnp.where(t >= 1, pltpu.roll(cu, 1, axis=0), 0.0)
    sh2 = jnp.where(t >= 2, pltpu.roll(cu, 2, axis=0), 0.0)
    w = w_ref[...]
    y = sh2 * w[0:1] + sh1 * w[1:2] + cu * w[2:3]
    o_ref[...] = (b_ref[...] * y).astype(o_ref.dtype)
    st_ref[...] = cu[cu.shape[0] - 8:]


def conv_prompt(h, conv_w, layer_i, batch, seq):
    nj = D_CONV // COL_BLK
    col = lambda base: pl.BlockSpec((seq, COL_BLK), lambda b, j: (b, base + j))
    return pl.pallas_call(
        _conv_prompt_kernel,
        out_shape=(jax.ShapeDtypeStruct((h.shape[0], D_CONV), BF16),
                   jax.ShapeDtypeStruct((batch * 8, D_CONV), F32)),
        grid=(batch, nj),
        in_specs=[col(B_BLK), col(C_BLK), col(U_BLK),
                  pl.BlockSpec((None, CONV_W, COL_BLK), lambda b, j: (layer_i, 0, j))],
        out_specs=(pl.BlockSpec((seq, COL_BLK), lambda b, j: (b, j)),
                   pl.BlockSpec((8, COL_BLK), lambda b, j: (b, j))),
        compiler_params=_params(("parallel", "parallel")),
        name="conv_prompt",
    )(h, h, h, conv_w)


def _conv_sample_kernel(b_ref, c_ref, u_ref, buf_ref, w_ref, o_ref, st_ref, *, n_new):
    w = w_ref[...]
    full = [buf_ref[i] for i in range(CONV_W - 1)] + [c_ref[t] * u_ref[t] for t in range(n_new)]
    for t in range(n_new):
        y = full[t] * w[0:1] + full[t + 1] * w[1:2] + full[t + 2] * w[2:3]
        o_ref[t] = (b_ref[t] * y).astype(o_ref.dtype)
    for i in range(CONV_W - 1):
        st_ref[i] = full[n_new + i]


def conv_sample(bg, cg, ug, buf, conv_w, layer_i):
    n_new, db, _ = bg.shape
    full = lambda a: pl.BlockSpec(a.shape, lambda g: (0,) * a.ndim)
    return pl.pallas_call(
        functools.partial(_conv_sample_kernel, n_new=n_new),
        out_shape=(jax.ShapeDtypeStruct((n_new, db, D_CONV), BF16),
                   jax.ShapeDtypeStruct((CONV_W - 1, db, D_CONV), F32)),
        grid=(1,),
        in_specs=[full(bg), full(cg), full(ug), full(buf),
                  pl.BlockSpec((None, CONV_W, D_CONV), lambda g: (layer_i, 0, 0))],
        out_specs=(pl.BlockSpec((n_new, db, D_CONV), lambda g: (0, 0, 0)),
                   pl.BlockSpec((CONV_W - 1, db, D_CONV), lambda g: (0, 0, 0))),
        compiler_params=_params(("arbitrary",)),
        name="conv_sample",
    )(bg, cg, ug, buf, conv_w)


PAIR = 2
PAIR_STATE = PAIR * SSM_STATE


def _s5_kernel(u_ref, w1_ref, w2_ref, ap_ref, h0_ref, y_ref, hn_ref, *, nbatch, nchunk, width):
    rows = nbatch * nchunk
    z = jnp.dot(u_ref[...], w1_ref[...], preferred_element_type=F32)
    yi = z[:, :width]
    vr = z[:, width:width + PAIR_STATE]
    vi = z[:, width + PAIR_STATE:]
    h0r = h0_ref[0]
    h0i = h0_ref[1]
    if nchunk == 1:
        h0r_rows, h0i_rows = h0r, h0i
    else:
        h0r_rows = jnp.concatenate([jnp.broadcast_to(h0r[b:b + 1], (nchunk, PAIR_STATE)) for b in range(nbatch)], 0)
        h0i_rows = jnp.concatenate([jnp.broadcast_to(h0i[b:b + 1], (nchunk, PAIR_STATE)) for b in range(nbatch)], 0)
    krow = lax.broadcasted_iota(jnp.int32, (rows, PAIR_STATE), 0) & (nchunk - 1)
    ar = ap_ref[0, 0:1]
    ai = ap_ref[0, 1:2]
    first = krow == 0
    vr = vr + jnp.where(first, ar * h0r_rows - ai * h0i_rows, 0.0)
    vi = vi + jnp.where(first, ar * h0i_rows + ai * h0r_rows, 0.0)
    step = 0
    while (1 << step) < nchunk:
        d = 1 << step
        ar = ap_ref[step, 0:1]
        ai = ap_ref[step, 1:2]
        keep = krow >= d
        sr = jnp.where(keep, pltpu.roll(vr, d, axis=0), 0.0)
        si = jnp.where(keep, pltpu.roll(vi, d, axis=0), 0.0)
        vr, vi = vr + (ar * sr - ai * si), vi + (ar * si + ai * sr)
        step += 1
    if nchunk == 1:
        hr, hi = h0r_rows, h0i_rows
    else:
        hr = jnp.where(first, h0r_rows, pltpu.roll(vr, 1, axis=0))
        hi = jnp.where(first, h0i_rows, pltpu.roll(vi, 1, axis=0))
    hcat = jnp.concatenate([hr, hi], axis=1).astype(BF16)
    y_ref[...] = yi + jnp.dot(hcat, w2_ref[...], preferred_element_type=F32)
    if nchunk == 1:
        hn_ref[0] = vr
        hn_ref[1] = vi
    else:
        last = [b * nchunk + nchunk - 1 for b in range(nbatch)]
        hn_ref[0] = jnp.concatenate([vr[i:i + 1] for i in last], axis=0)
        hn_ref[1] = jnp.concatenate([vi[i:i + 1] for i in last], axis=0)


def s5_scan(u, w1, w2, ap, h0, nbatch, nchunk):
    g2, rows, width = u.shape
    nstep = ap.shape[1]
    return pl.pallas_call(
        functools.partial(_s5_kernel, nbatch=nbatch, nchunk=nchunk, width=width),
        out_shape=(jax.ShapeDtypeStruct((g2, rows, width), F32),
                   jax.ShapeDtypeStruct((g2, 2, nbatch, PAIR_STATE), F32)),
        grid=(g2,),
        in_specs=[pl.BlockSpec((None, rows, width), lambda g: (g, 0, 0)),
                  pl.BlockSpec((None, width, width + 2 * PAIR_STATE), lambda g: (g, 0, 0)),
                  pl.BlockSpec((None, 2 * PAIR_STATE, width), lambda g: (g, 0, 0)),
                  pl.BlockSpec((None, nstep, 2, PAIR_STATE), lambda g: (g, 0, 0, 0)),
                  pl.BlockSpec((None, 2, nbatch, PAIR_STATE), lambda g: (g, 0, 0, 0))],
        out_specs=(pl.BlockSpec((None, rows, width), lambda g: (g, 0, 0)),
                   pl.BlockSpec((None, 2, nbatch, PAIR_STATE), lambda g: (g, 0, 0, 0))),
        compiler_params=_params(("parallel",)),
        name="s5_scan",
    )(u, w1, w2, ap, h0)


def _s5_operators(lam_re, lam_im, log_dt, b_re, b_im, c_re, c_im, chunk, nchunk):
    hp = lax.Precision.HIGHEST
    g, p = lam_re.shape
    lam = lax.complex(lam_re, lam_im)
    dt = jnp.exp(log_dt)[:, None]
    a_bar = jnp.exp(lam * dt)
    b_bar = ((a_bar - 1.0) / lam)[..., None] * lax.complex(b_re, b_im)
    steps = jnp.arange(chunk + 1, dtype=F32)
    pw = jnp.exp((lam * dt)[:, None, :] * steps[None, :, None])
    pwr, pwi = pw.real, pw.imag
    bbr, bbi = b_bar.real, b_bar.imag
    mr = c_re[:, None] * pwr[:, :chunk, None] - c_im[:, None] * pwi[:, :chunk, None]
    mi = c_re[:, None] * pwi[:, :chunk, None] + c_im[:, None] * pwr[:, :chunk, None]
    kd = (jnp.einsum('gdcp,gpe->gdec', mr, bbr, precision=hp)
          - jnp.einsum('gdcp,gpe->gdec', mi, bbi, precision=hp))
    s_idx = jnp.arange(chunk)
    lag = s_idx[None, :] - s_idx[:, None]
    ky = jnp.where((lag >= 0)[None, :, :, None, None], kd[:, jnp.maximum(lag, 0)], 0.0)
    ky = ky.transpose(0, 1, 3, 2, 4).reshape(g, chunk * SSM_GROUP, chunk * SSM_GROUP)
    rev_r, rev_i = pwr[:, chunk - 1 - s_idx], pwi[:, chunk - 1 - s_idx]
    ks_r = (rev_r[..., None] * bbr[:, None] - rev_i[..., None] * bbi[:, None]).transpose(0, 1, 3, 2)
    ks_i = (rev_r[..., None] * bbi[:, None] + rev_i[..., None] * bbr[:, None]).transpose(0, 1, 3, 2)
    ks_r = ks_r.reshape(g, chunk * SSM_GROUP, p)
    ks_i = ks_i.reshape(g, chunk * SSM_GROUP, p)
    nr, ni = pwr[:, 1:], pwi[:, 1:]
    kh_r = (c_re[:, None] * nr[:, :, None] - c_im[:, None] * ni[:, :, None])
    kh_i = -(c_re[:, None] * ni[:, :, None] + c_im[:, None] * nr[:, :, None])
    kh_r = kh_r.transpose(0, 3, 1, 2).reshape(g, p, chunk * SSM_GROUP)
    kh_i = kh_i.transpose(0, 3, 1, 2).reshape(g, p, chunk * SSM_GROUP)

    width = chunk * SSM_GROUP
    g2 = g // PAIR
    z_y = jnp.zeros((g2, width, width), F32)
    z_s = jnp.zeros((g2, width, p), F32)
    ky2, ksr2, ksi2 = (t.reshape(g2, PAIR, *t.shape[1:]) for t in (ky, ks_r, ks_i))
    top = jnp.concatenate([ky2[:, 0], z_y, ksr2[:, 0], z_s, ksi2[:, 0], z_s], axis=2)
    bot = jnp.concatenate([z_y, ky2[:, 1], z_s, ksr2[:, 1], z_s, ksi2[:, 1]], axis=2)
    w1 = jnp.concatenate([top, bot], axis=1).astype(BF16)
    khr2, khi2 = (t.reshape(g2, PAIR, p, width) for t in (kh_r, kh_i))
    z_h = jnp.zeros((g2, p, width), F32)
    w2 = jnp.concatenate([
        jnp.concatenate([khr2[:, 0], z_h], axis=2), jnp.concatenate([z_h, khr2[:, 1]], axis=2),
        jnp.concatenate([khi2[:, 0], z_h], axis=2), jnp.concatenate([z_h, khi2[:, 1]], axis=2)], axis=1).astype(BF16)
    nstep = max(1, (nchunk - 1).bit_length())
    mult = (chunk * (2 ** jnp.arange(nstep))).astype(F32)
    ap = jnp.exp((lam * dt)[:, None, :] * mult[None, :, None])
    ap = jnp.stack([ap.real.reshape(g2, PAIR, nstep, p).transpose(0, 2, 1, 3).reshape(g2, nstep, PAIR_STATE),
                    ap.imag.reshape(g2, PAIR, nstep, p).transpose(0, 2, 1, 3).reshape(g2, nstep, PAIR_STATE)], axis=2)
    return w1, w2, ap


def s5_apply(x, h0_re, h0_im, ssm, chunk):
    bsz, t, _ = x.shape
    nchunk = t // chunk
    g2 = N_SSM_GROUPS // PAIR
    width = PAIR * chunk * SSM_GROUP
    w1, w2, ap = _s5_operators(*ssm, chunk, nchunk)
    u = x.astype(BF16).reshape(bsz, nchunk, chunk, g2, PAIR, SSM_GROUP)
    u = u.transpose(3, 0, 1, 4, 2, 5).reshape(g2, bsz * nchunk, width)
    pair_state = lambda s: s.reshape(bsz, g2, PAIR_STATE).transpose(1, 0, 2)
    h0 = jnp.stack([pair_state(h0_re), pair_state(h0_im)], axis=1)
    y, hn = s5_scan(u, w1, w2, ap, h0, bsz, nchunk)
    y = y.reshape(g2, bsz, nchunk, PAIR, chunk, SSM_GROUP).transpose(1, 2, 4, 0, 3, 5).reshape(bsz, t, D_MODEL)
    unpair = lambda s: s.transpose(1, 0, 2).reshape(bsz, N_SSM_GROUPS, SSM_STATE)
    return y, unpair(hn[:, 0]), unpair(hn[:, 1])


def _gelu_tanh(v):
    inner = math.sqrt(2.0 / math.pi) * (v + 0.044715 * (v * v * v))
    return 0.5 * v * (1.0 + jnp.tanh(inner))


GROUPS_PER_TILE = LANES // SSM_GROUP
PAIRS_PER_TILE = GROUPS_PER_TILE // PAIR
S5_CHUNK = 16


def _block_transpose8(arrs):
    arrs = list(arrs)
    lane = lax.broadcasted_iota(jnp.int32, arrs[0].shape, 1)
    for d in (4, 2, 1):
        clear = (lane & (d * SSM_GROUP)) == 0
        nxt = list(arrs)
        for i in range(GROUPS_PER_TILE):
            if i & d:
                continue
            lo, hi = arrs[i], arrs[i + d]
            nxt[i] = jnp.where(clear, lo, pltpu.roll(hi, d * SSM_GROUP, axis=1))
            nxt[i + d] = jnp.where(clear, pltpu.roll(lo, LANES - d * SSM_GROUP, axis=1), hi)
        arrs = nxt
    return arrs


def _s5_prompt_kernel(x_ref, w1_ref, w2_ref, ap_ref, d_ref, g_ref, gb_ref, hn_ref, *, nbatch, nchunk):
    rows = nbatch * nchunk
    half = S5_CHUNK // 2
    xs = [x_ref[pl.ds(s, rows, stride=S5_CHUNK), :] for s in range(S5_CHUNK)]
    v0 = _block_transpose8(xs[:half])
    v1 = _block_transpose8(xs[half:])
    krow = lax.broadcasted_iota(jnp.int32, (rows, PAIR_STATE), 0) & (nchunk - 1)
    width = PAIR * S5_CHUNK * SSM_GROUP
    y0 = [None] * GROUPS_PER_TILE
    y1 = [None] * GROUPS_PER_TILE
    for q in range(PAIRS_PER_TILE):
        ga, gb = PAIR * q, PAIR * q + 1
        u = jnp.concatenate([v0[ga], v1[ga], v0[gb], v1[gb]], axis=1).astype(BF16)
        z = jnp.dot(u, w1_ref[q], preferred_element_type=F32)
        vr = z[:, width:width + PAIR_STATE]
        vi = z[:, width + PAIR_STATE:]
        step = 0
        while (1 << step) < nchunk:
            d = 1 << step
            ar = ap_ref[q, step, 0:1]
            ai = ap_ref[q, step, 1:2]
            keep = krow >= d
            sr = jnp.where(keep, pltpu.roll(vr, d, axis=0), 0.0)
            si = jnp.where(keep, pltpu.roll(vi, d, axis=0), 0.0)
            vr, vi = vr + (ar * sr - ai * si), vi + (ar * si + ai * sr)
            step += 1
        first = krow == 0
        hr = jnp.where(first, 0.0, pltpu.roll(vr, 1, axis=0))
        hi = jnp.where(first, 0.0, pltpu.roll(vi, 1, axis=0))
        hcat = jnp.concatenate([hr, hi], axis=1).astype(BF16)
        y = z[:, :width] + jnp.dot(hcat, w2_ref[q], preferred_element_type=F32)
        y0[ga], y1[ga] = y[:, 0:LANES], y[:, LANES:2 * LANES]
        y0[gb], y1[gb] = y[:, 2 * LANES:3 * LANES], y[:, 3 * LANES:]
        last = [b * nchunk + nchunk - 1 for b in range(nbatch)]
        hn_ref[q, 0] = jnp.concatenate([vr[i:i + 1] for i in last], axis=0)
        hn_ref[q, 1] = jnp.concatenate([vi[i:i + 1] for i in last], axis=0)
    ys = _block_transpose8(y0) + _block_transpose8(y1)
    dskip = d_ref[...]
    for s in range(S5_CHUNK):
        g_ref[pl.ds(s, rows, stride=S5_CHUNK), :] = _gelu_tanh(ys[s] + dskip * xs[s])
    gb_ref[...] = g_ref[...].astype(BF16)


def s5_prompt(x, w1, w2, ap, d, layer_i, nbatch, seq):
    n, dm = x.shape
    nchunk = seq // S5_CHUNK
    n_p = nbatch * seq
    nstep = ap.shape[1]
    g2 = w1.shape[0]
    width = PAIR * S5_CHUNK * SSM_GROUP
    tile = pl.BlockSpec((n_p, LANES), lambda j: (0, j))
    return pl.pallas_call(
        functools.partial(_s5_prompt_kernel, nbatch=nbatch, nchunk=nchunk),
        out_shape=(jax.ShapeDtypeStruct((n, dm), F32), jax.ShapeDtypeStruct((n, dm), BF16),
                   jax.ShapeDtypeStruct((g2, 2, nbatch, PAIR_STATE), F32)),
        grid=(dm // LANES,),
        in_specs=[tile,
                  pl.BlockSpec((PAIRS_PER_TILE, width, width + 2 * PAIR_STATE), lambda j: (j, 0, 0)),
                  pl.BlockSpec((PAIRS_PER_TILE, 2 * PAIR_STATE, width), lambda j: (j, 0, 0)),
                  pl.BlockSpec((PAIRS_PER_TILE, nstep, 2, PAIR_STATE), lambda j: (j, 0, 0, 0)),
                  pl.BlockSpec((None, 1, LANES), lambda j: (layer_i, 0, j))],
        out_specs=(tile, tile,
                   pl.BlockSpec((PAIRS_PER_TILE, 2, nbatch, PAIR_STATE), lambda j: (j, 0, 0, 0))),
        compiler_params=_params(("parallel",)),
        name="s5_prompt",
    )(x, w1, w2, ap, d.reshape(d.shape[0], 1, dm))


def _gelu_skip_kernel(y_ref, x_ref, d_ref, o_ref):
    o_ref[...] = _gelu_tanh(y_ref[...] + d_ref[...] * x_ref[...])


def gelu_skip(y, x, d, layer_i, *, tm):
    n, dm = x.shape
    row = pl.BlockSpec((tm, dm), lambda i: (i, 0))
    return pl.pallas_call(
        _gelu_skip_kernel,
        out_shape=jax.ShapeDtypeStruct((n, dm), F32),
        grid=(n // tm,),
        in_specs=[row, row, pl.BlockSpec((None, 1, dm), lambda i: (layer_i, 0, 0))],
        out_specs=row,
        compiler_params=_params(("parallel",)),
        name="gelu_skip",
    )(y, x, d.reshape(d.shape[0], 1, dm))


ROUTE_COLS = LANES


def _split_bf16(v):
    hi = v.astype(BF16)
    lo = (v - hi.astype(F32)).astype(BF16)
    return hi, lo


def _router_kernel(x_ref, w_ref, gate_ref, eid_ref):
    xh, xl = _split_bf16(x_ref[...])
    wh, wl = _split_bf16(w_ref[...])
    dot = lambda a, b: jnp.dot(a, b, preferred_element_type=F32)
    logits = dot(xh, wh) + (dot(xh, wl) + dot(xl, wh))
    lane = lax.broadcasted_iota(jnp.int32, logits.shape, 1).astype(F32)
    neg = -jnp.inf
    big = float(ROUTE_COLS)
    lg = jnp.where(lane < N_EXPERT_GROUPS, logits, neg)
    m = jnp.max(lg, axis=1, keepdims=True)
    grp = jnp.min(jnp.where(lg == m, lane, big), axis=1, keepdims=True)
    gate_g = 1.0 / jnp.sum(jnp.exp(lg - m), axis=1, keepdims=True)
    lo = N_EXPERT_GROUPS + grp * EXPERTS_PER_GROUP
    le = jnp.where((lane >= lo) & (lane < lo + EXPERTS_PER_GROUP), logits, neg)
    t1 = jnp.max(le, axis=1, keepdims=True)
    i1 = jnp.min(jnp.where(le == t1, lane, big), axis=1, keepdims=True)
    le2 = jnp.where(lane == i1, neg, le)
    t2 = jnp.max(le2, axis=1, keepdims=True)
    i2 = jnp.min(jnp.where(le2 == t2, lane, big), axis=1, keepdims=True)
    e = jnp.exp(t2 - t1)
    p1 = 1.0 / (1.0 + e)
    p2 = e / (1.0 + e)
    gate_ref[...] = jnp.where(lane == 0.0, gate_g * p1, jnp.where(lane == 1.0, gate_g * p2, 0.0))
    eid = jnp.where(lane == 0.0, i1, jnp.where(lane == 1.0, i2, float(N_EXPERT_GROUPS))) - N_EXPERT_GROUPS
    eid_ref[...] = eid.astype(jnp.int32)


def router(x, w_route, layer, *, tm):
    n, d = x.shape
    return pl.pallas_call(
        _router_kernel,
        out_shape=(jax.ShapeDtypeStruct((n, ROUTE_COLS), F32), jax.ShapeDtypeStruct((n, ROUTE_COLS), jnp.int32)),
        grid=(n // tm,),
        in_specs=[pl.BlockSpec((tm, d), lambda i: (i, 0)),
                  pl.BlockSpec((None, d, ROUTE_COLS), lambda i: (layer, 0, 0))],
        out_specs=(pl.BlockSpec((tm, ROUTE_COLS), lambda i: (i, 0)), pl.BlockSpec((tm, ROUTE_COLS), lambda i: (i, 0))),
        compiler_params=_params(("parallel",)),
        name="router",
    )(x, w_route)


MOE_SUB = 256
MOE_NSUB = 3
MOE_TM = MOE_SUB * MOE_NSUB
MOE_TF = 256
MOE_TN = 512
MOE_NF = D_EXPERT // MOE_TF
MOE_NN = D_MODEL // MOE_TN


def _moe_kernel(be_ref, ns_ref, bi_ref, x_ref, wg_ref, wu_ref, wd_ref, o_ref, xb_ref, hid_ref):
    blk = pl.program_id(0)
    s = pl.program_id(1)
    nsub = ns_ref[blk]
    live = nsub > 0

    @pl.when(live & (s == 0))
    def _():
        words = x_ref[...]
        xb_ref[:, :HALF_D] = pltpu.bitcast(words << 16, F32).astype(BF16)
        xb_ref[:, HALF_D:] = pltpu.bitcast(words & -65536, F32).astype(BF16)

    @pl.when(live & (s < MOE_NF))
    def _():
        wgb = wg_ref[...].astype(BF16)
        wub = wu_ref[...].astype(BF16)
        for sb in range(MOE_NSUB):
            rows = slice(sb * MOE_SUB, (sb + 1) * MOE_SUB)

            @pl.when(sb < nsub)
            def _():
                xr = xb_ref[rows, :]
                gate = jnp.dot(xr, wgb, preferred_element_type=F32)
                up = jnp.dot(xr, wub, preferred_element_type=F32)
                hid_ref[s, rows, :] = ((gate * _sigmoid(gate)) * up).astype(BF16)

    @pl.when(live & (s >= MOE_NF))
    def _():
        wdb = wd_ref[...].astype(BF16)
        for sb in range(MOE_NSUB):
            rows = slice(sb * MOE_SUB, (sb + 1) * MOE_SUB)

            @pl.when(sb < nsub)
            def _():
                hid = jnp.concatenate([hid_ref[f, rows, :] for f in range(MOE_NF)], axis=1)
                o_ref[rows, :] = jnp.dot(hid, wdb, preferred_element_type=F32)

            @pl.when(sb >= nsub)
            def _():
                o_ref[rows, :] = jnp.zeros((MOE_SUB, MOE_TN), F32)


def moe_experts(xs, blk_e, blk_nsub, blk_idx, w_gate, w_up, w_down, layer):
    p = xs.shape[0]
    d = D_MODEL
    up_idx = lambda s, ns, b: jnp.where(ns[b] > 0, jnp.minimum(s, MOE_NF - 1), MOE_NF - 1)
    dn_idx = lambda s, ns, b: jnp.where(ns[b] > 0, jnp.maximum(s - MOE_NF, 0), MOE_NN - 1)
    return pl.pallas_call(
        _moe_kernel,
        out_shape=jax.ShapeDtypeStruct((p, d), F32),
        grid_spec=pltpu.PrefetchScalarGridSpec(
            num_scalar_prefetch=3,
            grid=(p // MOE_TM, MOE_NF + MOE_NN),
            in_specs=[
                pl.BlockSpec((MOE_TM, HALF_D), lambda b, s, be, ns, bi: (bi[b], 0)),
                pl.BlockSpec((None, None, d, MOE_TF), lambda b, s, be, ns, bi: (layer, be[b], 0, up_idx(s, ns, b))),
                pl.BlockSpec((None, None, d, MOE_TF), lambda b, s, be, ns, bi: (layer, be[b], 0, up_idx(s, ns, b))),
                pl.BlockSpec((None, None, D_EXPERT, MOE_TN),
                             lambda b, s, be, ns, bi: (layer, be[b], 0, dn_idx(s, ns, b)))],
            out_specs=pl.BlockSpec((MOE_TM, MOE_TN), lambda b, s, be, ns, bi: (bi[b], dn_idx(s, ns, b))),
            scratch_shapes=[pltpu.VMEM((MOE_TM, d), BF16), pltpu.VMEM((MOE_NF, MOE_TM, MOE_TF), BF16)],
        ),
        compiler_params=_params(("arbitrary", "arbitrary")),
        name="moe_experts",
    )(blk_e, blk_nsub, blk_idx, xs, w_gate, w_up, w_down)


def moe_dispatch(eids, n):
    nslot = n * 2
    eid = eids.reshape(-1)
    experts = jnp.arange(N_EXPERTS, dtype=jnp.int32)
    onehot = (eid[:, None] == experts[None, :]).astype(jnp.int32)
    seen = jnp.cumsum(onehot, axis=0)
    rank = jnp.sum(seen * onehot, axis=1) - 1
    counts = seen[-1]
    nblk = (counts + MOE_TM - 1) // MOE_TM
    bend = jnp.cumsum(nblk)
    bstart = bend - nblk
    dest = jnp.sum(onehot * bstart[None, :], axis=1) * MOE_TM + rank
    n_blocks = nslot // MOE_TM + N_EXPERTS
    row_tok = jnp.zeros((n_blocks * MOE_TM,), jnp.int32).at[dest].set(jnp.arange(nslot, dtype=jnp.int32) // 2)
    pos = dest.reshape(n, 2)
    blk = jnp.arange(n_blocks, dtype=jnp.int32)
    n_used = bend[-1]
    blk_idx = jnp.minimum(blk, n_used - 1)
    blk_e = jnp.sum((blk_idx[:, None] >= bend[None, :]).astype(jnp.int32), axis=1)
    rows_left = counts[blk_e] - (blk_idx - bstart[blk_e]) * MOE_TM
    nsub = (jnp.clip(rows_left, 0, MOE_TM) + MOE_SUB - 1) // MOE_SUB
    blk_nsub = jnp.where(blk < n_used, nsub, 0).astype(jnp.int32)
    return pos, row_tok, blk_e.astype(jnp.int32), blk_nsub, blk_idx


def hier_moe_ln(x, x_packed, layer, w_route, w_gate, w_up, w_down, ln_g, ln_b, *, tm):
    n = x.shape[0]
    gates, eids = router(x, w_route, layer, tm=tm)
    pos, row_tok, blk_e, blk_nsub, blk_idx = moe_dispatch(eids[:, :2], n)
    xs = x_packed[row_tok]
    ys = moe_experts(xs, blk_e, blk_nsub, blk_idx, w_gate, w_up, w_down, layer)
    return ln_moe_combine(x, ys[pos[:, 0]], ys[pos[:, 1]], gates, ln_g, ln_b, layer, tm=tm)


ROW_TILE = 320
MM_TM = 1040
MM_TN = 512


def _rows_after(full, tail, start):
    return lax.dynamic_update_slice(full, tail.astype(full.dtype), (start, 0))


def kernel(x_prompt, x_sample, cache_k, cache_v, state_conv, state_ssm_re, state_ssm_im, w_in, w_out, attn_sinks, conv_w, ssm_lambda_re, ssm_lambda_im, ssm_log_dt, ssm_b_re, ssm_b_im, ssm_c_re, ssm_c_im, ssm_d, w_glu, w_o, ln1_g, ln1_b, ln2_g, ln2_b, w_router_group, w_router_expert, w_gate, w_up, w_down):
    batch, seq, _ = x_prompt.shape
    db, n_new, _ = x_sample.shape
    n_p = batch * seq
    n_s = db * n_new
    n = n_p + n_s
    assert n % ROW_TILE == 0 and n % MM_TM == 0 and n_new <= T_PAD

    w_route = jnp.concatenate(
        [w_router_group, w_router_expert,
         jnp.zeros((DEPTH, D_MODEL, ROUTE_COLS - N_EXPERT_GROUPS - N_EXPERTS), F32)], axis=-1)

    x = jnp.concatenate([x_prompt.reshape(n_p, D_MODEL), x_sample.reshape(n_s, D_MODEL)], axis=0)
    xb = x.astype(BF16)
    k_p, v_p, conv_p, re_p, im_p = [], [], [], [], []
    k_s, v_s, conv_s, re_s, im_s = [], [], [], [], []
    for layer in range(DEPTH):
        i = layer // 2
        if layer % 2 == 0:
            h = matmul([xb], w_in, i, tm=MM_TM, tn=MM_TN, out_dtype=F32)
            attn = attn_prompt(h, attn_sinks[i], batch, seq)
            gconv, cu_tail = conv_prompt(h, conv_w, i, batch, seq)
            tails = [h[b * seq + seq - WINDOW:(b + 1) * seq, D_ATT:D_ATT + 2 * D_KV] for b in range(batch)]
            kv_tail = jnp.stack(tails)
            k_p.append(kv_tail[:, :, :D_KV].reshape(batch, WINDOW, N_KV_HEADS, HEAD_DIM))
            v_p.append(kv_tail[:, :, D_KV:].reshape(batch, WINDOW, N_KV_HEADS, HEAD_DIM))
            conv_p.append(cu_tail.reshape(batch, 8, D_CONV)[:, 8 - (CONV_W - 1):])
            hs = h[n_p:].reshape(db, n_new, D_IN_EVEN)
            k_new = hs[:, :, D_ATT:D_ATT + D_KV]
            v_new = hs[:, :, D_ATT + D_KV:D_ATT + 2 * D_KV]
            pad_rows = jnp.zeros((db, KC_PAD - WINDOW - n_new, D_KV), F32)
            kc = jnp.concatenate([cache_k[i].reshape(db, WINDOW, D_KV), k_new, pad_rows], axis=1)
            vc = jnp.concatenate([cache_v[i].reshape(db, WINDOW, D_KV), v_new, pad_rows], axis=1)
            q_s = jnp.pad(hs[:, :, :D_ATT], ((0, 0), (0, T_PAD - n_new), (0, 0)))
            attn_s = attn_sample(q_s, kc, vc, attn_sinks[i], n_new)[:, :n_new].reshape(n_s, D_ATT)
            k_s.append(kc[:, n_new:n_new + WINDOW].reshape(db, WINDOW, N_KV_HEADS, HEAD_DIM))
            v_s.append(vc[:, n_new:n_new + WINDOW].reshape(db, WINDOW, N_KV_HEADS, HEAD_DIM))
            off = D_ATT + 2 * D_KV
            tmaj = lambda a: a.transpose(1, 0, 2)
            gconv_s, st_s = conv_sample(tmaj(hs[:, :, off:off + D_CONV]),
                                        tmaj(hs[:, :, off + D_CONV:off + 2 * D_CONV]),
                                        tmaj(hs[:, :, off + 2 * D_CONV:]),
                                        tmaj(state_conv[i]), conv_w, i)
            conv_s.append(tmaj(st_s))
            attn = _rows_after(attn, attn_s, n_p)
            gconv = _rows_after(gconv, tmaj(gconv_s).reshape(n_s, D_CONV), n_p)
            mixed = matmul([attn, gconv], w_out, i, tm=MM_TM, tn=MM_TN, out_dtype=F32)
        else:
            ssm = (ssm_lambda_re[i], ssm_lambda_im[i], ssm_log_dt[i], ssm_b_re[i], ssm_b_im[i],
                   ssm_c_re[i], ssm_c_im[i])
            w1, w2, ap = _s5_operators(*ssm, S5_CHUNK, seq // S5_CHUNK)
            g, gb, hn = s5_prompt(x, w1, w2, ap, ssm_d, i, batch, seq)
            unpair = lambda s: s.transpose(1, 0, 2).reshape(batch, N_SSM_GROUPS, SSM_STATE)
            re_p.append(unpair(hn[:, 0]))
            im_p.append(unpair(hn[:, 1]))
            x_s = x[n_p:]
            y_s, nre_s, nim_s = s5_apply(x_s.reshape(db, n_new, D_MODEL), state_ssm_re[i], state_ssm_im[i],
                                         ssm, n_new)
            re_s.append(nre_s)
            im_s.append(nim_s)
            g_s = gelu_skip(y_s.reshape(n_s, D_MODEL), x_s, ssm_d, i, tm=n_s)
            g = _rows_after(g, g_s, n_p)
            gb = _rows_after(gb, g_s, n_p)
            z = matmul([gb], w_glu, i, tm=MM_TM, tn=MM_TN, out_dtype=BF16, glu_gate=g)
            mixed = matmul([z], w_o, i, tm=MM_TM, tn=MM_TN, out_dtype=F32)
        x, x_packed = ln_residual(x, mixed, ln1_g, ln1_b, layer, tm=ROW_TILE)
        x, xb = hier_moe_ln(x, x_packed, layer, w_route, w_gate, w_up, w_down, ln2_g, ln2_b, tm=ROW_TILE)

    y_prompt = x[:n_p].reshape(batch, seq, D_MODEL)
    y_sample = x[n_p:].reshape(db, n_new, D_MODEL)
    st = jnp.stack
    return (y_prompt, y_sample, st(k_p), st(v_p), st(conv_p), st(re_p), st(im_p),
            st(k_s), st(v_s), st(conv_s), st(re_s), st(im_s))
```

```python
import functools
import math

import jax
import jax.numpy as jnp
from jax import lax
from jax.experimental import pallas as pl
from jax.experimental.pallas import tpu as pltpu

F32 = jnp.float32
BF16 = jnp.bfloat16

D_MODEL = 4096
DEPTH = 4
N_HEADS = 32
N_KV_HEADS = 4
Q_PER_KV = N_HEADS // N_KV_HEADS
HEAD_DIM = 64
WINDOW = 128
D_ATT = N_HEADS * HEAD_DIM
D_KV = N_KV_HEADS * HEAD_DIM
D_CONV = D_MODEL // 2
CONV_W = 3
D_IN_EVEN = D_ATT + 2 * D_KV + 3 * D_CONV
SSM_GROUP = 16
N_SSM_GROUPS = D_MODEL // SSM_GROUP
SSM_STATE = 64
N_EXPERT_GROUPS = 4
EXPERTS_PER_GROUP = 8
N_EXPERTS = N_EXPERT_GROUPS * EXPERTS_PER_GROUP
D_EXPERT = D_MODEL // 4
ALPHA = (2 * DEPTH) ** 0.25
LN_EPS = 1e-5

LANES = 128
VMEM_LIMIT = 58 * 1024 * 1024

COL_BLK = 256
K_BLK = D_ATT // COL_BLK
V_BLK = (D_ATT + D_KV) // COL_BLK
B_BLK = (D_ATT + 2 * D_KV) // COL_BLK
C_BLK = B_BLK + D_CONV // COL_BLK
U_BLK = C_BLK + D_CONV // COL_BLK


def _params(sem):
    return pltpu.CompilerParams(dimension_semantics=sem, vmem_limit_bytes=VMEM_LIMIT)


def _sigmoid(x):
    return 1.0 / (1.0 + jnp.exp(-x))


def _mm_kernel(*refs, n_a, glu):
    a_refs = refs[:n_a]
    w_ref = refs[n_a]
    g_ref = refs[n_a + 1] if glu else None
    o_ref = refs[-1]
    wb = w_ref[...].astype(BF16)
    acc = None
    off = 0
    for a_ref in a_refs:
        kp = a_ref.shape[1]
        part = jnp.dot(a_ref[...], wb[off:off + kp, :], preferred_element_type=F32)
        acc = part if acc is None else acc + part
        off += kp
    if glu:
        acc = g_ref[...] * _sigmoid(acc)
    o_ref[...] = acc.astype(o_ref.dtype)


def matmul(a_parts, w, layer, *, tm, tn, out_dtype, glu_gate=None):
    m = a_parts[0].shape[0]
    kdim, n = w.shape[-2:]
    assert m % tm == 0 and n % tn == 0 and sum(a.shape[1] for a in a_parts) == kdim
    in_specs = [pl.BlockSpec((tm, a.shape[1]), lambda i, j: (i, 0), pipeline_mode=pl.Buffered(1)) for a in a_parts]
    in_specs.append(pl.BlockSpec((None, kdim, tn), lambda i, j: (layer, 0, j)))
    args = list(a_parts) + [w]
    if glu_gate is not None:
        in_specs.append(pl.BlockSpec((tm, tn), lambda i, j: (i, j)))
        args.append(glu_gate)
    return pl.pallas_call(
        functools.partial(_mm_kernel, n_a=len(a_parts), glu=glu_gate is not None),
        out_shape=jax.ShapeDtypeStruct((m, n), out_dtype),
        grid=(m // tm, n // tn),
        in_specs=in_specs,
        out_specs=pl.BlockSpec((tm, tn), lambda i, j: (i, j)),
        compiler_params=_params(("parallel", "parallel")),
        name="matmul",
    )(*args)


def _layer_norm_rows(v, g, b):
    mu = jnp.mean(v, axis=-1, keepdims=True)
    d = v - mu
    var = jnp.mean(d * d, axis=-1, keepdims=True)
    return d * lax.rsqrt(var + LN_EPS) * g + b


HALF_D = D_MODEL // 2


def _ln_kernel(x_ref, y_ref, g_ref, b_ref, o_ref, op_ref):
    out = _layer_norm_rows(ALPHA * x_ref[...] + y_ref[...], g_ref[...], b_ref[...])
    o_ref[...] = out
    bits = pltpu.bitcast(out.astype(BF16).astype(F32), jnp.int32)
    op_ref[...] = bits[:, HALF_D:] | lax.shift_right_logical(bits[:, :HALF_D], 16)


def ln_residual(x, y, g, b, layer, *, tm):
    n, d = x.shape
    row = pl.BlockSpec((tm, d), lambda i: (i, 0))
    par = pl.BlockSpec((None, 1, d), lambda i: (layer, 0, 0))
    return pl.pallas_call(
        _ln_kernel,
        out_shape=(jax.ShapeDtypeStruct((n, d), F32), jax.ShapeDtypeStruct((n, HALF_D), jnp.int32)),
        grid=(n // tm,),
        in_specs=[row, row, par, par],
        out_specs=(row, pl.BlockSpec((tm, HALF_D), lambda i: (i, 0))),
        compiler_params=_params(("parallel",)),
        name="ln_residual",
    )(x, y, g.reshape(DEPTH, 1, d), b.reshape(DEPTH, 1, d))


def _ln_moe_kernel(x_ref, y1_ref, y2_ref, gate_ref, g_ref, b_ref, o_ref, ob_ref):
    gate = gate_ref[...]
    ffn = y1_ref[...] * gate[:, 0:1] + y2_ref[...] * gate[:, 1:2]
    out = _layer_norm_rows(ALPHA * x_ref[...] + ffn, g_ref[...], b_ref[...])
    o_ref[...] = out
    ob_ref[...] = out.astype(BF16)


def ln_moe_combine(x, y1, y2, gates, g, b, layer, *, tm):
    n, d = x.shape
    row = pl.BlockSpec((tm, d), lambda i: (i, 0))
    par = pl.BlockSpec((None, 1, d), lambda i: (layer, 0, 0))
    return pl.pallas_call(
        _ln_moe_kernel,
        out_shape=(jax.ShapeDtypeStruct((n, d), F32), jax.ShapeDtypeStruct((n, d), BF16)),
        grid=(n // tm,),
        in_specs=[row, row, row, pl.BlockSpec((tm, LANES), lambda i: (i, 0)), par, par],
        out_specs=(row, row),
        compiler_params=_params(("parallel",)),
        name="ln_moe_combine",
    )(x, y1, y2, gates, g.reshape(DEPTH, 1, d), b.reshape(DEPTH, 1, d))


def _softmax_sink_pv(s, valid, sink_col, vband):
    s = jnp.where(valid, s, -jnp.inf)
    m = jnp.maximum(jnp.max(s, axis=1, keepdims=True), sink_col)
    p = jnp.exp(s - m)
    denom = jnp.sum(p, axis=1, keepdims=True) + jnp.exp(sink_col - m)
    o = jnp.dot(p.astype(BF16), vband, preferred_element_type=F32)
    return o / denom


def _attn_prompt_kernel(sink_ref, q_ref, kc_ref, kp_ref, vc_ref, vp_ref, o_ref):
    nb = pl.program_id(1)
    rows = Q_PER_KV * WINDOW
    r = lax.broadcasted_iota(jnp.int32, (rows, 2 * WINDOW), 0) & (WINDOW - 1)
    c = lax.broadcasted_iota(jnp.int32, (rows, 2 * WINDOW), 1)
    valid = (c > r) & (c <= r + WINDOW) & ((nb > 0) | (c >= WINDOW))
    for j in range(N_KV_HEADS):
        heads = range(j * Q_PER_KV, (j + 1) * Q_PER_KV)
        qg = jnp.concatenate([q_ref[:, h * HEAD_DIM:(h + 1) * HEAD_DIM] for h in heads], axis=0).astype(BF16)
        cols = slice(j * HEAD_DIM, (j + 1) * HEAD_DIM)
        kband = jnp.concatenate([kp_ref[:, cols], kc_ref[:, cols]], axis=0).astype(BF16)
        vband = jnp.concatenate([vp_ref[:, cols], vc_ref[:, cols]], axis=0).astype(BF16)
        s = lax.dot_general(qg, kband, (((1,), (1,)), ((), ())), preferred_element_type=F32) * (HEAD_DIM ** -0.5)
        sink_col = jnp.concatenate([jnp.full((WINDOW, 1), sink_ref[h], F32) for h in heads], axis=0)
        o = _softmax_sink_pv(s, valid, sink_col, vband)
        o = jnp.concatenate([o[i * WINDOW:(i + 1) * WINDOW] for i in range(Q_PER_KV)], axis=1)
        o_ref[:, j * Q_PER_KV * HEAD_DIM:(j + 1) * Q_PER_KV * HEAD_DIM] = o.astype(o_ref.dtype)


def attn_prompt(h, sinks, batch, seq):
    nb = seq // WINDOW
    cur = lambda col: pl.BlockSpec((WINDOW, COL_BLK), lambda b, n, s: (b * nb + n, col))
    prev = lambda col: pl.BlockSpec((WINDOW, COL_BLK), lambda b, n, s: (b * nb + jnp.maximum(n - 1, 0), col))
    return pl.pallas_call(
        _attn_prompt_kernel,
        out_shape=jax.ShapeDtypeStruct((h.shape[0], D_ATT), BF16),
        grid_spec=pltpu.PrefetchScalarGridSpec(
            num_scalar_prefetch=1,
            grid=(batch, nb),
            in_specs=[pl.BlockSpec((WINDOW, D_ATT), lambda b, n, s: (b * nb + n, 0)),
                      cur(K_BLK), prev(K_BLK), cur(V_BLK), prev(V_BLK)],
            out_specs=pl.BlockSpec((WINDOW, D_ATT), lambda b, n, s: (b * nb + n, 0)),
        ),
        compiler_params=_params(("parallel", "arbitrary")),
        name="attn_prompt",
    )(sinks, h, h, h, h, h)


T_PAD = 8
KC_PAD = WINDOW + T_PAD


def _attn_sample_kernel(sink_ref, q_ref, kc_ref, vc_ref, o_ref, *, n_new):
    rows = Q_PER_KV * T_PAD
    t = lax.broadcasted_iota(jnp.int32, (rows, KC_PAD), 0) & (T_PAD - 1)
    c = lax.broadcasted_iota(jnp.int32, (rows, KC_PAD), 1)
    valid = (c > t) & (c <= t + WINDOW) & (c < WINDOW + n_new)
    for j in range(N_KV_HEADS):
        heads = range(j * Q_PER_KV, (j + 1) * Q_PER_KV)
        qg = jnp.concatenate([q_ref[:, h * HEAD_DIM:(h + 1) * HEAD_DIM] for h in heads], axis=0).astype(BF16)
        cols = slice(j * HEAD_DIM, (j + 1) * HEAD_DIM)
        kband = kc_ref[:, cols].astype(BF16)
        vband = vc_ref[:, cols].astype(BF16)
        s = lax.dot_general(qg, kband, (((1,), (1,)), ((), ())), preferred_element_type=F32) * (HEAD_DIM ** -0.5)
        sink_col = jnp.concatenate([jnp.full((T_PAD, 1), sink_ref[h], F32) for h in heads], axis=0)
        o = _softmax_sink_pv(s, valid, sink_col, vband)
        o = jnp.concatenate([o[i * T_PAD:(i + 1) * T_PAD] for i in range(Q_PER_KV)], axis=1)
        o_ref[:, j * Q_PER_KV * HEAD_DIM:(j + 1) * Q_PER_KV * HEAD_DIM] = o.astype(o_ref.dtype)


def attn_sample(q, kc, vc, sinks, n_new):
    db = q.shape[0]
    return pl.pallas_call(
        functools.partial(_attn_sample_kernel, n_new=n_new),
        out_shape=jax.ShapeDtypeStruct((db, T_PAD, D_ATT), BF16),
        grid_spec=pltpu.PrefetchScalarGridSpec(
            num_scalar_prefetch=1,
            grid=(db,),
            in_specs=[pl.BlockSpec((None, T_PAD, D_ATT), lambda b, s: (b, 0, 0)),
                      pl.BlockSpec((None, KC_PAD, D_KV), lambda b, s: (b, 0, 0)),
                      pl.BlockSpec((None, KC_PAD, D_KV), lambda b, s: (b, 0, 0))],
            out_specs=pl.BlockSpec((None, T_PAD, D_ATT), lambda b, s: (b, 0, 0)),
        ),
        compiler_params=_params(("parallel",)),
        name="attn_sample",
    )(sinks, q, kc, vc)


def _conv_prompt_kernel(b_ref, c_ref, u_ref, w_ref, o_ref, st_ref):
    cu = c_ref[...] * u_ref[...]
    t = lax.broadcasted_iota(jnp.int32, cu.shape, 0)
    sh1 = jnp.where(t >= 1, pltpu.roll(cu, 1, axis=0), 0.0)
    sh2 = jnp.where(t >= 2, pltpu.roll(cu, 2, axis=0), 0.0)
    w = w_ref[...]
    y = sh2 * w[0:1] + sh1 * w[1:2] + cu * w[2:3]
    o_ref[...] = (b_ref[...] * y).astype(o_ref.dtype)
    st_ref[...] = cu[cu.shape[0] - 8:]


def conv_prompt(h, conv_w, layer_i, batch, seq):
    nj = D_CONV // COL_BLK
    col = lambda base: pl.BlockSpec((seq, COL_BLK), lambda b, j: (b, base + j))
    return pl.pallas_call(
        _conv_prompt_kernel,
        out_shape=(jax.ShapeDtypeStruct((h.shape[0], D_CONV), BF16),
                   jax.ShapeDtypeStruct((batch * 8, D_CONV), F32)),
        grid=(batch, nj),
        in_specs=[col(B_BLK), col(C_BLK), col(U_BLK),
                  pl.BlockSpec((None, CONV_W, COL_BLK), lambda b, j: (layer_i, 0, j))],
        out_specs=(pl.BlockSpec((seq, COL_BLK), lambda b, j: (b, j)),
                   pl.BlockSpec((8, COL_BLK), lambda b, j: (b, j))),
        compiler_params=_params(("parallel", "parallel")),
        name="conv_prompt",
    )(h, h, h, conv_w)


def _conv_sample_kernel(b_ref, c_ref, u_ref, buf_ref, w_ref, o_ref, st_ref, *, n_new):
    w = w_ref[...]
    full = [buf_ref[i] for i in range(CONV_W - 1)] + [c_ref[t] * u_ref[t] for t in range(n_new)]
    for t in range(n_new):
        y = full[t] * w[0:1] + full[t + 1] * w[1:2] + full[t + 2] * w[2:3]
        o_ref[t] = (b_ref[t] * y).astype(o_ref.dtype)
    for i in range(CONV_W - 1):
        st_ref[i] = full[n_new + i]


def conv_sample(bg, cg, ug, buf, conv_w, layer_i):
    n_new, db, _ = bg.shape
    full = lambda a: pl.BlockSpec(a.shape, lambda g: (0,) * a.ndim)
    return pl.pallas_call(
        functools.partial(_conv_sample_kernel, n_new=n_new),
        out_shape=(jax.ShapeDtypeStruct((n_new, db, D_CONV), BF16),
                   jax.ShapeDtypeStruct((CONV_W - 1, db, D_CONV), F32)),
        grid=(1,),
        in_specs=[full(bg), full(cg), full(ug), full(buf),
                  pl.BlockSpec((None, CONV_W, D_CONV), lambda g: (layer_i, 0, 0))],
        out_specs=(pl.BlockSpec((n_new, db, D_CONV), lambda g: (0, 0, 0)),
                   pl.BlockSpec((CONV_W - 1, db, D_CONV), lambda g: (0, 0, 0))),
        compiler_params=_params(("arbitrary",)),
        name="conv_sample",
    )(bg, cg, ug, buf, conv_w)


PAIR = 2
PAIR_STATE = PAIR * SSM_STATE


def _s5_kernel(u_ref, w1_ref, w2t_ref, ap_ref, h0_ref, y_ref, hn_ref, *, nbatch, nchunk, width):
    rows = nbatch * nchunk
    z = jnp.dot(u_ref[...], w1_ref[...], preferred_element_type=F32)
    yi = z[:, :width]
    vr = z[:, width:width + PAIR_STATE]
    vi = z[:, width + PAIR_STATE:]
    h0r = h0_ref[0]
    h0i = h0_ref[1]
    if nchunk == 1:
        h0r_rows, h0i_rows = h0r, h0i
    else:
        h0r_rows = jnp.concatenate([jnp.broadcast_to(h0r[b:b + 1], (nchunk, PAIR_STATE)) for b in range(nbatch)], 0)
        h0i_rows = jnp.concatenate([jnp.broadcast_to(h0i[b:b + 1], (nchunk, PAIR_STATE)) for b in range(nbatch)], 0)
    krow = lax.broadcasted_iota(jnp.int32, (rows, PAIR_STATE), 0) & (nchunk - 1)
    ar = ap_ref[0, 0:1]
    ai = ap_ref[0, 1:2]
    first = krow == 0
    vr = vr + jnp.where(first, ar * h0r_rows - ai * h0i_rows, 0.0)
    vi = vi + jnp.where(first, ar * h0i_rows + ai * h0r_rows, 0.0)
    step = 0
    while (1 << step) < nchunk:
        d = 1 << step
        ar = ap_ref[step, 0:1]
        ai = ap_ref[step, 1:2]
        keep = krow >= d
        sr = jnp.where(keep, pltpu.roll(vr, d, axis=0), 0.0)
        si = jnp.where(keep, pltpu.roll(vi, d, axis=0), 0.0)
        vr, vi = vr + (ar * sr - ai * si), vi + (ar * si + ai * sr)
        step += 1
    if nchunk == 1:
        hr, hi = h0r_rows, h0i_rows
    else:
        hr = jnp.where(first, h0r_rows, pltpu.roll(vr, 1, axis=0))
        hi = jnp.where(first, h0i_rows, pltpu.roll(vi, 1, axis=0))
    hcat = jnp.concatenate([hr, hi], axis=1).astype(BF16)
    y_ref[...] = yi + lax.dot_general(hcat, w2t_ref[...], (((1,), (1,)), ((), ())), preferred_element_type=F32)
    if nchunk == 1:
        hn_ref[0] = vr
        hn_ref[1] = vi
    else:
        last = [b * nchunk + nchunk - 1 for b in range(nbatch)]
        hn_ref[0] = jnp.concatenate([vr[i:i + 1] for i in last], axis=0)
        hn_ref[1] = jnp.concatenate([vi[i:i + 1] for i in last], axis=0)


def s5_scan(u, w1, w2t, ap, h0, nbatch, nchunk):
    g2, rows, width = u.shape
    nstep = ap.shape[1]
    return pl.pallas_call(
        functools.partial(_s5_kernel, nbatch=nbatch, nchunk=nchunk, width=width),
        out_shape=(jax.ShapeDtypeStruct((g2, rows, width), F32),
                   jax.ShapeDtypeStruct((g2, 2, nbatch, PAIR_STATE), F32)),
        grid=(g2,),
        in_specs=[pl.BlockSpec((None, rows, width), lambda g: (g, 0, 0)),
                  pl.BlockSpec((None, width, width + 2 * PAIR_STATE), lambda g: (g, 0, 0)),
                  pl.BlockSpec((None, width, 2 * PAIR_STATE), lambda g: (g, 0, 0)),
                  pl.BlockSpec((None, nstep, 2, PAIR_STATE), lambda g: (g, 0, 0, 0)),
                  pl.BlockSpec((None, 2, nbatch, PAIR_STATE), lambda g: (g, 0, 0, 0))],
        out_specs=(pl.BlockSpec((None, rows, width), lambda g: (g, 0, 0)),
                   pl.BlockSpec((None, 2, nbatch, PAIR_STATE), lambda g: (g, 0, 0, 0))),
        compiler_params=_params(("parallel",)),
        name="s5_scan",
    )(u, w1, w2t, ap, h0)


def s5_apply(x, h0_re, h0_im, ssm, chunk):
    bsz, t, _ = x.shape
    nchunk = t // chunk
    g2 = N_SSM_GROUPS // PAIR
    width = PAIR * chunk * SSM_GROUP
    w1, w2t, ap = s5_operators(*ssm, chunk, nchunk)
    u = x.astype(BF16).reshape(bsz, nchunk, chunk, g2, PAIR, SSM_GROUP)
    u = u.transpose(3, 0, 1, 4, 2, 5).reshape(g2, bsz * nchunk, width)
    pair_state = lambda s: s.reshape(bsz, g2, PAIR_STATE).transpose(1, 0, 2)
    h0 = jnp.stack([pair_state(h0_re), pair_state(h0_im)], axis=1)
    y, hn = s5_scan(u, w1, w2t, ap, h0, bsz, nchunk)
    y = y.reshape(g2, bsz, nchunk, PAIR, chunk, SSM_GROUP).transpose(1, 2, 4, 0, 3, 5).reshape(bsz, t, D_MODEL)
    unpair = lambda s: s.transpose(1, 0, 2).reshape(bsz, N_SSM_GROUPS, SSM_STATE)
    return y, unpair(hn[:, 0]), unpair(hn[:, 1])


def _gelu_tanh(v):
    inner = math.sqrt(2.0 / math.pi) * (v + 0.044715 * (v * v * v))
    return 0.5 * v * (1.0 + jnp.tanh(inner))


GROUPS_PER_TILE = LANES // SSM_GROUP
PAIRS_PER_TILE = GROUPS_PER_TILE // PAIR
S5_CHUNK = 16


def _block_transpose8(arrs):
    arrs = list(arrs)
    lane = lax.broadcasted_iota(jnp.int32, arrs[0].shape, 1)
    for d in (4, 2, 1):
        clear = (lane & (d * SSM_GROUP)) == 0
        nxt = list(arrs)
        for i in range(GROUPS_PER_TILE):
            if i & d:
                continue
            lo, hi = arrs[i], arrs[i + d]
            nxt[i] = jnp.where(clear, lo, pltpu.roll(hi, d * SSM_GROUP, axis=1))
            nxt[i + d] = jnp.where(clear, pltpu.roll(lo, LANES - d * SSM_GROUP, axis=1), hi)
        arrs = nxt
    return arrs


def _shift_lanes(a, k):
    if k == 0:
        return a
    return jnp.concatenate([jnp.zeros((a.shape[0], k), a.dtype), a[:, :a.shape[1] - k]], axis=1)


def _dot_nt_3pass(a, b):
    ah, al = _split_bf16(a)
    bh, bl = _split_bf16(b)
    nt = lambda u, v: lax.dot_general(u, v, (((1,), (1,)), ((), ())), preferred_element_type=F32)
    return nt(ah, bh) + (nt(ah, bl) + nt(al, bh))


def _s5_ops_kernel(cr_ref, ci_ref, btr_ref, bti_ref, pwr_ref, pwi_ref, w1_ref, w2t_ref, *, chunk):
    nl = chunk
    wg = nl * SSM_GROUP
    rep = lambda a, lo: jnp.concatenate(
        [jnp.broadcast_to(a[d:d + 1], (SSM_GROUP, SSM_STATE)) for d in range(lo, lo + nl)], axis=0)
    tile = lambda a: jnp.concatenate([a] * nl, axis=0)
    ky, ksr, ksi, khr, khi = [], [], [], [], []
    for g in range(PAIR):
        cr, ci = tile(cr_ref[g]), tile(ci_ref[g])
        btr, bti = btr_ref[g], bti_ref[g]
        pwr, pwi = pwr_ref[g], pwi_ref[g]
        p0r, p0i = rep(pwr, 0), rep(pwi, 0)
        m0r = cr * p0r - ci * p0i
        m0i = cr * p0i + ci * p0r
        r = _dot_nt_3pass(btr, m0r) - _dot_nt_3pass(bti, m0i)
        ky.append(jnp.concatenate([_shift_lanes(r, s * SSM_GROUP) for s in range(nl)], axis=0))
        sr, si = [], []
        for s in range(nl):
            pr = pwr[nl - 1 - s:nl - s]
            pi = pwi[nl - 1 - s:nl - s]
            sr.append(btr * pr - bti * pi)
            si.append(btr * pi + bti * pr)
        ksr.append(jnp.concatenate(sr, axis=0))
        ksi.append(jnp.concatenate(si, axis=0))
        p1r, p1i = rep(pwr, 1), rep(pwi, 1)
        khr.append(cr * p1r - ci * p1i)
        khi.append(-(cr * p1i + ci * p1r))
    zy = jnp.zeros((wg, wg), F32)
    zs = jnp.zeros((wg, SSM_STATE), F32)
    top = jnp.concatenate([ky[0], zy, ksr[0], zs, ksi[0], zs], axis=1)
    bot = jnp.concatenate([zy, ky[1], zs, ksr[1], zs, ksi[1]], axis=1)
    w1_ref[...] = jnp.concatenate([top, bot], axis=0).astype(BF16)
    w2t_ref[...] = jnp.concatenate([jnp.concatenate([khr[0], zs, khi[0], zs], axis=1),
                                    jnp.concatenate([zs, khr[1], zs, khi[1]], axis=1)], axis=0).astype(BF16)


def s5_operators(lam_re, lam_im, log_dt, b_re, b_im, c_re, c_im, chunk, nchunk):
    g, p = lam_re.shape
    g2 = g // PAIR
    wg = chunk * SSM_GROUP
    ldt = lax.complex(lam_re, lam_im) * jnp.exp(log_dt)[:, None]
    a_bar = jnp.exp(ldt)
    b_bar = ((a_bar - 1.0) / lax.complex(lam_re, lam_im))[..., None] * lax.complex(b_re, b_im)
    pw = jnp.exp(ldt[:, None, :] * jnp.arange(chunk + 1, dtype=F32)[None, :, None])
    bt = b_bar.transpose(0, 2, 1)
    grp = lambda a: pl.BlockSpec((PAIR,) + a.shape[1:], lambda q: (q, 0, 0))
    args = (c_re, c_im, bt.real, bt.imag, pw.real, pw.imag)
    w1, w2t = pl.pallas_call(
        functools.partial(_s5_ops_kernel, chunk=chunk),
        out_shape=(jax.ShapeDtypeStruct((g2, PAIR * wg, PAIR * wg + 2 * PAIR_STATE), BF16),
                   jax.ShapeDtypeStruct((g2, PAIR * wg, 2 * PAIR_STATE), BF16)),
        grid=(g2,),
        in_specs=[grp(a) for a in args],
        out_specs=(pl.BlockSpec((None, PAIR * wg, PAIR * wg + 2 * PAIR_STATE), lambda q: (q, 0, 0)),
                   pl.BlockSpec((None, PAIR * wg, 2 * PAIR_STATE), lambda q: (q, 0, 0))),
        compiler_params=_params(("parallel",)),
        name="s5_operators",
    )(*args)
    nstep = max(1, (nchunk - 1).bit_length())
    mult = (chunk * (2 ** jnp.arange(nstep))).astype(F32)
    ap = jnp.exp(ldt[:, None, :] * mult[None, :, None])
    pair = lambda t: t.reshape(g2, PAIR, nstep, p).transpose(0, 2, 1, 3).reshape(g2, nstep, PAIR_STATE)
    return w1, w2t, jnp.stack([pair(ap.real), pair(ap.imag)], axis=2)


def _s5_prompt_kernel(x_ref, w1_ref, w2t_ref, ap_ref, d_ref, g_ref, gb_ref, hn_ref, *, nbatch, nchunk):
    rows = nbatch * nchunk
    half = S5_CHUNK // 2
    xs = [x_ref[pl.ds(s, rows, stride=S5_CHUNK), :] for s in range(S5_CHUNK)]
    v0 = _block_transpose8(xs[:half])
    v1 = _block_transpose8(xs[half:])
    krow = lax.broadcasted_iota(jnp.int32, (rows, PAIR_STATE), 0) & (nchunk - 1)
    width = PAIR * S5_CHUNK * SSM_GROUP
    y0 = [None] * GROUPS_PER_TILE
    y1 = [None] * GROUPS_PER_TILE
    for q in range(PAIRS_PER_TILE):
        ga, gb = PAIR * q, PAIR * q + 1
        u = jnp.concatenate([v0[ga], v1[ga], v0[gb], v1[gb]], axis=1).astype(BF16)
        z = jnp.dot(u, w1_ref[q], preferred_element_type=F32)
        vr = z[:, width:width + PAIR_STATE]
        vi = z[:, width + PAIR_STATE:]
        step = 0
        while (1 << step) < nchunk:
            d = 1 << step
            ar = ap_ref[q, step, 0:1]
            ai = ap_ref[q, step, 1:2]
            keep = krow >= d
            sr = jnp.where(keep, pltpu.roll(vr, d, axis=0), 0.0)
            si = jnp.where(keep, pltpu.roll(vi, d, axis=0), 0.0)
            vr, vi = vr + (ar * sr - ai * si), vi + (ar * si + ai * sr)
            step += 1
        first = krow == 0
        hr = jnp.where(first, 0.0, pltpu.roll(vr, 1, axis=0))
        hi = jnp.where(first, 0.0, pltpu.roll(vi, 1, axis=0))
        hcat = jnp.concatenate([hr, hi], axis=1).astype(BF16)
        y = z[:, :width] + lax.dot_general(hcat, w2t_ref[q], (((1,), (1,)), ((), ())), preferred_element_type=F32)
        y0[ga], y1[ga] = y[:, 0:LANES], y[:, LANES:2 * LANES]
        y0[gb], y1[gb] = y[:, 2 * LANES:3 * LANES], y[:, 3 * LANES:]
        last = [b * nchunk + nchunk - 1 for b in range(nbatch)]
        hn_ref[q, 0] = jnp.concatenate([vr[i:i + 1] for i in last], axis=0)
        hn_ref[q, 1] = jnp.concatenate([vi[i:i + 1] for i in last], axis=0)
    ys = _block_transpose8(y0) + _block_transpose8(y1)
    dskip = d_ref[...]
    for s in range(S5_CHUNK):
        g_ref[pl.ds(s, rows, stride=S5_CHUNK), :] = _gelu_tanh(ys[s] + dskip * xs[s])
    gb_ref[...] = g_ref[...].astype(BF16)


def s5_prompt(x, w1, w2t, ap, d, layer_i, nbatch, seq):
    n, dm = x.shape
    nchunk = seq // S5_CHUNK
    n_p = nbatch * seq
    nstep = ap.shape[1]
    g2 = w1.shape[0]
    width = PAIR * S5_CHUNK * SSM_GROUP
    tile = pl.BlockSpec((n_p, LANES), lambda j: (0, j))
    return pl.pallas_call(
        functools.partial(_s5_prompt_kernel, nbatch=nbatch, nchunk=nchunk),
        out_shape=(jax.ShapeDtypeStruct((n, dm), F32), jax.ShapeDtypeStruct((n, dm), BF16),
                   jax.ShapeDtypeStruct((g2, 2, nbatch, PAIR_STATE), F32)),
        grid=(dm // LANES,),
        in_specs=[tile,
                  pl.BlockSpec((PAIRS_PER_TILE, width, width + 2 * PAIR_STATE), lambda j: (j, 0, 0)),
                  pl.BlockSpec((PAIRS_PER_TILE, width, 2 * PAIR_STATE), lambda j: (j, 0, 0)),
                  pl.BlockSpec((PAIRS_PER_TILE, nstep, 2, PAIR_STATE), lambda j: (j, 0, 0, 0)),
                  pl.BlockSpec((None, 1, LANES), lambda j: (layer_i, 0, j))],
        out_specs=(tile, tile,
                   pl.BlockSpec((PAIRS_PER_TILE, 2, nbatch, PAIR_STATE), lambda j: (j, 0, 0, 0))),
        compiler_params=_params(("parallel",)),
        name="s5_prompt",
    )(x, w1, w2t, ap, d.reshape(d.shape[0], 1, dm))


def _gelu_skip_kernel(y_ref, x_ref, d_ref, o_ref):
    o_ref[...] = _gelu_tanh(y_ref[...] + d_ref[...] * x_ref[...])


def gelu_skip(y, x, d, layer_i, *, tm):
    n, dm = x.shape
    row = pl.BlockSpec((tm, dm), lambda i: (i, 0))
    return pl.pallas_call(
        _gelu_skip_kernel,
        out_shape=jax.ShapeDtypeStruct((n, dm), F32),
        grid=(n // tm,),
        in_specs=[row, row, pl.BlockSpec((None, 1, dm), lambda i: (layer_i, 0, 0))],
        out_specs=row,
        compiler_params=_params(("parallel",)),
        name="gelu_skip",
    )(y, x, d.reshape(d.shape[0], 1, dm))


ROUTE_COLS = LANES


def _split_bf16(v):
    hi = v.astype(BF16)
    lo = (v - hi.astype(F32)).astype(BF16)
    return hi, lo


def _router_kernel(x_ref, w_ref, gate_ref, eid_ref):
    xh, xl = _split_bf16(x_ref[...])
    wh, wl = _split_bf16(w_ref[...])
    dot = lambda a, b: jnp.dot(a, b, preferred_element_type=F32)
    logits = dot(xh, wh) + (dot(xh, wl) + dot(xl, wh))
    lane = lax.broadcasted_iota(jnp.int32, logits.shape, 1).astype(F32)
    neg = -jnp.inf
    big = float(ROUTE_COLS)
    lg = jnp.where(lane < N_EXPERT_GROUPS, logits, neg)
    m = jnp.max(lg, axis=1, keepdims=True)
    grp = jnp.min(jnp.where(lg == m, lane, big), axis=1, keepdims=True)
    gate_g = 1.0 / jnp.sum(jnp.exp(lg - m), axis=1, keepdims=True)
    lo = N_EXPERT_GROUPS + grp * EXPERTS_PER_GROUP
    le = jnp.where((lane >= lo) & (lane < lo + EXPERTS_PER_GROUP), logits, neg)
    t1 = jnp.max(le, axis=1, keepdims=True)
    i1 = jnp.min(jnp.where(le == t1, lane, big), axis=1, keepdims=True)
    le2 = jnp.where(lane == i1, neg, le)
    t2 = jnp.max(le2, axis=1, keepdims=True)
    i2 = jnp.min(jnp.where(le2 == t2, lane, big), axis=1, keepdims=True)
    e = jnp.exp(t2 - t1)
    p1 = 1.0 / (1.0 + e)
    p2 = e / (1.0 + e)
    gate_ref[...] = jnp.where(lane == 0.0, gate_g * p1, jnp.where(lane == 1.0, gate_g * p2, 0.0))
    eid = jnp.where(lane == 0.0, i1, jnp.where(lane == 1.0, i2, float(N_EXPERT_GROUPS))) - N_EXPERT_GROUPS
    eid_ref[...] = eid.astype(jnp.int32)


def router(x, w_route, layer, *, tm):
    n, d = x.shape
    return pl.pallas_call(
        _router_kernel,
        out_shape=(jax.ShapeDtypeStruct((n, ROUTE_COLS), F32), jax.ShapeDtypeStruct((n, ROUTE_COLS), jnp.int32)),
        grid=(n // tm,),
        in_specs=[pl.BlockSpec((tm, d), lambda i: (i, 0)),
                  pl.BlockSpec((None, d, ROUTE_COLS), lambda i: (layer, 0, 0))],
        out_specs=(pl.BlockSpec((tm, ROUTE_COLS), lambda i: (i, 0)), pl.BlockSpec((tm, ROUTE_COLS), lambda i: (i, 0))),
        compiler_params=_params(("parallel",)),
        name="router",
    )(x, w_route)


MOE_SUB = 256
MOE_NSUB = 3
MOE_TM = MOE_SUB * MOE_NSUB
MOE_TF = 256
MOE_TN = 512
MOE_NF = D_EXPERT // MOE_TF
MOE_NN = D_MODEL // MOE_TN


def _moe_kernel(be_ref, ns_ref, bi_ref, x_ref, wg_ref, wu_ref, wd_ref, o_ref, xb_ref, hid_ref):
    blk = pl.program_id(0)
    s = pl.program_id(1)
    nsub = ns_ref[blk]
    live = nsub > 0

    @pl.when(live & (s == 0))
    def _():
        words = x_ref[...]
        xb_ref[:, :HALF_D] = pltpu.bitcast(words << 16, F32).astype(BF16)
        xb_ref[:, HALF_D:] = pltpu.bitcast(words & -65536, F32).astype(BF16)

    @pl.when(live & (s < MOE_NF))
    def _():
        wgb = wg_ref[...].astype(BF16)
        wub = wu_ref[...].astype(BF16)
        for sb in range(MOE_NSUB):
            rows = slice(sb * MOE_SUB, (sb + 1) * MOE_SUB)

            @pl.when(sb < nsub)
            def _():
                xr = xb_ref[rows, :]
                gate = jnp.dot(xr, wgb, preferred_element_type=F32)
                up = jnp.dot(xr, wub, preferred_element_type=F32)
                hid_ref[s, rows, :] = ((gate * _sigmoid(gate)) * up).astype(BF16)

    @pl.when(live & (s >= MOE_NF))
    def _():
        wdb = wd_ref[...].astype(BF16)
        for sb in range(MOE_NSUB):
            rows = slice(sb * MOE_SUB, (sb + 1) * MOE_SUB)

            @pl.when(sb < nsub)
            def _():
                hid = jnp.concatenate([hid_ref[f, rows, :] for f in range(MOE_NF)], axis=1)
                o_ref[rows, :] = jnp.dot(hid, wdb, preferred_element_type=F32)

            @pl.when(sb >= nsub)
            def _():
                o_ref[rows, :] = jnp.zeros((MOE_SUB, MOE_TN), F32)


def moe_experts(xs, blk_e, blk_nsub, blk_idx, w_gate, w_up, w_down, layer):
    p = xs.shape[0]
    d = D_MODEL
    up_idx = lambda s, ns, b: jnp.where(ns[b] > 0, jnp.minimum(s, MOE_NF - 1), MOE_NF - 1)
    dn_idx = lambda s, ns, b: jnp.where(ns[b] > 0, jnp.maximum(s - MOE_NF, 0), MOE_NN - 1)
    return pl.pallas_call(
        _moe_kernel,
        out_shape=jax.ShapeDtypeStruct((p, d), F32),
        grid_spec=pltpu.PrefetchScalarGridSpec(
            num_scalar_prefetch=3,
            grid=(p // MOE_TM, MOE_NF + MOE_NN),
            in_specs=[
                pl.BlockSpec((MOE_TM, HALF_D), lambda b, s, be, ns, bi: (bi[b], 0)),
                pl.BlockSpec((None, None, d, MOE_TF), lambda b, s, be, ns, bi: (layer, be[b], 0, up_idx(s, ns, b))),
                pl.BlockSpec((None, None, d, MOE_TF), lambda b, s, be, ns, bi: (layer, be[b], 0, up_idx(s, ns, b))),
                pl.BlockSpec((None, None, D_EXPERT, MOE_TN),
                             lambda b, s, be, ns, bi: (layer, be[b], 0, dn_idx(s, ns, b)))],
            out_specs=pl.BlockSpec((MOE_TM, MOE_TN), lambda b, s, be, ns, bi: (bi[b], dn_idx(s, ns, b))),
            scratch_shapes=[pltpu.VMEM((MOE_TM, d), BF16), pltpu.VMEM((MOE_NF, MOE_TM, MOE_TF), BF16)],
        ),
        compiler_params=_params(("arbitrary", "arbitrary")),
        name="moe_experts",
    )(blk_e, blk_nsub, blk_idx, xs, w_gate, w_up, w_down)


def moe_dispatch(eids, n):
    nslot = n * 2
    eid = eids.reshape(-1)
    experts = jnp.arange(N_EXPERTS, dtype=jnp.int32)
    onehot = (eid[:, None] == experts[None, :]).astype(jnp.int32)
    seen = jnp.cumsum(onehot, axis=0)
    rank = jnp.sum(seen * onehot, axis=1) - 1
    counts = seen[-1]
    nblk = (counts + MOE_TM - 1) // MOE_TM
    bend = jnp.cumsum(nblk)
    bstart = bend - nblk
    dest = jnp.sum(onehot * bstart[None, :], axis=1) * MOE_TM + rank
    n_blocks = nslot // MOE_TM + N_EXPERTS
    row_tok = jnp.zeros((n_blocks * MOE_TM,), jnp.int32).at[dest].set(jnp.arange(nslot, dtype=jnp.int32) // 2)
    pos = dest.reshape(n, 2)
    blk = jnp.arange(n_blocks, dtype=jnp.int32)
    n_used = bend[-1]
    blk_idx = jnp.minimum(blk, n_used - 1)
    blk_e = jnp.sum((blk_idx[:, None] >= bend[None, :]).astype(jnp.int32), axis=1)
    rows_left = counts[blk_e] - (blk_idx - bstart[blk_e]) * MOE_TM
    nsub = (jnp.clip(rows_left, 0, MOE_TM) + MOE_SUB - 1) // MOE_SUB
    blk_nsub = jnp.where(blk < n_used, nsub, 0).astype(jnp.int32)
    return pos, row_tok, blk_e.astype(jnp.int32), blk_nsub, blk_idx


def hier_moe_ln(x, x_packed, layer, w_route, w_gate, w_up, w_down, ln_g, ln_b, *, tm):
    n = x.shape[0]
    gates, eids = router(x, w_route, layer, tm=tm)
    pos, row_tok, blk_e, blk_nsub, blk_idx = moe_dispatch(eids[:, :2], n)
    xs = x_packed[row_tok]
    ys = moe_experts(xs, blk_e, blk_nsub, blk_idx, w_gate, w_up, w_down, layer)
    return ln_moe_combine(x, ys[pos[:, 0]], ys[pos[:, 1]], gates, ln_g, ln_b, layer, tm=tm)


ROW_TILE = 320
MM_TM = 2080
MM_TN = 256


def _rows_after(full, tail, start):
    return lax.dynamic_update_slice(full, tail.astype(full.dtype), (start, 0))


def kernel(x_prompt, x_sample, cache_k, cache_v, state_conv, state_ssm_re, state_ssm_im, w_in, w_out, attn_sinks, conv_w, ssm_lambda_re, ssm_lambda_im, ssm_log_dt, ssm_b_re, ssm_b_im, ssm_c_re, ssm_c_im, ssm_d, w_glu, w_o, ln1_g, ln1_b, ln2_g, ln2_b, w_router_group, w_router_expert, w_gate, w_up, w_down):
    batch, seq, _ = x_prompt.shape
    db, n_new, _ = x_sample.shape
    n_p = batch * seq
    n_s = db * n_new
    n = n_p + n_s
    assert n % ROW_TILE == 0 and n % MM_TM == 0 and n_new <= T_PAD

    w_route = jnp.concatenate(
        [w_router_group, w_router_expert,
         jnp.zeros((DEPTH, D_MODEL, ROUTE_COLS - N_EXPERT_GROUPS - N_EXPERTS), F32)], axis=-1)

    x = jnp.concatenate([x_prompt.reshape(n_p, D_MODEL), x_sample.reshape(n_s, D_MODEL)], axis=0)
    xb = x.astype(BF16)
    k_p, v_p, conv_p, re_p, im_p = [], [], [], [], []
    k_s, v_s, conv_s, re_s, im_s = [], [], [], [], []
    for layer in range(DEPTH):
        i = layer // 2
        if layer % 2 == 0:
            h = matmul([xb], w_in, i, tm=MM_TM, tn=MM_TN, out_dtype=F32)
            attn = attn_prompt(h, attn_sinks[i], batch, seq)
            gconv, cu_tail = conv_prompt(h, conv_w, i, batch, seq)
            tails = [h[b * seq + seq - WINDOW:(b + 1) * seq, D_ATT:D_ATT + 2 * D_KV] for b in range(batch)]
            kv_tail = jnp.stack(tails)
            k_p.append(kv_tail[:, :, :D_KV].reshape(batch, WINDOW, N_KV_HEADS, HEAD_DIM))
            v_p.append(kv_tail[:, :, D_KV:].reshape(batch, WINDOW, N_KV_HEADS, HEAD_DIM))
            conv_p.append(cu_tail.reshape(batch, 8, D_CONV)[:, 8 - (CONV_W - 1):])
            hs = h[n_p:].reshape(db, n_new, D_IN_EVEN)
            k_new = hs[:, :, D_ATT:D_ATT + D_KV]
            v_new = hs[:, :, D_ATT + D_KV:D_ATT + 2 * D_KV]
            pad_rows = jnp.zeros((db, KC_PAD - WINDOW - n_new, D_KV), F32)
            kc = jnp.concatenate([cache_k[i].reshape(db, WINDOW, D_KV), k_new, pad_rows], axis=1)
            vc = jnp.concatenate([cache_v[i].reshape(db, WINDOW, D_KV), v_new, pad_rows], axis=1)
            q_s = jnp.pad(hs[:, :, :D_ATT], ((0, 0), (0, T_PAD - n_new), (0, 0)))
            attn_s = attn_sample(q_s, kc, vc, attn_sinks[i], n_new)[:, :n_new].reshape(n_s, D_ATT)
            k_s.append(kc[:, n_new:n_new + WINDOW].reshape(db, WINDOW, N_KV_HEADS, HEAD_DIM))
            v_s.append(vc[:, n_new:n_new + WINDOW].reshape(db, WINDOW, N_KV_HEADS, HEAD_DIM))
            off = D_ATT + 2 * D_KV
            tmaj = lambda a: a.transpose(1, 0, 2)
            gconv_s, st_s = conv_sample(tmaj(hs[:, :, off:off + D_CONV]),
                                        tmaj(hs[:, :, off + D_CONV:off + 2 * D_CONV]),
                                        tmaj(hs[:, :, off + 2 * D_CONV:]),
                                        tmaj(state_conv[i]), conv_w, i)
            conv_s.append(tmaj(st_s))
            attn = _rows_after(attn, attn_s, n_p)
            gconv = _rows_after(gconv, tmaj(gconv_s).reshape(n_s, D_CONV), n_p)
            mixed = matmul([attn, gconv], w_out, i, tm=MM_TM, tn=MM_TN, out_dtype=F32)
        else:
            ssm = (ssm_lambda_re[i], ssm_lambda_im[i], ssm_log_dt[i], ssm_b_re[i], ssm_b_im[i],
                   ssm_c_re[i], ssm_c_im[i])
            w1, w2t, ap = s5_operators(*ssm, S5_CHUNK, seq // S5_CHUNK)
            g, gb, hn = s5_prompt(x, w1, w2t, ap, ssm_d, i, batch, seq)
            unpair = lambda s: s.transpose(1, 0, 2).reshape(batch, N_SSM_GROUPS, SSM_STATE)
            re_p.append(unpair(hn[:, 0]))
            im_p.append(unpair(hn[:, 1]))
            x_s = x[n_p:]
            y_s, nre_s, nim_s = s5_apply(x_s.reshape(db, n_new, D_MODEL), state_ssm_re[i], state_ssm_im[i],
                                         ssm, n_new)
            re_s.append(nre_s)
            im_s.append(nim_s)
            g_s = gelu_skip(y_s.reshape(n_s, D_MODEL), x_s, ssm_d, i, tm=n_s)
            g = _rows_after(g, g_s, n_p)
            gb = _rows_after(gb, g_s, n_p)
            z = matmul([gb], w_glu, i, tm=MM_TM, tn=MM_TN, out_dtype=BF16, glu_gate=g)
            mixed = matmul([z], w_o, i, tm=MM_TM, tn=MM_TN, out_dtype=F32)
        x, x_packed = ln_residual(x, mixed, ln1_g, ln1_b, layer, tm=ROW_TILE)
        x, xb = hier_moe_ln(x, x_packed, layer, w_route, w_gate, w_up, w_down, ln2_g, ln2_b, tm=ROW_TILE)

    y_prompt = x[:n_p].reshape(batch, seq, D_MODEL)
    y_sample = x[n_p:].reshape(db, n_new, D_MODEL)
    st = jnp.stack
    return (y_prompt, y_sample, st(k_p), st(v_p), st(conv_p), st(re_p), st(im_p),
            st(k_s), st(v_s), st(conv_s), st(re_s), st(im_s))
```

```python
import functools
import math

import jax
import jax.numpy as jnp
from jax import lax
from jax.experimental import pallas as pl
from jax.experimental.pallas import tpu as pltpu

F32 = jnp.float32
BF16 = jnp.bfloat16

D_MODEL = 4096
DEPTH = 4
N_HEADS = 32
N_KV_HEADS = 4
Q_PER_KV = N_HEADS // N_KV_HEADS
HEAD_DIM = 64
WINDOW = 128
D_ATT = N_HEADS * HEAD_DIM
D_KV = N_KV_HEADS * HEAD_DIM
D_CONV = D_MODEL // 2
CONV_W = 3
D_IN_EVEN = D_ATT + 2 * D_KV + 3 * D_CONV
SSM_GROUP = 16
N_SSM_GROUPS = D_MODEL // SSM_GROUP
SSM_STATE = 64
N_EXPERT_GROUPS = 4
EXPERTS_PER_GROUP = 8
N_EXPERTS = N_EXPERT_GROUPS * EXPERTS_PER_GROUP
D_EXPERT = D_MODEL // 4
ALPHA = (2 * DEPTH) ** 0.25
LN_EPS = 1e-5

LANES = 128
VMEM_LIMIT = 58 * 1024 * 1024

COL_BLK = 256
K_BLK = D_ATT // COL_BLK
V_BLK = (D_ATT + D_KV) // COL_BLK
B_BLK = (D_ATT + 2 * D_KV) // COL_BLK
C_BLK = B_BLK + D_CONV // COL_BLK
U_BLK = C_BLK + D_CONV // COL_BLK


def _params(sem):
    return pltpu.CompilerParams(dimension_semantics=sem, vmem_limit_bytes=VMEM_LIMIT)


def _sigmoid(x):
    return 1.0 / (1.0 + jnp.exp(-x))


def _mm_kernel(*refs, n_a, glu):
    a_refs = refs[:n_a]
    w_ref = refs[n_a]
    g_ref = refs[n_a + 1] if glu else None
    o_ref = refs[-1]
    wb = w_ref[...].astype(BF16)
    acc = None
    off = 0
    for a_ref in a_refs:
        kp = a_ref.shape[1]
        part = jnp.dot(a_ref[...], wb[off:off + kp, :], preferred_element_type=F32)
        acc = part if acc is None else acc + part
        off += kp
    if glu:
        acc = g_ref[...] * _sigmoid(acc)
    o_ref[...] = acc.astype(o_ref.dtype)


def matmul(a_parts, w, layer, *, tm, tn, out_dtype, glu_gate=None):
    m = a_parts[0].shape[0]
    kdim, n = w.shape[-2:]
    assert m % tm == 0 and n % tn == 0 and sum(a.shape[1] for a in a_parts) == kdim
    in_specs = [pl.BlockSpec((tm, a.shape[1]), lambda i, j: (i, 0), pipeline_mode=pl.Buffered(1)) for a in a_parts]
    in_specs.append(pl.BlockSpec((None, kdim, tn), lambda i, j: (layer, 0, j)))
    args = list(a_parts) + [w]
    if glu_gate is not None:
        in_specs.append(pl.BlockSpec((tm, tn), lambda i, j: (i, j)))
        args.append(glu_gate)
    return pl.pallas_call(
        functools.partial(_mm_kernel, n_a=len(a_parts), glu=glu_gate is not None),
        out_shape=jax.ShapeDtypeStruct((m, n), out_dtype),
        grid=(m // tm, n // tn),
        in_specs=in_specs,
        out_specs=pl.BlockSpec((tm, tn), lambda i, j: (i, j)),
        compiler_params=_params(("parallel", "parallel")),
        name="matmul",
    )(*args)


def _layer_norm_rows(v, g, b):
    mu = jnp.mean(v, axis=-1, keepdims=True)
    d = v - mu
    var = jnp.mean(d * d, axis=-1, keepdims=True)
    return d * lax.rsqrt(var + LN_EPS) * g + b


HALF_D = D_MODEL // 2


def _ln_kernel(x_ref, y_ref, g_ref, b_ref, o_ref, op_ref):
    out = _layer_norm_rows(ALPHA * x_ref[...] + y_ref[...], g_ref[...], b_ref[...])
    o_ref[...] = out
    bits = pltpu.bitcast(out.astype(BF16).astype(F32), jnp.int32)
    op_ref[...] = bits[:, HALF_D:] | lax.shift_right_logical(bits[:, :HALF_D], 16)


def ln_residual(x, y, g, b, layer, *, tm):
    n, d = x.shape
    row = pl.BlockSpec((tm, d), lambda i: (i, 0))
    par = pl.BlockSpec((None, 1, d), lambda i: (layer, 0, 0))
    return pl.pallas_call(
        _ln_kernel,
        out_shape=(jax.ShapeDtypeStruct((n, d), F32), jax.ShapeDtypeStruct((n, HALF_D), jnp.int32)),
        grid=(n // tm,),
        in_specs=[row, row, par, par],
        out_specs=(row, pl.BlockSpec((tm, HALF_D), lambda i: (i, 0))),
        compiler_params=_params(("parallel",)),
        name="ln_residual",
    )(x, y, g.reshape(DEPTH, 1, d), b.reshape(DEPTH, 1, d))


def _ln_moe_kernel(x_ref, y1_ref, y2_ref, gate_ref, g_ref, b_ref, o_ref, ob_ref):
    gate = gate_ref[...]
    ffn = y1_ref[...] * gate[:, 0:1] + y2_ref[...] * gate[:, 1:2]
    out = _layer_norm_rows(ALPHA * x_ref[...] + ffn, g_ref[...], b_ref[...])
    o_ref[...] = out
    ob_ref[...] = out.astype(BF16)


def ln_moe_combine(x, y1, y2, gates, g, b, layer, *, tm):
    n, d = x.shape
    row = pl.BlockSpec((tm, d), lambda i: (i, 0))
    par = pl.BlockSpec((None, 1, d), lambda i: (layer, 0, 0))
    return pl.pallas_call(
        _ln_moe_kernel,
        out_shape=(jax.ShapeDtypeStruct((n, d), F32), jax.ShapeDtypeStruct((n, d), BF16)),
        grid=(n // tm,),
        in_specs=[row, row, row, pl.BlockSpec((tm, LANES), lambda i: (i, 0)), par, par],
        out_specs=(row, row),
        compiler_params=_params(("parallel",)),
        name="ln_moe_combine",
    )(x, y1, y2, gates, g.reshape(DEPTH, 1, d), b.reshape(DEPTH, 1, d))


def _softmax_sink_pv(s, valid, sink_col, vband):
    s = jnp.where(valid, s, -jnp.inf)
    m = jnp.maximum(jnp.max(s, axis=1, keepdims=True), sink_col)
    p = jnp.exp(s - m)
    denom = jnp.sum(p, axis=1, keepdims=True) + jnp.exp(sink_col - m)
    o = jnp.dot(p.astype(BF16), vband, preferred_element_type=F32)
    return o / denom


def _attn_prompt_kernel(sink_ref, q_ref, kc_ref, kp_ref, vc_ref, vp_ref, o_ref):
    nb = pl.program_id(1)
    rows = Q_PER_KV * WINDOW
    r = lax.broadcasted_iota(jnp.int32, (rows, WINDOW), 0) & (WINDOW - 1)
    c = lax.broadcasted_iota(jnp.int32, (rows, WINDOW), 1)
    from_prev = c > r
    prev_live = from_prev & (nb > 0)
    nt = lambda a, b: lax.dot_general(a, b, (((1,), (1,)), ((), ())), preferred_element_type=F32)
    for j in range(N_KV_HEADS):
        heads = range(j * Q_PER_KV, (j + 1) * Q_PER_KV)
        qg = jnp.concatenate([q_ref[:, h * HEAD_DIM:(h + 1) * HEAD_DIM] for h in heads], axis=0).astype(BF16)
        cols = slice(j * HEAD_DIM, (j + 1) * HEAD_DIM)
        s_prev = nt(qg, kp_ref[:, cols].astype(BF16)) * (HEAD_DIM ** -0.5)
        s_cur = nt(qg, kc_ref[:, cols].astype(BF16)) * (HEAD_DIM ** -0.5)
        s = jnp.where(prev_live, s_prev, jnp.where(from_prev, -jnp.inf, s_cur))
        sink_col = jnp.concatenate([jnp.full((WINDOW, 1), sink_ref[h], F32) for h in heads], axis=0)
        m = jnp.maximum(jnp.max(s, axis=1, keepdims=True), sink_col)
        p = jnp.exp(s - m)
        denom = jnp.sum(p, axis=1, keepdims=True) + jnp.exp(sink_col - m)
        o = (jnp.dot(jnp.where(from_prev, p, 0.0).astype(BF16), vp_ref[:, cols].astype(BF16),
                     preferred_element_type=F32)
             + jnp.dot(jnp.where(from_prev, 0.0, p).astype(BF16), vc_ref[:, cols].astype(BF16),
                       preferred_element_type=F32)) / denom
        o = jnp.concatenate([o[i * WINDOW:(i + 1) * WINDOW] for i in range(Q_PER_KV)], axis=1)
        o_ref[:, j * Q_PER_KV * HEAD_DIM:(j + 1) * Q_PER_KV * HEAD_DIM] = o.astype(o_ref.dtype)


def attn_prompt(h, sinks, batch, seq):
    nb = seq // WINDOW
    cur = lambda col: pl.BlockSpec((WINDOW, COL_BLK), lambda b, n, s: (b * nb + n, col))
    prev = lambda col: pl.BlockSpec((WINDOW, COL_BLK), lambda b, n, s: (b * nb + jnp.maximum(n - 1, 0), col))
    return pl.pallas_call(
        _attn_prompt_kernel,
        out_shape=jax.ShapeDtypeStruct((h.shape[0], D_ATT), BF16),
        grid_spec=pltpu.PrefetchScalarGridSpec(
            num_scalar_prefetch=1,
            grid=(batch, nb),
            in_specs=[pl.BlockSpec((WINDOW, D_ATT), lambda b, n, s: (b * nb + n, 0)),
                      cur(K_BLK), prev(K_BLK), cur(V_BLK), prev(V_BLK)],
            out_specs=pl.BlockSpec((WINDOW, D_ATT), lambda b, n, s: (b * nb + n, 0)),
        ),
        compiler_params=_params(("parallel", "arbitrary")),
        name="attn_prompt",
    )(sinks, h, h, h, h, h)


T_PAD = 8
KC_PAD = WINDOW + T_PAD


def _attn_sample_kernel(sink_ref, q_ref, kc_ref, vc_ref, o_ref, *, n_new):
    rows = Q_PER_KV * T_PAD
    t = lax.broadcasted_iota(jnp.int32, (rows, KC_PAD), 0) & (T_PAD - 1)
    c = lax.broadcasted_iota(jnp.int32, (rows, KC_PAD), 1)
    valid = (c > t) & (c <= t + WINDOW) & (c < WINDOW + n_new)
    for j in range(N_KV_HEADS):
        heads = range(j * Q_PER_KV, (j + 1) * Q_PER_KV)
        qg = jnp.concatenate([q_ref[:, h * HEAD_DIM:(h + 1) * HEAD_DIM] for h in heads], axis=0).astype(BF16)
        cols = slice(j * HEAD_DIM, (j + 1) * HEAD_DIM)
        kband = kc_ref[:, cols].astype(BF16)
        vband = vc_ref[:, cols].astype(BF16)
        s = lax.dot_general(qg, kband, (((1,), (1,)), ((), ())), preferred_element_type=F32) * (HEAD_DIM ** -0.5)
        sink_col = jnp.concatenate([jnp.full((T_PAD, 1), sink_ref[h], F32) for h in heads], axis=0)
        o = _softmax_sink_pv(s, valid, sink_col, vband)
        o = jnp.concatenate([o[i * T_PAD:(i + 1) * T_PAD] for i in range(Q_PER_KV)], axis=1)
        o_ref[:, j * Q_PER_KV * HEAD_DIM:(j + 1) * Q_PER_KV * HEAD_DIM] = o.astype(o_ref.dtype)


def attn_sample(q, kc, vc, sinks, n_new):
    db = q.shape[0]
    return pl.pallas_call(
        functools.partial(_attn_sample_kernel, n_new=n_new),
        out_shape=jax.ShapeDtypeStruct((db, T_PAD, D_ATT), BF16),
        grid_spec=pltpu.PrefetchScalarGridSpec(
            num_scalar_prefetch=1,
            grid=(db,),
            in_specs=[pl.BlockSpec((None, T_PAD, D_ATT), lambda b, s: (b, 0, 0)),
                      pl.BlockSpec((None, KC_PAD, D_KV), lambda b, s: (b, 0, 0)),
                      pl.BlockSpec((None, KC_PAD, D_KV), lambda b, s: (b, 0, 0))],
            out_specs=pl.BlockSpec((None, T_PAD, D_ATT), lambda b, s: (b, 0, 0)),
        ),
        compiler_params=_params(("parallel",)),
        name="attn_sample",
    )(sinks, q, kc, vc)


def _conv_prompt_kernel(b_ref, c_ref, u_ref, w_ref, o_ref, st_ref):
    cu = c_ref[...] * u_ref[...]
    t = lax.broadcasted_iota(jnp.int32, cu.shape, 0)
    sh1 = jnp.where(t >= 1, pltpu.roll(cu, 1, axis=0), 0.0)
    sh2 = jnp.where(t >= 2, pltpu.roll(cu, 2, axis=0), 0.0)
    w = w_ref[...]
    y = sh2 * w[0:1] + sh1 * w[1:2] + cu * w[2:3]
    o_ref[...] = (b_ref[...] * y).astype(o_ref.dtype)
    st_ref[...] = cu[cu.shape[0] - 8:]


def conv_prompt(h, conv_w, layer_i, batch, seq):
    nj = D_CONV // COL_BLK
    col = lambda base: pl.BlockSpec((seq, COL_BLK), lambda b, j: (b, base + j))
    return pl.pallas_call(
        _conv_prompt_kernel,
        out_shape=(jax.ShapeDtypeStruct((h.shape[0], D_CONV), BF16),
                   jax.ShapeDtypeStruct((batch * 8, D_CONV), F32)),
        grid=(batch, nj),
        in_specs=[col(B_BLK), col(C_BLK), col(U_BLK),
                  pl.BlockSpec((None, CONV_W, COL_BLK), lambda b, j: (layer_i, 0, j))],
        out_specs=(pl.BlockSpec((seq, COL_BLK), lambda b, j: (b, j)),
                   pl.BlockSpec((8, COL_BLK), lambda b, j: (b, j))),
        compiler_params=_params(("parallel", "parallel")),
        name="conv_prompt",
    )(h, h, h, conv_w)


def _conv_sample_kernel(b_ref, c_ref, u_ref, buf_ref, w_ref, o_ref, st_ref, *, n_new):
    w = w_ref[...]
    full = [buf_ref[i] for i in range(CONV_W - 1)] + [c_ref[t] * u_ref[t] for t in range(n_new)]
    for t in range(n_new):
        y = full[t] * w[0:1] + full[t + 1] * w[1:2] + full[t + 2] * w[2:3]
        o_ref[t] = (b_ref[t] * y).astype(o_ref.dtype)
    for i in range(CONV_W - 1):
        st_ref[i] = full[n_new + i]


def conv_sample(bg, cg, ug, buf, conv_w, layer_i):
    n_new, db, _ = bg.shape
    full = lambda a: pl.BlockSpec(a.shape, lambda g: (0,) * a.ndim)
    return pl.pallas_call(
        functools.partial(_conv_sample_kernel, n_new=n_new),
        out_shape=(jax.ShapeDtypeStruct((n_new, db, D_CONV), BF16),
                   jax.ShapeDtypeStruct((CONV_W - 1, db, D_CONV), F32)),
        grid=(1,),
        in_specs=[full(bg), full(cg), full(ug), full(buf),
                  pl.BlockSpec((None, CONV_W, D_CONV), lambda g: (layer_i, 0, 0))],
        out_specs=(pl.BlockSpec((n_new, db, D_CONV), lambda g: (0, 0, 0)),
                   pl.BlockSpec((CONV_W - 1, db, D_CONV), lambda g: (0, 0, 0))),
        compiler_params=_params(("arbitrary",)),
        name="conv_sample",
    )(bg, cg, ug, buf, conv_w)


PAIR = 2
PAIR_STATE = PAIR * SSM_STATE


def _s5_kernel(u_ref, w1_ref, w2t_ref, ap_ref, h0_ref, y_ref, hn_ref, *, nbatch, nchunk, width):
    rows = nbatch * nchunk
    z = jnp.dot(u_ref[...], w1_ref[...], preferred_element_type=F32)
    yi = z[:, :width]
    vr = z[:, width:width + PAIR_STATE]
    vi = z[:, width + PAIR_STATE:]
    h0r = h0_ref[0]
    h0i = h0_ref[1]
    if nchunk == 1:
        h0r_rows, h0i_rows = h0r, h0i
    else:
        h0r_rows = jnp.concatenate([jnp.broadcast_to(h0r[b:b + 1], (nchunk, PAIR_STATE)) for b in range(nbatch)], 0)
        h0i_rows = jnp.concatenate([jnp.broadcast_to(h0i[b:b + 1], (nchunk, PAIR_STATE)) for b in range(nbatch)], 0)
    krow = lax.broadcasted_iota(jnp.int32, (rows, PAIR_STATE), 0) & (nchunk - 1)
    ar = ap_ref[0, 0:1]
    ai = ap_ref[0, 1:2]
    first = krow == 0
    vr = vr + jnp.where(first, ar * h0r_rows - ai * h0i_rows, 0.0)
    vi = vi + jnp.where(first, ar * h0i_rows + ai * h0r_rows, 0.0)
    step = 0
    while (1 << step) < nchunk:
        d = 1 << step
        ar = ap_ref[step, 0:1]
        ai = ap_ref[step, 1:2]
        keep = krow >= d
        sr = jnp.where(keep, pltpu.roll(vr, d, axis=0), 0.0)
        si = jnp.where(keep, pltpu.roll(vi, d, axis=0), 0.0)
        vr, vi = vr + (ar * sr - ai * si), vi + (ar * si + ai * sr)
        step += 1
    if nchunk == 1:
        hr, hi = h0r_rows, h0i_rows
    else:
        hr = jnp.where(first, h0r_rows, pltpu.roll(vr, 1, axis=0))
        hi = jnp.where(first, h0i_rows, pltpu.roll(vi, 1, axis=0))
    hcat = jnp.concatenate([hr, hi], axis=1).astype(BF16)
    y_ref[...] = yi + lax.dot_general(hcat, w2t_ref[...], (((1,), (1,)), ((), ())), preferred_element_type=F32)
    if nchunk == 1:
        hn_ref[0] = vr
        hn_ref[1] = vi
    else:
        last = [b * nchunk + nchunk - 1 for b in range(nbatch)]
        hn_ref[0] = jnp.concatenate([vr[i:i + 1] for i in last], axis=0)
        hn_ref[1] = jnp.concatenate([vi[i:i + 1] for i in last], axis=0)


def s5_scan(u, w1, w2t, ap, h0, nbatch, nchunk):
    g2, rows, width = u.shape
    nstep = ap.shape[1]
    return pl.pallas_call(
        functools.partial(_s5_kernel, nbatch=nbatch, nchunk=nchunk, width=width),
        out_shape=(jax.ShapeDtypeStruct((g2, rows, width), F32),
                   jax.ShapeDtypeStruct((g2, 2, nbatch, PAIR_STATE), F32)),
        grid=(g2,),
        in_specs=[pl.BlockSpec((None, rows, width), lambda g: (g, 0, 0)),
                  pl.BlockSpec((None, width, width + 2 * PAIR_STATE), lambda g: (g, 0, 0)),
                  pl.BlockSpec((None, width, 2 * PAIR_STATE), lambda g: (g, 0, 0)),
                  pl.BlockSpec((None, nstep, 2, PAIR_STATE), lambda g: (g, 0, 0, 0)),
                  pl.BlockSpec((None, 2, nbatch, PAIR_STATE), lambda g: (g, 0, 0, 0))],
        out_specs=(pl.BlockSpec((None, rows, width), lambda g: (g, 0, 0)),
                   pl.BlockSpec((None, 2, nbatch, PAIR_STATE), lambda g: (g, 0, 0, 0))),
        compiler_params=_params(("parallel",)),
        name="s5_scan",
    )(u, w1, w2t, ap, h0)


def s5_apply(x, h0_re, h0_im, ssm, chunk):
    bsz, t, _ = x.shape
    nchunk = t // chunk
    g2 = N_SSM_GROUPS // PAIR
    width = PAIR * chunk * SSM_GROUP
    w1, w2t, ap = s5_operators(*ssm, chunk, nchunk)
    u = x.astype(BF16).reshape(bsz, nchunk, chunk, g2, PAIR, SSM_GROUP)
    u = u.transpose(3, 0, 1, 4, 2, 5).reshape(g2, bsz * nchunk, width)
    pair_state = lambda s: s.reshape(bsz, g2, PAIR_STATE).transpose(1, 0, 2)
    h0 = jnp.stack([pair_state(h0_re), pair_state(h0_im)], axis=1)
    y, hn = s5_scan(u, w1, w2t, ap, h0, bsz, nchunk)
    y = y.reshape(g2, bsz, nchunk, PAIR, chunk, SSM_GROUP).transpose(1, 2, 4, 0, 3, 5).reshape(bsz, t, D_MODEL)
    unpair = lambda s: s.transpose(1, 0, 2).reshape(bsz, N_SSM_GROUPS, SSM_STATE)
    return y, unpair(hn[:, 0]), unpair(hn[:, 1])


def _gelu_tanh(v):
    inner = math.sqrt(2.0 / math.pi) * (v + 0.044715 * (v * v * v))
    return 0.5 * v * (1.0 + jnp.tanh(inner))


GROUPS_PER_TILE = LANES // SSM_GROUP
PAIRS_PER_TILE = GROUPS_PER_TILE // PAIR
S5_CHUNK = 16


def _block_transpose8(arrs):
    arrs = list(arrs)
    lane = lax.broadcasted_iota(jnp.int32, arrs[0].shape, 1)
    for d in (4, 2, 1):
        clear = (lane & (d * SSM_GROUP)) == 0
        nxt = list(arrs)
        for i in range(GROUPS_PER_TILE):
            if i & d:
                continue
            lo, hi = arrs[i], arrs[i + d]
            nxt[i] = jnp.where(clear, lo, pltpu.roll(hi, d * SSM_GROUP, axis=1))
            nxt[i + d] = jnp.where(clear, pltpu.roll(lo, LANES - d * SSM_GROUP, axis=1), hi)
        arrs = nxt
    return arrs


def _shift_lanes(a, k):
    if k == 0:
        return a
    return jnp.concatenate([jnp.zeros((a.shape[0], k), a.dtype), a[:, :a.shape[1] - k]], axis=1)


def _dot_nt_3pass(a, b):
    ah, al = _split_bf16(a)
    bh, bl = _split_bf16(b)
    nt = lambda u, v: lax.dot_general(u, v, (((1,), (1,)), ((), ())), preferred_element_type=F32)
    return nt(ah, bh) + (nt(ah, bl) + nt(al, bh))


def _s5_ops_kernel(cr_ref, ci_ref, btr_ref, bti_ref, pwr_ref, pwi_ref, w1_ref, w2t_ref, *, chunk):
    nl = chunk
    wg = nl * SSM_GROUP
    rep = lambda a, lo: jnp.concatenate(
        [jnp.broadcast_to(a[d:d + 1], (SSM_GROUP, SSM_STATE)) for d in range(lo, lo + nl)], axis=0)
    tile = lambda a: jnp.concatenate([a] * nl, axis=0)
    for q in range(PAIRS_PER_TILE):
        _s5_pair_operators(q, cr_ref, ci_ref, btr_ref, bti_ref, pwr_ref, pwi_ref, w1_ref, w2t_ref, nl, wg, rep, tile)


def _s5_pair_operators(q, cr_ref, ci_ref, btr_ref, bti_ref, pwr_ref, pwi_ref, w1_ref, w2t_ref, nl, wg, rep, tile):
    ky, ksr, ksi, khr, khi = [], [], [], [], []
    for g in range(PAIR * q, PAIR * q + PAIR):
        cr, ci = tile(cr_ref[g]), tile(ci_ref[g])
        btr, bti = btr_ref[g], bti_ref[g]
        pwr, pwi = pwr_ref[g], pwi_ref[g]
        p0r, p0i = rep(pwr, 0), rep(pwi, 0)
        m0r = cr * p0r - ci * p0i
        m0i = cr * p0i + ci * p0r
        r = _dot_nt_3pass(btr, m0r) - _dot_nt_3pass(bti, m0i)
        ky.append(jnp.concatenate([_shift_lanes(r, s * SSM_GROUP) for s in range(nl)], axis=0))
        sr, si = [], []
        for s in range(nl):
            pr = pwr[nl - 1 - s:nl - s]
            pi = pwi[nl - 1 - s:nl - s]
            sr.append(btr * pr - bti * pi)
            si.append(btr * pi + bti * pr)
        ksr.append(jnp.concatenate(sr, axis=0))
        ksi.append(jnp.concatenate(si, axis=0))
        p1r, p1i = rep(pwr, 1), rep(pwi, 1)
        khr.append(cr * p1r - ci * p1i)
        khi.append(-(cr * p1i + ci * p1r))
    zy = jnp.zeros((wg, wg), F32)
    zs = jnp.zeros((wg, SSM_STATE), F32)
    top = jnp.concatenate([ky[0], zy, ksr[0], zs, ksi[0], zs], axis=1)
    bot = jnp.concatenate([zy, ky[1], zs, ksr[1], zs, ksi[1]], axis=1)
    w1_ref[q] = jnp.concatenate([top, bot], axis=0).astype(BF16)
    w2t_ref[q] = jnp.concatenate([jnp.concatenate([khr[0], zs, khi[0], zs], axis=1),
                                  jnp.concatenate([zs, khr[1], zs, khi[1]], axis=1)], axis=0).astype(BF16)


def s5_operators(lam_re, lam_im, log_dt, b_re, b_im, c_re, c_im, chunk, nchunk):
    g, p = lam_re.shape
    g2 = g // PAIR
    wg = chunk * SSM_GROUP
    ldt = lax.complex(lam_re, lam_im) * jnp.exp(log_dt)[:, None]
    a_bar = jnp.exp(ldt)
    b_bar = ((a_bar - 1.0) / lax.complex(lam_re, lam_im))[..., None] * lax.complex(b_re, b_im)
    pw = jnp.exp(ldt[:, None, :] * jnp.arange(chunk + 1, dtype=F32)[None, :, None])
    bt = b_bar.transpose(0, 2, 1)
    grp = lambda a: pl.BlockSpec((GROUPS_PER_TILE,) + a.shape[1:], lambda j: (j, 0, 0))
    args = (c_re, c_im, bt.real, bt.imag, pw.real, pw.imag)
    w1, w2t = pl.pallas_call(
        functools.partial(_s5_ops_kernel, chunk=chunk),
        out_shape=(jax.ShapeDtypeStruct((g2, PAIR * wg, PAIR * wg + 2 * PAIR_STATE), BF16),
                   jax.ShapeDtypeStruct((g2, PAIR * wg, 2 * PAIR_STATE), BF16)),
        grid=(g2 // PAIRS_PER_TILE,),
        in_specs=[grp(a) for a in args],
        out_specs=(pl.BlockSpec((PAIRS_PER_TILE, PAIR * wg, PAIR * wg + 2 * PAIR_STATE), lambda j: (j, 0, 0)),
                   pl.BlockSpec((PAIRS_PER_TILE, PAIR * wg, 2 * PAIR_STATE), lambda j: (j, 0, 0))),
        compiler_params=_params(("parallel",)),
        name="s5_operators",
    )(*args)
    nstep = max(1, (nchunk - 1).bit_length())
    mult = (chunk * (2 ** jnp.arange(nstep))).astype(F32)
    ap = jnp.exp(ldt[:, None, :] * mult[None, :, None])
    pair = lambda t: t.reshape(g2, PAIR, nstep, p).transpose(0, 2, 1, 3).reshape(g2, nstep, PAIR_STATE)
    return w1, w2t, jnp.stack([pair(ap.real), pair(ap.imag)], axis=2)


def _s5_prompt_kernel(x_ref, w1_ref, w2t_ref, ap_ref, d_ref, g_ref, gb_ref, hn_ref, *, nbatch, nchunk):
    rows = nbatch * nchunk
    half = S5_CHUNK // 2
    xs = [x_ref[pl.ds(s, rows, stride=S5_CHUNK), :] for s in range(S5_CHUNK)]
    v0 = _block_transpose8(xs[:half])
    v1 = _block_transpose8(xs[half:])
    krow = lax.broadcasted_iota(jnp.int32, (rows, PAIR_STATE), 0) & (nchunk - 1)
    width = PAIR * S5_CHUNK * SSM_GROUP
    y0 = [None] * GROUPS_PER_TILE
    y1 = [None] * GROUPS_PER_TILE
    for q in range(PAIRS_PER_TILE):
        ga, gb = PAIR * q, PAIR * q + 1
        u = jnp.concatenate([v0[ga], v1[ga], v0[gb], v1[gb]], axis=1).astype(BF16)
        z = jnp.dot(u, w1_ref[q], preferred_element_type=F32)
        vr = z[:, width:width + PAIR_STATE]
        vi = z[:, width + PAIR_STATE:]
        step = 0
        while (1 << step) < nchunk:
            d = 1 << step
            ar = ap_ref[q, step, 0:1]
            ai = ap_ref[q, step, 1:2]
            keep = krow >= d
            sr = jnp.where(keep, pltpu.roll(vr, d, axis=0), 0.0)
            si = jnp.where(keep, pltpu.roll(vi, d, axis=0), 0.0)
            vr, vi = vr + (ar * sr - ai * si), vi + (ar * si + ai * sr)
            step += 1
        first = krow == 0
        hr = jnp.where(first, 0.0, pltpu.roll(vr, 1, axis=0))
        hi = jnp.where(first, 0.0, pltpu.roll(vi, 1, axis=0))
        hcat = jnp.concatenate([hr, hi], axis=1).astype(BF16)
        y = z[:, :width] + lax.dot_general(hcat, w2t_ref[q], (((1,), (1,)), ((), ())), preferred_element_type=F32)
        y0[ga], y1[ga] = y[:, 0:LANES], y[:, LANES:2 * LANES]
        y0[gb], y1[gb] = y[:, 2 * LANES:3 * LANES], y[:, 3 * LANES:]
        last = [b * nchunk + nchunk - 1 for b in range(nbatch)]
        hn_ref[q, 0] = jnp.concatenate([vr[i:i + 1] for i in last], axis=0)
        hn_ref[q, 1] = jnp.concatenate([vi[i:i + 1] for i in last], axis=0)
    ys = _block_transpose8(y0) + _block_transpose8(y1)
    dskip = d_ref[...]
    for s in range(S5_CHUNK):
        g_ref[pl.ds(s, rows, stride=S5_CHUNK), :] = _gelu_tanh(ys[s] + dskip * xs[s])
    gb_ref[...] = g_ref[...].astype(BF16)


def s5_prompt(x, w1, w2t, ap, d, layer_i, nbatch, seq):
    n, dm = x.shape
    nchunk = seq // S5_CHUNK
    n_p = nbatch * seq
    nstep = ap.shape[1]
    g2 = w1.shape[0]
    width = PAIR * S5_CHUNK * SSM_GROUP
    tile = pl.BlockSpec((n_p, LANES), lambda j: (0, j))
    return pl.pallas_call(
        functools.partial(_s5_prompt_kernel, nbatch=nbatch, nchunk=nchunk),
        out_shape=(jax.ShapeDtypeStruct((n, dm), F32), jax.ShapeDtypeStruct((n, dm), BF16),
                   jax.ShapeDtypeStruct((g2, 2, nbatch, PAIR_STATE), F32)),
        grid=(dm // LANES,),
        in_specs=[tile,
                  pl.BlockSpec((PAIRS_PER_TILE, width, width + 2 * PAIR_STATE), lambda j: (j, 0, 0)),
                  pl.BlockSpec((PAIRS_PER_TILE, width, 2 * PAIR_STATE), lambda j: (j, 0, 0)),
                  pl.BlockSpec((PAIRS_PER_TILE, nstep, 2, PAIR_STATE), lambda j: (j, 0, 0, 0)),
                  pl.BlockSpec((None, 1, LANES), lambda j: (layer_i, 0, j))],
        out_specs=(tile, tile,
                   pl.BlockSpec((PAIRS_PER_TILE, 2, nbatch, PAIR_STATE), lambda j: (j, 0, 0, 0))),
        compiler_params=_params(("parallel",)),
        name="s5_prompt",
    )(x, w1, w2t, ap, d.reshape(d.shape[0], 1, dm))


def _gelu_skip_kernel(y_ref, x_ref, d_ref, o_ref):
    o_ref[...] = _gelu_tanh(y_ref[...] + d_ref[...] * x_ref[...])


def gelu_skip(y, x, d, layer_i, *, tm):
    n, dm = x.shape
    row = pl.BlockSpec((tm, dm), lambda i: (i, 0))
    return pl.pallas_call(
        _gelu_skip_kernel,
        out_shape=jax.ShapeDtypeStruct((n, dm), F32),
        grid=(n // tm,),
        in_specs=[row, row, pl.BlockSpec((None, 1, dm), lambda i: (layer_i, 0, 0))],
        out_specs=row,
        compiler_params=_params(("parallel",)),
        name="gelu_skip",
    )(y, x, d.reshape(d.shape[0], 1, dm))


ROUTE_COLS = LANES


def _split_bf16(v):
    hi = v.astype(BF16)
    lo = (v - hi.astype(F32)).astype(BF16)
    return hi, lo


def _router_kernel(x_ref, w_ref, gate_ref, eid_ref):
    xh, xl = _split_bf16(x_ref[...])
    wh, wl = _split_bf16(w_ref[...])
    dot = lambda a, b: jnp.dot(a, b, preferred_element_type=F32)
    logits = dot(xh, wh) + (dot(xh, wl) + dot(xl, wh))
    lane = lax.broadcasted_iota(jnp.int32, logits.shape, 1).astype(F32)
    neg = -jnp.inf
    big = float(ROUTE_COLS)
    lg = jnp.where(lane < N_EXPERT_GROUPS, logits, neg)
    m = jnp.max(lg, axis=1, keepdims=True)
    grp = jnp.min(jnp.where(lg == m, lane, big), axis=1, keepdims=True)
    gate_g = 1.0 / jnp.sum(jnp.exp(lg - m), axis=1, keepdims=True)
    lo = N_EXPERT_GROUPS + grp * EXPERTS_PER_GROUP
    le = jnp.where((lane >= lo) & (lane < lo + EXPERTS_PER_GROUP), logits, neg)
    t1 = jnp.max(le, axis=1, keepdims=True)
    i1 = jnp.min(jnp.where(le == t1, lane, big), axis=1, keepdims=True)
    le2 = jnp.where(lane == i1, neg, le)
    t2 = jnp.max(le2, axis=1, keepdims=True)
    i2 = jnp.min(jnp.where(le2 == t2, lane, big), axis=1, keepdims=True)
    e = jnp.exp(t2 - t1)
    p1 = 1.0 / (1.0 + e)
    p2 = e / (1.0 + e)
    gate_ref[...] = jnp.where(lane == 0.0, gate_g * p1, jnp.where(lane == 1.0, gate_g * p2, 0.0))
    eid = jnp.where(lane == 0.0, i1, jnp.where(lane == 1.0, i2, float(N_EXPERT_GROUPS))) - N_EXPERT_GROUPS
    eid_ref[...] = eid.astype(jnp.int32)


def router(x, w_route, layer, *, tm):
    n, d = x.shape
    return pl.pallas_call(
        _router_kernel,
        out_shape=(jax.ShapeDtypeStruct((n, ROUTE_COLS), F32), jax.ShapeDtypeStruct((n, ROUTE_COLS), jnp.int32)),
        grid=(n // tm,),
        in_specs=[pl.BlockSpec((tm, d), lambda i: (i, 0)),
                  pl.BlockSpec((None, d, ROUTE_COLS), lambda i: (layer, 0, 0))],
        out_specs=(pl.BlockSpec((tm, ROUTE_COLS), lambda i: (i, 0)), pl.BlockSpec((tm, ROUTE_COLS), lambda i: (i, 0))),
        compiler_params=_params(("parallel",)),
        name="router",
    )(x, w_route)


MOE_SUB = 128
MOE_NSUB = 6
MOE_KC = 1024
MOE_TM = MOE_SUB * MOE_NSUB
MOE_TF = 256
MOE_TN = 512
MOE_NF = D_EXPERT // MOE_TF
MOE_NN = D_MODEL // MOE_TN


def _moe_kernel(be_ref, ns_ref, bi_ref, x_ref, wg_ref, wu_ref, wd_ref, o_ref, xb_ref, hid_ref):
    blk = pl.program_id(0)
    s = pl.program_id(1)
    nsub = ns_ref[blk]
    live = nsub > 0

    @pl.when(live & (s == 0))
    def _():
        words = x_ref[...]
        xb_ref[:, :HALF_D] = pltpu.bitcast(words << 16, F32).astype(BF16)
        xb_ref[:, HALF_D:] = pltpu.bitcast(words & -65536, F32).astype(BF16)

    for n in range(1, MOE_NSUB + 1):
        rows = n * MOE_SUB

        @pl.when((nsub == n) & (s < MOE_NF))
        def _():
            gate = up = None
            for kc in range(D_MODEL // MOE_KC):
                ks = slice(kc * MOE_KC, (kc + 1) * MOE_KC)
                xr = xb_ref[0:rows, ks]
                gp = jnp.dot(xr, wg_ref[ks, :].astype(BF16), preferred_element_type=F32)
                upp = jnp.dot(xr, wu_ref[ks, :].astype(BF16), preferred_element_type=F32)
                gate = gp if gate is None else gate + gp
                up = upp if up is None else up + upp
            hid_ref[s, 0:rows, :] = ((gate * _sigmoid(gate)) * up).astype(BF16)

        @pl.when((nsub == n) & (s >= MOE_NF))
        def _():
            out = None
            for f in range(MOE_NF):
                fs = slice(f * MOE_TF, (f + 1) * MOE_TF)
                part = jnp.dot(hid_ref[f, 0:rows, :], wd_ref[fs, :].astype(BF16), preferred_element_type=F32)
                out = part if out is None else out + part
            o_ref[0:rows, :] = out
            if rows < MOE_TM:
                o_ref[rows:, :] = jnp.zeros((MOE_TM - rows, MOE_TN), F32)


def moe_experts(xs, blk_e, blk_nsub, blk_idx, w_gate, w_up, w_down, layer):
    p = xs.shape[0]
    d = D_MODEL
    up_idx = lambda s, ns, b: jnp.where(ns[b] > 0, jnp.minimum(s, MOE_NF - 1), MOE_NF - 1)
    dn_idx = lambda s, ns, b: jnp.where(ns[b] > 0, jnp.maximum(s - MOE_NF, 0), MOE_NN - 1)
    return pl.pallas_call(
        _moe_kernel,
        out_shape=jax.ShapeDtypeStruct((p, d), F32),
        grid_spec=pltpu.PrefetchScalarGridSpec(
            num_scalar_prefetch=3,
            grid=(p // MOE_TM, MOE_NF + MOE_NN),
            in_specs=[
                pl.BlockSpec((MOE_TM, HALF_D), lambda b, s, be, ns, bi: (bi[b], 0)),
                pl.BlockSpec((None, None, d, MOE_TF), lambda b, s, be, ns, bi: (layer, be[b], 0, up_idx(s, ns, b))),
                pl.BlockSpec((None, None, d, MOE_TF), lambda b, s, be, ns, bi: (layer, be[b], 0, up_idx(s, ns, b))),
                pl.BlockSpec((None, None, D_EXPERT, MOE_TN),
                             lambda b, s, be, ns, bi: (layer, be[b], 0, dn_idx(s, ns, b)))],
            out_specs=pl.BlockSpec((MOE_TM, MOE_TN), lambda b, s, be, ns, bi: (bi[b], dn_idx(s, ns, b))),
            scratch_shapes=[pltpu.VMEM((MOE_TM, d), BF16), pltpu.VMEM((MOE_NF, MOE_TM, MOE_TF), BF16)],
        ),
        compiler_params=_params(("arbitrary", "arbitrary")),
        name="moe_experts",
    )(blk_e, blk_nsub, blk_idx, xs, w_gate, w_up, w_down)


def moe_dispatch(eids, n):
    nslot = n * 2
    eid = eids.reshape(-1)
    experts = jnp.arange(N_EXPERTS, dtype=jnp.int32)
    onehot = (eid[:, None] == experts[None, :]).astype(jnp.int32)
    seen = jnp.cumsum(onehot, axis=0)
    rank = jnp.sum(seen * onehot, axis=1) - 1
    counts = seen[-1]
    nblk = (counts + MOE_TM - 1) // MOE_TM
    bend = jnp.cumsum(nblk)
    bstart = bend - nblk
    dest = jnp.sum(onehot * bstart[None, :], axis=1) * MOE_TM + rank
    n_blocks = nslot // MOE_TM + N_EXPERTS
    row_tok = jnp.zeros((n_blocks * MOE_TM,), jnp.int32).at[dest].set(jnp.arange(nslot, dtype=jnp.int32) // 2)
    pos = dest.reshape(n, 2)
    blk = jnp.arange(n_blocks, dtype=jnp.int32)
    n_used = bend[-1]
    blk_idx = jnp.minimum(blk, n_used - 1)
    blk_e = jnp.sum((blk_idx[:, None] >= bend[None, :]).astype(jnp.int32), axis=1)
    rows_left = counts[blk_e] - (blk_idx - bstart[blk_e]) * MOE_TM
    nsub = (jnp.clip(rows_left, 0, MOE_TM) + MOE_SUB - 1) // MOE_SUB
    blk_nsub = jnp.where(blk < n_used, nsub, 0).astype(jnp.int32)
    return pos, row_tok, blk_e.astype(jnp.int32), blk_nsub, blk_idx


def hier_moe_ln(x, x_packed, layer, w_route, w_gate, w_up, w_down, ln_g, ln_b, *, tm):
    n = x.shape[0]
    gates, eids = router(x, w_route, layer, tm=tm)
    pos, row_tok, blk_e, blk_nsub, blk_idx = moe_dispatch(eids[:, :2], n)
    xs = x_packed[row_tok]
    ys = moe_experts(xs, blk_e, blk_nsub, blk_idx, w_gate, w_up, w_down, layer)
    return ln_moe_combine(x, ys[pos[:, 0]], ys[pos[:, 1]], gates, ln_g, ln_b, layer, tm=tm)


ROW_TILE = 320
MM_TM = 2080
MM_TN = 256


def _rows_after(full, tail, start):
    return lax.dynamic_update_slice(full, tail.astype(full.dtype), (start, 0))


def kernel(x_prompt, x_sample, cache_k, cache_v, state_conv, state_ssm_re, state_ssm_im, w_in, w_out, attn_sinks, conv_w, ssm_lambda_re, ssm_lambda_im, ssm_log_dt, ssm_b_re, ssm_b_im, ssm_c_re, ssm_c_im, ssm_d, w_glu, w_o, ln1_g, ln1_b, ln2_g, ln2_b, w_router_group, w_router_expert, w_gate, w_up, w_down):
    batch, seq, _ = x_prompt.shape
    db, n_new, _ = x_sample.shape
    n_p = batch * seq
    n_s = db * n_new
    n = n_p + n_s
    assert n % ROW_TILE == 0 and n % MM_TM == 0 and n_new <= T_PAD

    w_route = jnp.concatenate(
        [w_router_group, w_router_expert,
         jnp.zeros((DEPTH, D_MODEL, ROUTE_COLS - N_EXPERT_GROUPS - N_EXPERTS), F32)], axis=-1)

    x = jnp.concatenate([x_prompt.reshape(n_p, D_MODEL), x_sample.reshape(n_s, D_MODEL)], axis=0)
    xb = x.astype(BF16)
    k_p, v_p, conv_p, re_p, im_p = [], [], [], [], []
    k_s, v_s, conv_s, re_s, im_s = [], [], [], [], []
    for layer in range(DEPTH):
        i = layer // 2
        if layer % 2 == 0:
            h = matmul([xb], w_in, i, tm=MM_TM, tn=MM_TN, out_dtype=F32)
            attn = attn_prompt(h, attn_sinks[i], batch, seq)
            gconv, cu_tail = conv_prompt(h, conv_w, i, batch, seq)
            tails = [h[b * seq + seq - WINDOW:(b + 1) * seq, D_ATT:D_ATT + 2 * D_KV] for b in range(batch)]
            kv_tail = jnp.stack(tails)
            k_p.append(kv_tail[:, :, :D_KV].reshape(batch, WINDOW, N_KV_HEADS, HEAD_DIM))
            v_p.append(kv_tail[:, :, D_KV:].reshape(batch, WINDOW, N_KV_HEADS, HEAD_DIM))
            conv_p.append(cu_tail.reshape(batch, 8, D_CONV)[:, 8 - (CONV_W - 1):])
            hs = h[n_p:].reshape(db, n_new, D_IN_EVEN)
            k_new = hs[:, :, D_ATT:D_ATT + D_KV]
            v_new = hs[:, :, D_ATT + D_KV:D_ATT + 2 * D_KV]
            pad_rows = jnp.zeros((db, KC_PAD - WINDOW - n_new, D_KV), F32)
            kc = jnp.concatenate([cache_k[i].reshape(db, WINDOW, D_KV), k_new, pad_rows], axis=1)
            vc = jnp.concatenate([cache_v[i].reshape(db, WINDOW, D_KV), v_new, pad_rows], axis=1)
            q_s = jnp.pad(hs[:, :, :D_ATT], ((0, 0), (0, T_PAD - n_new), (0, 0)))
            attn_s = attn_sample(q_s, kc, vc, attn_sinks[i], n_new)[:, :n_new].reshape(n_s, D_ATT)
            k_s.append(kc[:, n_new:n_new + WINDOW].reshape(db, WINDOW, N_KV_HEADS, HEAD_DIM))
            v_s.append(vc[:, n_new:n_new + WINDOW].reshape(db, WINDOW, N_KV_HEADS, HEAD_DIM))
            off = D_ATT + 2 * D_KV
            tmaj = lambda a: a.transpose(1, 0, 2)
            gconv_s, st_s = conv_sample(tmaj(hs[:, :, off:off + D_CONV]),
                                        tmaj(hs[:, :, off + D_CONV:off + 2 * D_CONV]),
                                        tmaj(hs[:, :, off + 2 * D_CONV:]),
                                        tmaj(state_conv[i]), conv_w, i)
            conv_s.append(tmaj(st_s))
            attn = _rows_after(attn, attn_s, n_p)
            gconv = _rows_after(gconv, tmaj(gconv_s).reshape(n_s, D_CONV), n_p)
            mixed = matmul([attn, gconv], w_out, i, tm=MM_TM, tn=MM_TN, out_dtype=F32)
        else:
            ssm = (ssm_lambda_re[i], ssm_lambda_im[i], ssm_log_dt[i], ssm_b_re[i], ssm_b_im[i],
                   ssm_c_re[i], ssm_c_im[i])
            w1, w2t, ap = s5_operators(*ssm, S5_CHUNK, seq // S5_CHUNK)
            g, gb, hn = s5_prompt(x, w1, w2t, ap, ssm_d, i, batch, seq)
            unpair = lambda s: s.transpose(1, 0, 2).reshape(batch, N_SSM_GROUPS, SSM_STATE)
            re_p.append(unpair(hn[:, 0]))
            im_p.append(unpair(hn[:, 1]))
            x_s = x[n_p:]
            y_s, nre_s, nim_s = s5_apply(x_s.reshape(db, n_new, D_MODEL), state_ssm_re[i], state_ssm_im[i],
                                         ssm, n_new)
            re_s.append(nre_s)
            im_s.append(nim_s)
            g_s = gelu_skip(y_s.reshape(n_s, D_MODEL), x_s, ssm_d, i, tm=n_s)
            g = _rows_after(g, g_s, n_p)
            gb = _rows_after(gb, g_s, n_p)
            z = matmul([gb], w_glu, i, tm=MM_TM, tn=MM_TN, out_dtype=BF16, glu_gate=g)
            mixed = matmul([z], w_o, i, tm=MM_TM, tn=MM_TN, out_dtype=F32)
        x, x_packed = ln_residual(x, mixed, ln1_g, ln1_b, layer, tm=ROW_TILE)
        x, xb = hier_moe_ln(x, x_packed, layer, w_route, w_gate, w_up, w_down, ln2_g, ln2_b, tm=ROW_TILE)

    y_prompt = x[:n_p].reshape(batch, seq, D_MODEL)
    y_sample = x[n_p:].reshape(db, n_new, D_MODEL)
    st = jnp.stack
    return (y_prompt, y_sample, st(k_p), st(v_p), st(conv_p), st(re_p), st(im_p),
            st(k_s), st(v_s), st(conv_s), st(re_s), st(im_s))
```

```python
import functools
import math

import jax
import jax.numpy as jnp
from jax import lax
from jax.experimental import pallas as pl
from jax.experimental.pallas import tpu as pltpu

F32 = jnp.float32
BF16 = jnp.bfloat16

D_MODEL = 4096
DEPTH = 4
N_HEADS = 32
N_KV_HEADS = 4
Q_PER_KV = N_HEADS // N_KV_HEADS
HEAD_DIM = 64
WINDOW = 128
D_ATT = N_HEADS * HEAD_DIM
D_KV = N_KV_HEADS * HEAD_DIM
D_CONV = D_MODEL // 2
CONV_W = 3
D_IN_EVEN = D_ATT + 2 * D_KV + 3 * D_CONV
SSM_GROUP = 16
N_SSM_GROUPS = D_MODEL // SSM_GROUP
SSM_STATE = 64
N_EXPERT_GROUPS = 4
EXPERTS_PER_GROUP = 8
N_EXPERTS = N_EXPERT_GROUPS * EXPERTS_PER_GROUP
D_EXPERT = D_MODEL // 4
ALPHA = (2 * DEPTH) ** 0.25
LN_EPS = 1e-5

LANES = 128
VMEM_LIMIT = 58 * 1024 * 1024

COL_BLK = 256
K_BLK = D_ATT // COL_BLK
V_BLK = (D_ATT + D_KV) // COL_BLK
B_BLK = (D_ATT + 2 * D_KV) // COL_BLK
C_BLK = B_BLK + D_CONV // COL_BLK
U_BLK = C_BLK + D_CONV // COL_BLK


def _params(sem):
    return pltpu.CompilerParams(dimension_semantics=sem, vmem_limit_bytes=VMEM_LIMIT)


def _sigmoid(x):
    return 1.0 / (1.0 + jnp.exp(-x))


def _mm_kernel(*refs, n_a, glu):
    a_refs = refs[:n_a]
    w_ref = refs[n_a]
    g_ref = refs[n_a + 1] if glu else None
    o_ref = refs[-1]
    wb = w_ref[...].astype(BF16)
    acc = None
    off = 0
    for a_ref in a_refs:
        kp = a_ref.shape[1]
        part = jnp.dot(a_ref[...], wb[off:off + kp, :], preferred_element_type=F32)
        acc = part if acc is None else acc + part
        off += kp
    if glu:
        acc = g_ref[...] * _sigmoid(acc)
    o_ref[...] = acc.astype(o_ref.dtype)


def matmul(a_parts, w, layer, *, tm, tn, out_dtype, glu_gate=None):
    m = a_parts[0].shape[0]
    kdim, n = w.shape[-2:]
    assert m % tm == 0 and n % tn == 0 and sum(a.shape[1] for a in a_parts) == kdim
    in_specs = [pl.BlockSpec((tm, a.shape[1]), lambda i, j: (i, 0), pipeline_mode=pl.Buffered(1)) for a in a_parts]
    in_specs.append(pl.BlockSpec((None, kdim, tn), lambda i, j: (layer, 0, j)))
    args = list(a_parts) + [w]
    if glu_gate is not None:
        in_specs.append(pl.BlockSpec((tm, tn), lambda i, j: (i, j)))
        args.append(glu_gate)
    return pl.pallas_call(
        functools.partial(_mm_kernel, n_a=len(a_parts), glu=glu_gate is not None),
        out_shape=jax.ShapeDtypeStruct((m, n), out_dtype),
        grid=(m // tm, n // tn),
        in_specs=in_specs,
        out_specs=pl.BlockSpec((tm, tn), lambda i, j: (i, j)),
        compiler_params=_params(("parallel", "parallel")),
        name="matmul",
    )(*args)


def _layer_norm_rows(v, g, b):
    mu = jnp.mean(v, axis=-1, keepdims=True)
    d = v - mu
    var = jnp.mean(d * d, axis=-1, keepdims=True)
    return d * lax.rsqrt(var + LN_EPS) * g + b


HALF_D = D_MODEL // 2


def _ln_kernel(x_ref, y_ref, g_ref, b_ref, wr_ref, o_ref, op_ref, gate_ref, eid_ref):
    out = _layer_norm_rows(ALPHA * x_ref[...] + y_ref[...], g_ref[...], b_ref[...])
    o_ref[...] = out
    gate_ref[...], eid_ref[...] = _route(out, wr_ref[...])
    bits = pltpu.bitcast(out.astype(BF16).astype(F32), jnp.int32)
    op_ref[...] = bits[:, HALF_D:] | lax.shift_right_logical(bits[:, :HALF_D], 16)


def ln_residual_route(x, y, g, b, w_route, layer, *, tm):
    n, d = x.shape
    row = pl.BlockSpec((tm, d), lambda i: (i, 0))
    par = pl.BlockSpec((None, 1, d), lambda i: (layer, 0, 0))
    lanes = pl.BlockSpec((tm, ROUTE_COLS), lambda i: (i, 0))
    return pl.pallas_call(
        _ln_kernel,
        out_shape=(jax.ShapeDtypeStruct((n, d), F32), jax.ShapeDtypeStruct((n, HALF_D), jnp.int32),
                   jax.ShapeDtypeStruct((n, ROUTE_COLS), F32), jax.ShapeDtypeStruct((n, ROUTE_COLS), jnp.int32)),
        grid=(n // tm,),
        in_specs=[row, row, par, par, pl.BlockSpec((None, d, ROUTE_COLS), lambda i: (layer, 0, 0))],
        out_specs=(row, pl.BlockSpec((tm, HALF_D), lambda i: (i, 0)), lanes, lanes),
        compiler_params=_params(("parallel",)),
        name="ln_residual_route",
    )(x, y, g.reshape(DEPTH, 1, d), b.reshape(DEPTH, 1, d), w_route)


def _ln_moe_kernel(x_ref, y1_ref, y2_ref, gate_ref, g_ref, b_ref, o_ref, ob_ref):
    gate = gate_ref[...]
    ffn = y1_ref[...] * gate[:, 0:1] + y2_ref[...] * gate[:, 1:2]
    out = _layer_norm_rows(ALPHA * x_ref[...] + ffn, g_ref[...], b_ref[...])
    o_ref[...] = out
    ob_ref[...] = out.astype(BF16)


def ln_moe_combine(x, y1, y2, gates, g, b, layer, *, tm):
    n, d = x.shape
    row = pl.BlockSpec((tm, d), lambda i: (i, 0))
    par = pl.BlockSpec((None, 1, d), lambda i: (layer, 0, 0))
    return pl.pallas_call(
        _ln_moe_kernel,
        out_shape=(jax.ShapeDtypeStruct((n, d), F32), jax.ShapeDtypeStruct((n, d), BF16)),
        grid=(n // tm,),
        in_specs=[row, row, row, pl.BlockSpec((tm, LANES), lambda i: (i, 0)), par, par],
        out_specs=(row, row),
        compiler_params=_params(("parallel",)),
        name="ln_moe_combine",
    )(x, y1, y2, gates, g.reshape(DEPTH, 1, d), b.reshape(DEPTH, 1, d))


def _softmax_sink_pv(s, valid, sink_col, vband):
    s = jnp.where(valid, s, -jnp.inf)
    m = jnp.maximum(jnp.max(s, axis=1, keepdims=True), sink_col)
    p = jnp.exp(s - m)
    denom = jnp.sum(p, axis=1, keepdims=True) + jnp.exp(sink_col - m)
    o = jnp.dot(p.astype(BF16), vband, preferred_element_type=F32)
    return o / denom


def _attn_prompt_kernel(sink_ref, q_ref, kc_ref, kp_ref, vc_ref, vp_ref, o_ref):
    nb = pl.program_id(1)
    pair_w = 2 * HEAD_DIM
    r = lax.broadcasted_iota(jnp.int32, (WINDOW, WINDOW), 0)
    c = lax.broadcasted_iota(jnp.int32, (WINDOW, WINDOW), 1)
    from_prev = c > r
    prev_live = from_prev & (nb > 0)
    first_head = lax.broadcasted_iota(jnp.int32, (WINDOW, pair_w), 1) < HEAD_DIM
    nt = lambda a, b: lax.dot_general(a, b, (((1,), (1,)), ((), ())), preferred_element_type=F32)

    def both_heads(t):
        z = jnp.zeros_like(t)
        return jnp.concatenate([jnp.concatenate([t, z], axis=1), jnp.concatenate([z, t], axis=1)], axis=0).astype(BF16)

    for j in range(N_KV_HEADS):
        cols = slice(j * HEAD_DIM, (j + 1) * HEAD_DIM)
        k2p, k2c = both_heads(kp_ref[:, cols]), both_heads(kc_ref[:, cols])
        v2p, v2c = both_heads(vp_ref[:, cols]), both_heads(vc_ref[:, cols])
        for hp in range(j * Q_PER_KV // 2, (j + 1) * Q_PER_KV // 2):
            lanes = slice(hp * pair_w, (hp + 1) * pair_w)
            q2 = q_ref[:, lanes].astype(BF16)
            s_prev = nt(q2, k2p) * (HEAD_DIM ** -0.5)
            s_cur = nt(q2, k2c) * (HEAD_DIM ** -0.5)
            p_prev, p_cur, inv = [], [], []
            for side in range(2):
                half = slice(side * WINDOW, (side + 1) * WINDOW)
                s = jnp.where(prev_live, s_prev[:, half], jnp.where(from_prev, -jnp.inf, s_cur[:, half]))
                sink = sink_ref[2 * hp + side]
                m = jnp.maximum(jnp.max(s, axis=1, keepdims=True), sink)
                p = jnp.exp(s - m)
                inv.append(1.0 / (jnp.sum(p, axis=1, keepdims=True) + jnp.exp(sink - m)))
                p_prev.append(jnp.where(from_prev, p, 0.0))
                p_cur.append(jnp.where(from_prev, 0.0, p))
            o = (jnp.dot(jnp.concatenate(p_prev, axis=1).astype(BF16), v2p, preferred_element_type=F32)
                 + jnp.dot(jnp.concatenate(p_cur, axis=1).astype(BF16), v2c, preferred_element_type=F32))
            o_ref[:, lanes] = (o * jnp.where(first_head, inv[0], inv[1])).astype(o_ref.dtype)


def attn_prompt(h, sinks, batch, seq):
    nb = seq // WINDOW
    cur = lambda col: pl.BlockSpec((WINDOW, COL_BLK), lambda b, n, s: (b * nb + n, col))
    prev = lambda col: pl.BlockSpec((WINDOW, COL_BLK), lambda b, n, s: (b * nb + jnp.maximum(n - 1, 0), col))
    return pl.pallas_call(
        _attn_prompt_kernel,
        out_shape=jax.ShapeDtypeStruct((h.shape[0], D_ATT), BF16),
        grid_spec=pltpu.PrefetchScalarGridSpec(
            num_scalar_prefetch=1,
            grid=(batch, nb),
            in_specs=[pl.BlockSpec((WINDOW, D_ATT), lambda b, n, s: (b * nb + n, 0)),
                      cur(K_BLK), prev(K_BLK), cur(V_BLK), prev(V_BLK)],
            out_specs=pl.BlockSpec((WINDOW, D_ATT), lambda b, n, s: (b * nb + n, 0)),
        ),
        compiler_params=_params(("parallel", "arbitrary")),
        name="attn_prompt",
    )(sinks, h, h, h, h, h)


T_PAD = 8
KC_PAD = WINDOW + T_PAD


def _attn_sample_kernel(sink_ref, q_ref, kc_ref, vc_ref, o_ref, *, n_new):
    rows = Q_PER_KV * T_PAD
    t = lax.broadcasted_iota(jnp.int32, (rows, KC_PAD), 0) & (T_PAD - 1)
    c = lax.broadcasted_iota(jnp.int32, (rows, KC_PAD), 1)
    valid = (c > t) & (c <= t + WINDOW) & (c < WINDOW + n_new)
    for j in range(N_KV_HEADS):
        heads = range(j * Q_PER_KV, (j + 1) * Q_PER_KV)
        qg = jnp.concatenate([q_ref[:, h * HEAD_DIM:(h + 1) * HEAD_DIM] for h in heads], axis=0).astype(BF16)
        cols = slice(j * HEAD_DIM, (j + 1) * HEAD_DIM)
        kband = kc_ref[:, cols].astype(BF16)
        vband = vc_ref[:, cols].astype(BF16)
        s = lax.dot_general(qg, kband, (((1,), (1,)), ((), ())), preferred_element_type=F32) * (HEAD_DIM ** -0.5)
        sink_col = jnp.concatenate([jnp.full((T_PAD, 1), sink_ref[h], F32) for h in heads], axis=0)
        o = _softmax_sink_pv(s, valid, sink_col, vband)
        o = jnp.concatenate([o[i * T_PAD:(i + 1) * T_PAD] for i in range(Q_PER_KV)], axis=1)
        o_ref[:, j * Q_PER_KV * HEAD_DIM:(j + 1) * Q_PER_KV * HEAD_DIM] = o.astype(o_ref.dtype)


def attn_sample(q, kc, vc, sinks, n_new):
    db = q.shape[0]
    return pl.pallas_call(
        functools.partial(_attn_sample_kernel, n_new=n_new),
        out_shape=jax.ShapeDtypeStruct((db, T_PAD, D_ATT), BF16),
        grid_spec=pltpu.PrefetchScalarGridSpec(
            num_scalar_prefetch=1,
            grid=(db,),
            in_specs=[pl.BlockSpec((None, T_PAD, D_ATT), lambda b, s: (b, 0, 0)),
                      pl.BlockSpec((None, KC_PAD, D_KV), lambda b, s: (b, 0, 0)),
                      pl.BlockSpec((None, KC_PAD, D_KV), lambda b, s: (b, 0, 0))],
            out_specs=pl.BlockSpec((None, T_PAD, D_ATT), lambda b, s: (b, 0, 0)),
        ),
        compiler_params=_params(("parallel",)),
        name="attn_sample",
    )(sinks, q, kc, vc)


def _conv_prompt_kernel(b_ref, c_ref, u_ref, w_ref, o_ref, st_ref):
    cu = c_ref[...] * u_ref[...]
    t = lax.broadcasted_iota(jnp.int32, cu.shape, 0)
    sh1 = jnp.where(t >= 1, pltpu.roll(cu, 1, axis=0), 0.0)
    sh2 = jnp.where(t >= 2, pltpu.roll(cu, 2, axis=0), 0.0)
    w = w_ref[...]
    y = sh2 * w[0:1] + sh1 * w[1:2] + cu * w[2:3]
    o_ref[...] = (b_ref[...] * y).astype(o_ref.dtype)
    st_ref[...] = cu[cu.shape[0] - 8:]


def conv_prompt(h, conv_w, layer_i, batch, seq):
    nj = D_CONV // COL_BLK
    col = lambda base: pl.BlockSpec((seq, COL_BLK), lambda b, j: (b, base + j))
    return pl.pallas_call(
        _conv_prompt_kernel,
        out_shape=(jax.ShapeDtypeStruct((h.shape[0], D_CONV), BF16),
                   jax.ShapeDtypeStruct((batch * 8, D_CONV), F32)),
        grid=(batch, nj),
        in_specs=[col(B_BLK), col(C_BLK), col(U_BLK),
                  pl.BlockSpec((None, CONV_W, COL_BLK), lambda b, j: (layer_i, 0, j))],
        out_specs=(pl.BlockSpec((seq, COL_BLK), lambda b, j: (b, j)),
                   pl.BlockSpec((8, COL_BLK), lambda b, j: (b, j))),
        compiler_params=_params(("parallel", "parallel")),
        name="conv_prompt",
    )(h, h, h, conv_w)


def _conv_sample_kernel(b_ref, c_ref, u_ref, buf_ref, w_ref, o_ref, st_ref, *, n_new):
    w = w_ref[...]
    full = [buf_ref[i] for i in range(CONV_W - 1)] + [c_ref[t] * u_ref[t] for t in range(n_new)]
    for t in range(n_new):
        y = full[t] * w[0:1] + full[t + 1] * w[1:2] + full[t + 2] * w[2:3]
        o_ref[t] = (b_ref[t] * y).astype(o_ref.dtype)
    for i in range(CONV_W - 1):
        st_ref[i] = full[n_new + i]


def conv_sample(bg, cg, ug, buf, conv_w, layer_i):
    n_new, db, _ = bg.shape
    full = lambda a: pl.BlockSpec(a.shape, lambda g: (0,) * a.ndim)
    return pl.pallas_call(
        functools.partial(_conv_sample_kernel, n_new=n_new),
        out_shape=(jax.ShapeDtypeStruct((n_new, db, D_CONV), BF16),
                   jax.ShapeDtypeStruct((CONV_W - 1, db, D_CONV), F32)),
        grid=(1,),
        in_specs=[full(bg), full(cg), full(ug), full(buf),
                  pl.BlockSpec((None, CONV_W, D_CONV), lambda g: (layer_i, 0, 0))],
        out_specs=(pl.BlockSpec((n_new, db, D_CONV), lambda g: (0, 0, 0)),
                   pl.BlockSpec((CONV_W - 1, db, D_CONV), lambda g: (0, 0, 0))),
        compiler_params=_params(("arbitrary",)),
        name="conv_sample",
    )(bg, cg, ug, buf, conv_w)


PAIR = 2
PAIR_STATE = PAIR * SSM_STATE


def _s5_kernel(u_ref, w1_ref, w2t_ref, ap_ref, h0_ref, y_ref, hn_ref, *, nbatch, nchunk, width):
    rows = nbatch * nchunk
    z = jnp.dot(u_ref[...], w1_ref[...], preferred_element_type=F32)
    yi = z[:, :width]
    vr = z[:, width:width + PAIR_STATE]
    vi = z[:, width + PAIR_STATE:]
    h0r = h0_ref[0]
    h0i = h0_ref[1]
    if nchunk == 1:
        h0r_rows, h0i_rows = h0r, h0i
    else:
        h0r_rows = jnp.concatenate([jnp.broadcast_to(h0r[b:b + 1], (nchunk, PAIR_STATE)) for b in range(nbatch)], 0)
        h0i_rows = jnp.concatenate([jnp.broadcast_to(h0i[b:b + 1], (nchunk, PAIR_STATE)) for b in range(nbatch)], 0)
    krow = lax.broadcasted_iota(jnp.int32, (rows, PAIR_STATE), 0) & (nchunk - 1)
    ar = ap_ref[0, 0:1]
    ai = ap_ref[0, 1:2]
    first = krow == 0
    vr = vr + jnp.where(first, ar * h0r_rows - ai * h0i_rows, 0.0)
    vi = vi + jnp.where(first, ar * h0i_rows + ai * h0r_rows, 0.0)
    step = 0
    while (1 << step) < nchunk:
        d = 1 << step
        ar = ap_ref[step, 0:1]
        ai = ap_ref[step, 1:2]
        keep = krow >= d
        sr = jnp.where(keep, pltpu.roll(vr, d, axis=0), 0.0)
        si = jnp.where(keep, pltpu.roll(vi, d, axis=0), 0.0)
        vr, vi = vr + (ar * sr - ai * si), vi + (ar * si + ai * sr)
        step += 1
    if nchunk == 1:
        hr, hi = h0r_rows, h0i_rows
    else:
        hr = jnp.where(first, h0r_rows, pltpu.roll(vr, 1, axis=0))
        hi = jnp.where(first, h0i_rows, pltpu.roll(vi, 1, axis=0))
    hcat = jnp.concatenate([hr, hi], axis=1).astype(BF16)
    y_ref[...] = yi + lax.dot_general(hcat, w2t_ref[...], (((1,), (1,)), ((), ())), preferred_element_type=F32)
    if nchunk == 1:
        hn_ref[0] = vr
        hn_ref[1] = vi
    else:
        last = [b * nchunk + nchunk - 1 for b in range(nbatch)]
        hn_ref[0] = jnp.concatenate([vr[i:i + 1] for i in last], axis=0)
        hn_ref[1] = jnp.concatenate([vi[i:i + 1] for i in last], axis=0)


def s5_scan(u, w1, w2t, ap, h0, nbatch, nchunk):
    g2, rows, width = u.shape
    nstep = ap.shape[1]
    return pl.pallas_call(
        functools.partial(_s5_kernel, nbatch=nbatch, nchunk=nchunk, width=width),
        out_shape=(jax.ShapeDtypeStruct((g2, rows, width), F32),
                   jax.ShapeDtypeStruct((g2, 2, nbatch, PAIR_STATE), F32)),
        grid=(g2,),
        in_specs=[pl.BlockSpec((None, rows, width), lambda g: (g, 0, 0)),
                  pl.BlockSpec((None, width, width + 2 * PAIR_STATE), lambda g: (g, 0, 0)),
                  pl.BlockSpec((None, width, 2 * PAIR_STATE), lambda g: (g, 0, 0)),
                  pl.BlockSpec((None, nstep, 2, PAIR_STATE), lambda g: (g, 0, 0, 0)),
                  pl.BlockSpec((None, 2, nbatch, PAIR_STATE), lambda g: (g, 0, 0, 0))],
        out_specs=(pl.BlockSpec((None, rows, width), lambda g: (g, 0, 0)),
                   pl.BlockSpec((None, 2, nbatch, PAIR_STATE), lambda g: (g, 0, 0, 0))),
        compiler_params=_params(("parallel",)),
        name="s5_scan",
    )(u, w1, w2t, ap, h0)


def s5_apply(x, h0_re, h0_im, ssm, chunk):
    bsz, t, _ = x.shape
    nchunk = t // chunk
    g2 = N_SSM_GROUPS // PAIR
    width = PAIR * chunk * SSM_GROUP
    w1, w2t, ap = s5_operators(*ssm, chunk, nchunk)
    u = x.astype(BF16).reshape(bsz, nchunk, chunk, g2, PAIR, SSM_GROUP)
    u = u.transpose(3, 0, 1, 4, 2, 5).reshape(g2, bsz * nchunk, width)
    pair_state = lambda s: s.reshape(bsz, g2, PAIR_STATE).transpose(1, 0, 2)
    h0 = jnp.stack([pair_state(h0_re), pair_state(h0_im)], axis=1)
    y, hn = s5_scan(u, w1, w2t, ap, h0, bsz, nchunk)
    y = y.reshape(g2, bsz, nchunk, PAIR, chunk, SSM_GROUP).transpose(1, 2, 4, 0, 3, 5).reshape(bsz, t, D_MODEL)
    unpair = lambda s: s.transpose(1, 0, 2).reshape(bsz, N_SSM_GROUPS, SSM_STATE)
    return y, unpair(hn[:, 0]), unpair(hn[:, 1])


def _gelu_tanh(v):
    inner = math.sqrt(2.0 / math.pi) * (v + 0.044715 * (v * v * v))
    return 0.5 * v * (1.0 + jnp.tanh(inner))


GROUPS_PER_TILE = LANES // SSM_GROUP
PAIRS_PER_TILE = GROUPS_PER_TILE // PAIR
S5_CHUNK = 16


def _block_transpose8(arrs):
    arrs = list(arrs)
    lane = lax.broadcasted_iota(jnp.int32, arrs[0].shape, 1)
    for d in (4, 2, 1):
        clear = (lane & (d * SSM_GROUP)) == 0
        nxt = list(arrs)
        for i in range(GROUPS_PER_TILE):
            if i & d:
                continue
            lo, hi = arrs[i], arrs[i + d]
            nxt[i] = jnp.where(clear, lo, pltpu.roll(hi, d * SSM_GROUP, axis=1))
            nxt[i + d] = jnp.where(clear, pltpu.roll(lo, LANES - d * SSM_GROUP, axis=1), hi)
        arrs = nxt
    return arrs


def _shift_lanes(a, k):
    if k == 0:
        return a
    return jnp.concatenate([jnp.zeros((a.shape[0], k), a.dtype), a[:, :a.shape[1] - k]], axis=1)


def _dot_nt_3pass(a, b):
    ah, al = _split_bf16(a)
    bh, bl = _split_bf16(b)
    nt = lambda u, v: lax.dot_general(u, v, (((1,), (1,)), ((), ())), preferred_element_type=F32)
    return nt(ah, bh) + (nt(ah, bl) + nt(al, bh))


def _s5_ops_kernel(cr_ref, ci_ref, btr_ref, bti_ref, pwr_ref, pwi_ref, w1_ref, w2t_ref, *, chunk):
    nl = chunk
    wg = nl * SSM_GROUP
    rep = lambda a, lo: jnp.concatenate(
        [jnp.broadcast_to(a[d:d + 1], (SSM_GROUP, SSM_STATE)) for d in range(lo, lo + nl)], axis=0)
    tile = lambda a: jnp.concatenate([a] * nl, axis=0)
    for q in range(PAIRS_PER_TILE):
        _s5_pair_operators(q, cr_ref, ci_ref, btr_ref, bti_ref, pwr_ref, pwi_ref, w1_ref, w2t_ref, nl, wg, rep, tile)


def _s5_pair_operators(q, cr_ref, ci_ref, btr_ref, bti_ref, pwr_ref, pwi_ref, w1_ref, w2t_ref, nl, wg, rep, tile):
    ky, ksr, ksi, khr, khi = [], [], [], [], []
    for g in range(PAIR * q, PAIR * q + PAIR):
        cr, ci = tile(cr_ref[g]), tile(ci_ref[g])
        btr, bti = btr_ref[g], bti_ref[g]
        pwr, pwi = pwr_ref[g], pwi_ref[g]
        p0r, p0i = rep(pwr, 0), rep(pwi, 0)
        m0r = cr * p0r - ci * p0i
        m0i = cr * p0i + ci * p0r
        r = _dot_nt_3pass(btr, m0r) - _dot_nt_3pass(bti, m0i)
        ky.append(jnp.concatenate([_shift_lanes(r, s * SSM_GROUP) for s in range(nl)], axis=0))
        sr, si = [], []
        for s in range(nl):
            pr = pwr[nl - 1 - s:nl - s]
            pi = pwi[nl - 1 - s:nl - s]
            sr.append(btr * pr - bti * pi)
            si.append(btr * pi + bti * pr)
        ksr.append(jnp.concatenate(sr, axis=0))
        ksi.append(jnp.concatenate(si, axis=0))
        p1r, p1i = rep(pwr, 1), rep(pwi, 1)
        khr.append(cr * p1r - ci * p1i)
        khi.append(-(cr * p1i + ci * p1r))
    zy = jnp.zeros((wg, wg), F32)
    zs = jnp.zeros((wg, SSM_STATE), F32)
    top = jnp.concatenate([ky[0], zy, ksr[0], zs, ksi[0], zs], axis=1)
    bot = jnp.concatenate([zy, ky[1], zs, ksr[1], zs, ksi[1]], axis=1)
    w1_ref[q] = jnp.concatenate([top, bot], axis=0).astype(BF16)
    w2t_ref[q] = jnp.concatenate([jnp.concatenate([khr[0], zs, khi[0], zs], axis=1),
                                  jnp.concatenate([zs, khr[1], zs, khi[1]], axis=1)], axis=0).astype(BF16)


def s5_operators(lam_re, lam_im, log_dt, b_re, b_im, c_re, c_im, chunk, nchunk):
    g, p = lam_re.shape
    g2 = g // PAIR
    wg = chunk * SSM_GROUP
    ldt = lax.complex(lam_re, lam_im) * jnp.exp(log_dt)[:, None]
    a_bar = jnp.exp(ldt)
    b_bar = ((a_bar - 1.0) / lax.complex(lam_re, lam_im))[..., None] * lax.complex(b_re, b_im)
    pw = jnp.exp(ldt[:, None, :] * jnp.arange(chunk + 1, dtype=F32)[None, :, None])
    bt = b_bar.transpose(0, 2, 1)
    grp = lambda a: pl.BlockSpec((GROUPS_PER_TILE,) + a.shape[1:], lambda j: (j, 0, 0))
    args = (c_re, c_im, bt.real, bt.imag, pw.real, pw.imag)
    w1, w2t = pl.pallas_call(
        functools.partial(_s5_ops_kernel, chunk=chunk),
        out_shape=(jax.ShapeDtypeStruct((g2, PAIR * wg, PAIR * wg + 2 * PAIR_STATE), BF16),
                   jax.ShapeDtypeStruct((g2, PAIR * wg, 2 * PAIR_STATE), BF16)),
        grid=(g2 // PAIRS_PER_TILE,),
        in_specs=[grp(a) for a in args],
        out_specs=(pl.BlockSpec((PAIRS_PER_TILE, PAIR * wg, PAIR * wg + 2 * PAIR_STATE), lambda j: (j, 0, 0)),
                   pl.BlockSpec((PAIRS_PER_TILE, PAIR * wg, 2 * PAIR_STATE), lambda j: (j, 0, 0))),
        compiler_params=_params(("parallel",)),
        name="s5_operators",
    )(*args)
    nstep = max(1, (nchunk - 1).bit_length())
    mult = (chunk * (2 ** jnp.arange(nstep))).astype(F32)
    ap = jnp.exp(ldt[:, None, :] * mult[None, :, None])
    pair = lambda t: t.reshape(g2, PAIR, nstep, p).transpose(0, 2, 1, 3).reshape(g2, nstep, PAIR_STATE)
    return w1, w2t, jnp.stack([pair(ap.real), pair(ap.imag)], axis=2)


def _s5_prompt_kernel(x_ref, w1_ref, w2t_ref, ap_ref, d_ref, g_ref, gb_ref, hn_ref, *, nbatch, nchunk):
    rows = nbatch * nchunk
    half = S5_CHUNK // 2
    xs = [x_ref[pl.ds(s, rows, stride=S5_CHUNK), :] for s in range(S5_CHUNK)]
    v0 = _block_transpose8(xs[:half])
    v1 = _block_transpose8(xs[half:])
    krow = lax.broadcasted_iota(jnp.int32, (rows, PAIR_STATE), 0) & (nchunk - 1)
    width = PAIR * S5_CHUNK * SSM_GROUP
    y0 = [None] * GROUPS_PER_TILE
    y1 = [None] * GROUPS_PER_TILE
    for q in range(PAIRS_PER_TILE):
        ga, gb = PAIR * q, PAIR * q + 1
        u = jnp.concatenate([v0[ga], v1[ga], v0[gb], v1[gb]], axis=1).astype(BF16)
        z = jnp.dot(u, w1_ref[q], preferred_element_type=F32)
        vr = z[:, width:width + PAIR_STATE]
        vi = z[:, width + PAIR_STATE:]
        step = 0
        while (1 << step) < nchunk:
            d = 1 << step
            ar = ap_ref[q, step, 0:1]
            ai = ap_ref[q, step, 1:2]
            keep = krow >= d
            sr = jnp.where(keep, pltpu.roll(vr, d, axis=0), 0.0)
            si = jnp.where(keep, pltpu.roll(vi, d, axis=0), 0.0)
            vr, vi = vr + (ar * sr - ai * si), vi + (ar * si + ai * sr)
            step += 1
        first = krow == 0
        hr = jnp.where(first, 0.0, pltpu.roll(vr, 1, axis=0))
        hi = jnp.where(first, 0.0, pltpu.roll(vi, 1, axis=0))
        hcat = jnp.concatenate([hr, hi], axis=1).astype(BF16)
        y = z[:, :width] + lax.dot_general(hcat, w2t_ref[q], (((1,), (1,)), ((), ())), preferred_element_type=F32)
        y0[ga], y1[ga] = y[:, 0:LANES], y[:, LANES:2 * LANES]
        y0[gb], y1[gb] = y[:, 2 * LANES:3 * LANES], y[:, 3 * LANES:]
        last = [b * nchunk + nchunk - 1 for b in range(nbatch)]
        hn_ref[q, 0] = jnp.concatenate([vr[i:i + 1] for i in last], axis=0)
        hn_ref[q, 1] = jnp.concatenate([vi[i:i + 1] for i in last], axis=0)
    ys = _block_transpose8(y0) + _block_transpose8(y1)
    dskip = d_ref[...]
    for s in range(S5_CHUNK):
        g_ref[pl.ds(s, rows, stride=S5_CHUNK), :] = _gelu_tanh(ys[s] + dskip * xs[s])
    gb_ref[...] = g_ref[...].astype(BF16)


def s5_prompt(x, w1, w2t, ap, d, layer_i, nbatch, seq):
    n, dm = x.shape
    nchunk = seq // S5_CHUNK
    n_p = nbatch * seq
    nstep = ap.shape[1]
    g2 = w1.shape[0]
    width = PAIR * S5_CHUNK * SSM_GROUP
    tile = pl.BlockSpec((n_p, LANES), lambda j: (0, j))
    return pl.pallas_call(
        functools.partial(_s5_prompt_kernel, nbatch=nbatch, nchunk=nchunk),
        out_shape=(jax.ShapeDtypeStruct((n, dm), F32), jax.ShapeDtypeStruct((n, dm), BF16),
                   jax.ShapeDtypeStruct((g2, 2, nbatch, PAIR_STATE), F32)),
        grid=(dm // LANES,),
        in_specs=[tile,
                  pl.BlockSpec((PAIRS_PER_TILE, width, width + 2 * PAIR_STATE), lambda j: (j, 0, 0)),
                  pl.BlockSpec((PAIRS_PER_TILE, width, 2 * PAIR_STATE), lambda j: (j, 0, 0)),
                  pl.BlockSpec((PAIRS_PER_TILE, nstep, 2, PAIR_STATE), lambda j: (j, 0, 0, 0)),
                  pl.BlockSpec((None, 1, LANES), lambda j: (layer_i, 0, j))],
        out_specs=(tile, tile,
                   pl.BlockSpec((PAIRS_PER_TILE, 2, nbatch, PAIR_STATE), lambda j: (j, 0, 0, 0))),
        compiler_params=_params(("parallel",)),
        name="s5_prompt",
    )(x, w1, w2t, ap, d.reshape(d.shape[0], 1, dm))


def _gelu_skip_kernel(y_ref, x_ref, d_ref, o_ref):
    o_ref[...] = _gelu_tanh(y_ref[...] + d_ref[...] * x_ref[...])


def gelu_skip(y, x, d, layer_i, *, tm):
    n, dm = x.shape
    row = pl.BlockSpec((tm, dm), lambda i: (i, 0))
    return pl.pallas_call(
        _gelu_skip_kernel,
        out_shape=jax.ShapeDtypeStruct((n, dm), F32),
        grid=(n // tm,),
        in_specs=[row, row, pl.BlockSpec((None, 1, dm), lambda i: (layer_i, 0, 0))],
        out_specs=row,
        compiler_params=_params(("parallel",)),
        name="gelu_skip",
    )(y, x, d.reshape(d.shape[0], 1, dm))


ROUTE_COLS = LANES


def _split_bf16(v):
    hi = v.astype(BF16)
    lo = (v - hi.astype(F32)).astype(BF16)
    return hi, lo


def _route(x, w):
    xh, xl = _split_bf16(x)
    wh, wl = _split_bf16(w)
    dot = lambda a, b: jnp.dot(a, b, preferred_element_type=F32)
    logits = dot(xh, wh) + (dot(xh, wl) + dot(xl, wh))
    lane = lax.broadcasted_iota(jnp.int32, logits.shape, 1).astype(F32)
    neg = -jnp.inf
    big = float(ROUTE_COLS)
    lg = jnp.where(lane < N_EXPERT_GROUPS, logits, neg)
    m = jnp.max(lg, axis=1, keepdims=True)
    grp = jnp.min(jnp.where(lg == m, lane, big), axis=1, keepdims=True)
    gate_g = 1.0 / jnp.sum(jnp.exp(lg - m), axis=1, keepdims=True)
    lo = N_EXPERT_GROUPS + grp * EXPERTS_PER_GROUP
    le = jnp.where((lane >= lo) & (lane < lo + EXPERTS_PER_GROUP), logits, neg)
    t1 = jnp.max(le, axis=1, keepdims=True)
    i1 = jnp.min(jnp.where(le == t1, lane, big), axis=1, keepdims=True)
    le2 = jnp.where(lane == i1, neg, le)
    t2 = jnp.max(le2, axis=1, keepdims=True)
    i2 = jnp.min(jnp.where(le2 == t2, lane, big), axis=1, keepdims=True)
    e = jnp.exp(t2 - t1)
    p1 = 1.0 / (1.0 + e)
    p2 = e / (1.0 + e)
    gates = jnp.where(lane == 0.0, gate_g * p1, jnp.where(lane == 1.0, gate_g * p2, 0.0))
    eid = jnp.where(lane == 0.0, i1, jnp.where(lane == 1.0, i2, float(N_EXPERT_GROUPS))) - N_EXPERT_GROUPS
    return gates, eid.astype(jnp.int32)


MOE_SUB = 128
MOE_NSUB = 6
MOE_KC = 1024
MOE_TM = MOE_SUB * MOE_NSUB
MOE_TF = 256
MOE_TN = 1024
MOE_NF = D_EXPERT // MOE_TF
MOE_NN = D_MODEL // MOE_TN


def _moe_kernel(be_ref, ns_ref, bi_ref, x_ref, wg_ref, wu_ref, wd_ref, o_ref, xb_ref, hid_ref):
    blk = pl.program_id(0)
    s = pl.program_id(1)
    nsub = ns_ref[blk]
    live = nsub > 0

    @pl.when(live & (s == 0))
    def _():
        words = x_ref[...]
        xb_ref[:, :HALF_D] = pltpu.bitcast(words << 16, F32).astype(BF16)
        xb_ref[:, HALF_D:] = pltpu.bitcast(words & -65536, F32).astype(BF16)

    for n in range(1, MOE_NSUB + 1):
        rows = n * MOE_SUB

        @pl.when((nsub == n) & (s < MOE_NF))
        def _():
            gate = up = None
            for kc in range(D_MODEL // MOE_KC):
                ks = slice(kc * MOE_KC, (kc + 1) * MOE_KC)
                xr = xb_ref[0:rows, ks]
                gp = jnp.dot(xr, wg_ref[ks, :].astype(BF16), preferred_element_type=F32)
                upp = jnp.dot(xr, wu_ref[ks, :].astype(BF16), preferred_element_type=F32)
                gate = gp if gate is None else gate + gp
                up = upp if up is None else up + upp
            hid_ref[s, 0:rows, :] = ((gate * _sigmoid(gate)) * up).astype(BF16)

        @pl.when((nsub == n) & (s >= MOE_NF))
        def _():
            out = None
            for f in range(MOE_NF):
                fs = slice(f * MOE_TF, (f + 1) * MOE_TF)
                part = jnp.dot(hid_ref[f, 0:rows, :], wd_ref[fs, :].astype(BF16), preferred_element_type=F32)
                out = part if out is None else out + part
            o_ref[0:rows, :] = out
            if rows < MOE_TM:
                o_ref[rows:, :] = jnp.zeros((MOE_TM - rows, MOE_TN), F32)


def moe_experts(xs, blk_e, blk_nsub, blk_idx, w_gate, w_up, w_down, layer):
    p = xs.shape[0]
    d = D_MODEL
    nblk = p // MOE_TM
    up_idx = lambda s, ns, b: jnp.where(ns[b] > 0, jnp.minimum(s, MOE_NF - 1), MOE_NF - 1)
    dn_idx = lambda s, ns, b: jnp.where(ns[b] > 0, jnp.maximum(s - MOE_NF, 0), MOE_NN - 1)
    x_blk = lambda s, b: jnp.minimum(b + (s >= MOE_NF).astype(jnp.int32), nblk - 1)
    return pl.pallas_call(
        _moe_kernel,
        out_shape=jax.ShapeDtypeStruct((p, d), F32),
        grid_spec=pltpu.PrefetchScalarGridSpec(
            num_scalar_prefetch=3,
            grid=(nblk, MOE_NF + MOE_NN),
            in_specs=[
                pl.BlockSpec((MOE_TM, HALF_D), lambda b, s, be, ns, bi: (bi[x_blk(s, b)], 0)),
                pl.BlockSpec((None, None, d, MOE_TF), lambda b, s, be, ns, bi: (layer, be[b], 0, up_idx(s, ns, b))),
                pl.BlockSpec((None, None, d, MOE_TF), lambda b, s, be, ns, bi: (layer, be[b], 0, up_idx(s, ns, b))),
                pl.BlockSpec((None, None, D_EXPERT, MOE_TN),
                             lambda b, s, be, ns, bi: (layer, be[b], 0, dn_idx(s, ns, b)))],
            out_specs=pl.BlockSpec((MOE_TM, MOE_TN), lambda b, s, be, ns, bi: (bi[b], dn_idx(s, ns, b))),
            scratch_shapes=[pltpu.VMEM((MOE_TM, d), BF16), pltpu.VMEM((MOE_NF, MOE_TM, MOE_TF), BF16)],
        ),
        compiler_params=_params(("arbitrary", "arbitrary")),
        name="moe_experts",
    )(blk_e, blk_nsub, blk_idx, xs, w_gate, w_up, w_down)


def moe_dispatch(eids, n):
    nslot = n * 2
    eid = eids.reshape(-1)
    experts = jnp.arange(N_EXPERTS, dtype=jnp.int32)
    onehot = (eid[:, None] == experts[None, :]).astype(jnp.int32)
    seen = jnp.cumsum(onehot, axis=0)
    rank = jnp.sum(seen * onehot, axis=1) - 1
    counts = seen[-1]
    nblk = (counts + MOE_TM - 1) // MOE_TM
    bend = jnp.cumsum(nblk)
    bstart = bend - nblk
    dest = jnp.sum(onehot * bstart[None, :], axis=1) * MOE_TM + rank
    n_blocks = nslot // MOE_TM + N_EXPERTS
    row_tok = jnp.zeros((n_blocks * MOE_TM,), jnp.int32).at[dest].set(jnp.arange(nslot, dtype=jnp.int32) // 2)
    pos = dest.reshape(n, 2)
    blk = jnp.arange(n_blocks, dtype=jnp.int32)
    n_used = bend[-1]
    blk_idx = jnp.minimum(blk, n_used - 1)
    blk_e = jnp.sum((blk_idx[:, None] >= bend[None, :]).astype(jnp.int32), axis=1)
    rows_left = counts[blk_e] - (blk_idx - bstart[blk_e]) * MOE_TM
    nsub = (jnp.clip(rows_left, 0, MOE_TM) + MOE_SUB - 1) // MOE_SUB
    blk_nsub = jnp.where(blk < n_used, nsub, 0).astype(jnp.int32)
    return pos, row_tok, blk_e.astype(jnp.int32), blk_nsub, blk_idx


def hier_moe_ln(x, x_packed, gates, eids, layer, w_gate, w_up, w_down, ln_g, ln_b, *, tm):
    n = x.shape[0]
    pos, row_tok, blk_e, blk_nsub, blk_idx = moe_dispatch(eids[:, :2], n)
    xs = x_packed[row_tok]
    ys = moe_experts(xs, blk_e, blk_nsub, blk_idx, w_gate, w_up, w_down, layer)
    return ln_moe_combine(x, ys[pos[:, 0]], ys[pos[:, 1]], gates, ln_g, ln_b, layer, tm=tm)


ROW_TILE = 320
MM_TM = 2080
MM_TN = 256


def _rows_after(full, tail, start):
    return lax.dynamic_update_slice(full, tail.astype(full.dtype), (start, 0))


def kernel(x_prompt, x_sample, cache_k, cache_v, state_conv, state_ssm_re, state_ssm_im, w_in, w_out, attn_sinks, conv_w, ssm_lambda_re, ssm_lambda_im, ssm_log_dt, ssm_b_re, ssm_b_im, ssm_c_re, ssm_c_im, ssm_d, w_glu, w_o, ln1_g, ln1_b, ln2_g, ln2_b, w_router_group, w_router_expert, w_gate, w_up, w_down):
    batch, seq, _ = x_prompt.shape
    db, n_new, _ = x_sample.shape
    n_p = batch * seq
    n_s = db * n_new
    n = n_p + n_s
    assert n % ROW_TILE == 0 and n % MM_TM == 0 and n_new <= T_PAD

    w_route = jnp.concatenate(
        [w_router_group, w_router_expert,
         jnp.zeros((DEPTH, D_MODEL, ROUTE_COLS - N_EXPERT_GROUPS - N_EXPERTS), F32)], axis=-1)

    x = jnp.concatenate([x_prompt.reshape(n_p, D_MODEL), x_sample.reshape(n_s, D_MODEL)], axis=0)
    xb = x.astype(BF16)
    k_p, v_p, conv_p, re_p, im_p = [], [], [], [], []
    k_s, v_s, conv_s, re_s, im_s = [], [], [], [], []
    for layer in range(DEPTH):
        i = layer // 2
        if layer % 2 == 0:
            h = matmul([xb], w_in, i, tm=MM_TM, tn=MM_TN, out_dtype=F32)
            attn = attn_prompt(h, attn_sinks[i], batch, seq)
            gconv, cu_tail = conv_prompt(h, conv_w, i, batch, seq)
            tails = [h[b * seq + seq - WINDOW:(b + 1) * seq, D_ATT:D_ATT + 2 * D_KV] for b in range(batch)]
            kv_tail = jnp.stack(tails)
            k_p.append(kv_tail[:, :, :D_KV].reshape(batch, WINDOW, N_KV_HEADS, HEAD_DIM))
            v_p.append(kv_tail[:, :, D_KV:].reshape(batch, WINDOW, N_KV_HEADS, HEAD_DIM))
            conv_p.append(cu_tail.reshape(batch, 8, D_CONV)[:, 8 - (CONV_W - 1):])
            hs = h[n_p:].reshape(db, n_new, D_IN_EVEN)
            k_new = hs[:, :, D_ATT:D_ATT + D_KV]
            v_new = hs[:, :, D_ATT + D_KV:D_ATT + 2 * D_KV]
            pad_rows = jnp.zeros((db, KC_PAD - WINDOW - n_new, D_KV), F32)
            kc = jnp.concatenate([cache_k[i].reshape(db, WINDOW, D_KV), k_new, pad_rows], axis=1)
            vc = jnp.concatenate([cache_v[i].reshape(db, WINDOW, D_KV), v_new, pad_rows], axis=1)
            q_s = jnp.pad(hs[:, :, :D_ATT], ((0, 0), (0, T_PAD - n_new), (0, 0)))
            attn_s = attn_sample(q_s, kc, vc, attn_sinks[i], n_new)[:, :n_new].reshape(n_s, D_ATT)
            k_s.append(kc[:, n_new:n_new + WINDOW].reshape(db, WINDOW, N_KV_HEADS, HEAD_DIM))
            v_s.append(vc[:, n_new:n_new + WINDOW].reshape(db, WINDOW, N_KV_HEADS, HEAD_DIM))
            off = D_ATT + 2 * D_KV
            tmaj = lambda a: a.transpose(1, 0, 2)
            gconv_s, st_s = conv_sample(tmaj(hs[:, :, off:off + D_CONV]),
                                        tmaj(hs[:, :, off + D_CONV:off + 2 * D_CONV]),
                                        tmaj(hs[:, :, off + 2 * D_CONV:]),
                                        tmaj(state_conv[i]), conv_w, i)
            conv_s.append(tmaj(st_s))
            attn = _rows_after(attn, attn_s, n_p)
            gconv = _rows_after(gconv, tmaj(gconv_s).reshape(n_s, D_CONV), n_p)
            mixed = matmul([attn, gconv], w_out, i, tm=MM_TM, tn=MM_TN, out_dtype=F32)
        else:
            ssm = (ssm_lambda_re[i], ssm_lambda_im[i], ssm_log_dt[i], ssm_b_re[i], ssm_b_im[i],
                   ssm_c_re[i], ssm_c_im[i])
            w1, w2t, ap = s5_operators(*ssm, S5_CHUNK, seq // S5_CHUNK)
            g, gb, hn = s5_prompt(x, w1, w2t, ap, ssm_d, i, batch, seq)
            unpair = lambda s: s.transpose(1, 0, 2).reshape(batch, N_SSM_GROUPS, SSM_STATE)
            re_p.append(unpair(hn[:, 0]))
            im_p.append(unpair(hn[:, 1]))
            x_s = x[n_p:]
            y_s, nre_s, nim_s = s5_apply(x_s.reshape(db, n_new, D_MODEL), state_ssm_re[i], state_ssm_im[i],
                                         ssm, n_new)
            re_s.append(nre_s)
            im_s.append(nim_s)
            g_s = gelu_skip(y_s.reshape(n_s, D_MODEL), x_s, ssm_d, i, tm=n_s)
            g = _rows_after(g, g_s, n_p)
            gb = _rows_after(gb, g_s, n_p)
            z = matmul([gb], w_glu, i, tm=MM_TM, tn=MM_TN, out_dtype=BF16, glu_gate=g)
            mixed = matmul([z], w_o, i, tm=MM_TM, tn=MM_TN, out_dtype=F32)
        x, x_packed, gates, eids = ln_residual_route(x, mixed, ln1_g, ln1_b, w_route, layer, tm=ROW_TILE)
        x, xb = hier_moe_ln(x, x_packed, gates, eids, layer, w_gate, w_up, w_down, ln2_g, ln2_b, tm=ROW_TILE)

    y_prompt = x[:n_p].reshape(batch, seq, D_MODEL)
    y_sample = x[n_p:].reshape(db, n_new, D_MODEL)
    st = jnp.stack
    return (y_prompt, y_sample, st(k_p), st(v_p), st(conv_p), st(re_p), st(im_p),
            st(k_s), st(v_s), st(conv_s), st(re_s), st(im_s))
```

```python
import functools
import math

import jax
import jax.numpy as jnp
from jax import lax
from jax.experimental import pallas as pl
from jax.experimental.pallas import tpu as pltpu

F32 = jnp.float32
BF16 = jnp.bfloat16

D_MODEL = 4096
DEPTH = 4
N_HEADS = 32
N_KV_HEADS = 4
Q_PER_KV = N_HEADS // N_KV_HEADS
HEAD_DIM = 64
WINDOW = 128
D_ATT = N_HEADS * HEAD_DIM
D_KV = N_KV_HEADS * HEAD_DIM
D_CONV = D_MODEL // 2
CONV_W = 3
D_IN_EVEN = D_ATT + 2 * D_KV + 3 * D_CONV
SSM_GROUP = 16
N_SSM_GROUPS = D_MODEL // SSM_GROUP
SSM_STATE = 64
N_EXPERT_GROUPS = 4
EXPERTS_PER_GROUP = 8
N_EXPERTS = N_EXPERT_GROUPS * EXPERTS_PER_GROUP
D_EXPERT = D_MODEL // 4
ALPHA = (2 * DEPTH) ** 0.25
LN_EPS = 1e-5

LANES = 128
VMEM_LIMIT = 58 * 1024 * 1024

COL_BLK = 256
K_BLK = D_ATT // COL_BLK
V_BLK = (D_ATT + D_KV) // COL_BLK
B_BLK = (D_ATT + 2 * D_KV) // COL_BLK
C_BLK = B_BLK + D_CONV // COL_BLK
U_BLK = C_BLK + D_CONV // COL_BLK


def _params(sem):
    return pltpu.CompilerParams(dimension_semantics=sem, vmem_limit_bytes=VMEM_LIMIT)


def _sigmoid(x):
    return 1.0 / (1.0 + jnp.exp(-x))


def _mm_kernel(*refs, n_a, glu):
    a_refs = refs[:n_a]
    w_ref = refs[n_a]
    g_ref = refs[n_a + 1] if glu else None
    o_ref = refs[-1]
    wb = w_ref[...].astype(BF16)
    acc = None
    off = 0
    for a_ref in a_refs:
        kp = a_ref.shape[1]
        part = jnp.dot(a_ref[...], wb[off:off + kp, :], preferred_element_type=F32)
        acc = part if acc is None else acc + part
        off += kp
    if glu:
        acc = g_ref[...] * _sigmoid(acc)
    o_ref[...] = acc.astype(o_ref.dtype)


def matmul(a_parts, w, layer, *, tm, tn, out_dtype, glu_gate=None):
    m = a_parts[0].shape[0]
    kdim, n = w.shape[-2:]
    assert m % tm == 0 and n % tn == 0 and sum(a.shape[1] for a in a_parts) == kdim
    in_specs = [pl.BlockSpec((tm, a.shape[1]), lambda i, j: (i, 0), pipeline_mode=pl.Buffered(1)) for a in a_parts]
    in_specs.append(pl.BlockSpec((None, kdim, tn), lambda i, j: (layer, 0, j)))
    args = list(a_parts) + [w]
    if glu_gate is not None:
        in_specs.append(pl.BlockSpec((tm, tn), lambda i, j: (i, j)))
        args.append(glu_gate)
    return pl.pallas_call(
        functools.partial(_mm_kernel, n_a=len(a_parts), glu=glu_gate is not None),
        out_shape=jax.ShapeDtypeStruct((m, n), out_dtype),
        grid=(m // tm, n // tn),
        in_specs=in_specs,
        out_specs=pl.BlockSpec((tm, tn), lambda i, j: (i, j)),
        compiler_params=_params(("parallel", "parallel")),
        name="matmul",
    )(*args)


def _layer_norm_rows(v, g, b):
    mu = jnp.mean(v, axis=-1, keepdims=True)
    d = v - mu
    var = jnp.mean(d * d, axis=-1, keepdims=True)
    return d * lax.rsqrt(var + LN_EPS) * g + b


HALF_D = D_MODEL // 2


def _ln_kernel(*refs, tail):
    if tail:
        xh_ref, xt_ref, y_ref, g_ref, b_ref, wr_ref, o_ref, op_ref, gate_ref, eid_ref = refs
        x = jnp.where(pl.program_id(0) < pl.num_programs(0) - 1, xh_ref[...], xt_ref[...])
    else:
        x_ref, y_ref, g_ref, b_ref, wr_ref, o_ref, op_ref, gate_ref, eid_ref = refs
        x = x_ref[...]
    out = _layer_norm_rows(ALPHA * x + y_ref[...], g_ref[...], b_ref[...])
    o_ref[...] = out
    gate_ref[...], eid_ref[...] = _route(out, wr_ref[...])
    bits = pltpu.bitcast(out.astype(BF16).astype(F32), jnp.int32)
    op_ref[...] = bits[:, HALF_D:] | lax.shift_right_logical(bits[:, :HALF_D], 16)


def ln_residual_route(x, y, g, b, w_route, layer, *, tm, x_tail=None):
    n, d = y.shape
    row = pl.BlockSpec((tm, d), lambda i: (i, 0))
    par = pl.BlockSpec((None, 1, d), lambda i: (layer, 0, 0))
    lanes = pl.BlockSpec((tm, ROUTE_COLS), lambda i: (i, 0))
    if x_tail is None:
        x_specs, x_args = [row], [x]
    else:
        assert x_tail.shape[0] == tm and x.shape[0] == n - tm
        head_tiles = x.shape[0] // tm
        x_specs = [pl.BlockSpec((tm, d), lambda i: (jnp.minimum(i, head_tiles - 1), 0)),
                   pl.BlockSpec((tm, d), lambda i: (0, 0))]
        x_args = [x, x_tail]
    return pl.pallas_call(
        functools.partial(_ln_kernel, tail=x_tail is not None),
        out_shape=(jax.ShapeDtypeStruct((n, d), F32), jax.ShapeDtypeStruct((n, HALF_D), jnp.int32),
                   jax.ShapeDtypeStruct((n, ROUTE_COLS), F32), jax.ShapeDtypeStruct((n, ROUTE_COLS), jnp.int32)),
        grid=(n // tm,),
        in_specs=x_specs + [row, par, par, pl.BlockSpec((None, d, ROUTE_COLS), lambda i: (layer, 0, 0))],
        out_specs=(row, pl.BlockSpec((tm, HALF_D), lambda i: (i, 0)), lanes, lanes),
        compiler_params=_params(("arbitrary",)),
        name="ln_residual_route",
    )(*x_args, y, g.reshape(DEPTH, 1, d), b.reshape(DEPTH, 1, d), w_route)


def _ln_moe_kernel(x_ref, y1_ref, y2_ref, gate_ref, g_ref, b_ref, *o_refs, split):
    gate = gate_ref[...]
    ffn = y1_ref[...] * gate[:, 0:1] + y2_ref[...] * gate[:, 1:2]
    out = _layer_norm_rows(ALPHA * x_ref[...] + ffn, g_ref[...], b_ref[...])
    if split:
        head_ref, tail_ref = o_refs
        last = pl.num_programs(0) - 1

        @pl.when(pl.program_id(0) < last)
        def _():
            head_ref[...] = out

        @pl.when(pl.program_id(0) == last)
        def _():
            tail_ref[...] = out
    else:
        o_ref, ob_ref = o_refs
        o_ref[...] = out
        ob_ref[...] = out.astype(BF16)


def ln_moe_combine(x, y1, y2, gates, g, b, layer, *, tm, split_tail=False):
    n, d = x.shape
    row = pl.BlockSpec((tm, d), lambda i: (i, 0))
    par = pl.BlockSpec((None, 1, d), lambda i: (layer, 0, 0))
    if split_tail:
        head_tiles = n // tm - 1
        out_shape = (jax.ShapeDtypeStruct((n - tm, d), F32), jax.ShapeDtypeStruct((tm, d), F32))
        out_specs = (pl.BlockSpec((tm, d), lambda i: (jnp.minimum(i, head_tiles - 1), 0)),
                     pl.BlockSpec((tm, d), lambda i: (0, 0)))
    else:
        out_shape = (jax.ShapeDtypeStruct((n, d), F32), jax.ShapeDtypeStruct((n, d), BF16))
        out_specs = (row, row)
    return pl.pallas_call(
        functools.partial(_ln_moe_kernel, split=split_tail),
        out_shape=out_shape,
        grid=(n // tm,),
        in_specs=[row, row, row, pl.BlockSpec((tm, LANES), lambda i: (i, 0)), par, par],
        out_specs=out_specs,
        compiler_params=_params(("arbitrary",)),
        name="ln_moe_combine",
    )(x, y1, y2, gates, g.reshape(DEPTH, 1, d), b.reshape(DEPTH, 1, d))


def _softmax_sink_pv(s, valid, sink_col, vband):
    s = jnp.where(valid, s, -jnp.inf)
    m = jnp.maximum(jnp.max(s, axis=1, keepdims=True), sink_col)
    p = jnp.exp(s - m)
    denom = jnp.sum(p, axis=1, keepdims=True) + jnp.exp(sink_col - m)
    o = jnp.dot(p.astype(BF16), vband, preferred_element_type=F32)
    return o / denom


def _attn_prompt_kernel(sink_ref, q_ref, kc_ref, kp_ref, vc_ref, vp_ref, o_ref):
    nb = pl.program_id(1)
    pair_w = 2 * HEAD_DIM
    r = lax.broadcasted_iota(jnp.int32, (WINDOW, WINDOW), 0)
    c = lax.broadcasted_iota(jnp.int32, (WINDOW, WINDOW), 1)
    from_prev = c > r
    prev_live = from_prev & (nb > 0)
    first_head = lax.broadcasted_iota(jnp.int32, (WINDOW, pair_w), 1) < HEAD_DIM
    nt = lambda a, b: lax.dot_general(a, b, (((1,), (1,)), ((), ())), preferred_element_type=F32)

    def both_heads(t):
        z = jnp.zeros_like(t)
        return jnp.concatenate([jnp.concatenate([t, z], axis=1), jnp.concatenate([z, t], axis=1)], axis=0).astype(BF16)

    for j in range(N_KV_HEADS):
        cols = slice(j * HEAD_DIM, (j + 1) * HEAD_DIM)
        k2p, k2c = both_heads(kp_ref[:, cols]), both_heads(kc_ref[:, cols])
        v2p, v2c = both_heads(vp_ref[:, cols]), both_heads(vc_ref[:, cols])
        for hp in range(j * Q_PER_KV // 2, (j + 1) * Q_PER_KV // 2):
            lanes = slice(hp * pair_w, (hp + 1) * pair_w)
            q2 = q_ref[:, lanes].astype(BF16)
            s_prev = nt(q2, k2p) * (HEAD_DIM ** -0.5)
            s_cur = nt(q2, k2c) * (HEAD_DIM ** -0.5)
            p_prev, p_cur, inv = [], [], []
            for side in range(2):
                half = slice(side * WINDOW, (side + 1) * WINDOW)
                s = jnp.where(prev_live, s_prev[:, half], jnp.where(from_prev, -jnp.inf, s_cur[:, half]))
                sink = sink_ref[2 * hp + side]
                m = jnp.maximum(jnp.max(s, axis=1, keepdims=True), sink)
                p = jnp.exp(s - m)
                inv.append(1.0 / (jnp.sum(p, axis=1, keepdims=True) + jnp.exp(sink - m)))
                p_prev.append(jnp.where(from_prev, p, 0.0))
                p_cur.append(jnp.where(from_prev, 0.0, p))
            o = (jnp.dot(jnp.concatenate(p_prev, axis=1).astype(BF16), v2p, preferred_element_type=F32)
                 + jnp.dot(jnp.concatenate(p_cur, axis=1).astype(BF16), v2c, preferred_element_type=F32))
            o_ref[:, lanes] = (o * jnp.where(first_head, inv[0], inv[1])).astype(o_ref.dtype)


def attn_prompt(h, sinks, batch, seq):
    nb = seq // WINDOW
    cur = lambda col: pl.BlockSpec((WINDOW, COL_BLK), lambda b, n, s: (b * nb + n, col))
    prev = lambda col: pl.BlockSpec((WINDOW, COL_BLK), lambda b, n, s: (b * nb + jnp.maximum(n - 1, 0), col))
    return pl.pallas_call(
        _attn_prompt_kernel,
        out_shape=jax.ShapeDtypeStruct((h.shape[0], D_ATT), BF16),
        grid_spec=pltpu.PrefetchScalarGridSpec(
            num_scalar_prefetch=1,
            grid=(batch, nb),
            in_specs=[pl.BlockSpec((WINDOW, D_ATT), lambda b, n, s: (b * nb + n, 0)),
                      cur(K_BLK), prev(K_BLK), cur(V_BLK), prev(V_BLK)],
            out_specs=pl.BlockSpec((WINDOW, D_ATT), lambda b, n, s: (b * nb + n, 0)),
        ),
        compiler_params=_params(("parallel", "arbitrary")),
        name="attn_prompt",
    )(sinks, h, h, h, h, h)


T_PAD = 8
KC_PAD = WINDOW + T_PAD


def _attn_sample_kernel(sink_ref, q_ref, kc_ref, vc_ref, o_ref, *, n_new):
    rows = Q_PER_KV * T_PAD
    t = lax.broadcasted_iota(jnp.int32, (rows, KC_PAD), 0) & (T_PAD - 1)
    c = lax.broadcasted_iota(jnp.int32, (rows, KC_PAD), 1)
    valid = (c > t) & (c <= t + WINDOW) & (c < WINDOW + n_new)
    for j in range(N_KV_HEADS):
        heads = range(j * Q_PER_KV, (j + 1) * Q_PER_KV)
        qg = jnp.concatenate([q_ref[:, h * HEAD_DIM:(h + 1) * HEAD_DIM] for h in heads], axis=0).astype(BF16)
        cols = slice(j * HEAD_DIM, (j + 1) * HEAD_DIM)
        kband = kc_ref[:, cols].astype(BF16)
        vband = vc_ref[:, cols].astype(BF16)
        s = lax.dot_general(qg, kband, (((1,), (1,)), ((), ())), preferred_element_type=F32) * (HEAD_DIM ** -0.5)
        sink_col = jnp.concatenate([jnp.full((T_PAD, 1), sink_ref[h], F32) for h in heads], axis=0)
        o = _softmax_sink_pv(s, valid, sink_col, vband)
        o = jnp.concatenate([o[i * T_PAD:(i + 1) * T_PAD] for i in range(Q_PER_KV)], axis=1)
        o_ref[:, j * Q_PER_KV * HEAD_DIM:(j + 1) * Q_PER_KV * HEAD_DIM] = o.astype(o_ref.dtype)


def attn_sample(q, kc, vc, sinks, n_new):
    db = q.shape[0]
    return pl.pallas_call(
        functools.partial(_attn_sample_kernel, n_new=n_new),
        out_shape=jax.ShapeDtypeStruct((db, T_PAD, D_ATT), BF16),
        grid_spec=pltpu.PrefetchScalarGridSpec(
            num_scalar_prefetch=1,
            grid=(db,),
            in_specs=[pl.BlockSpec((None, T_PAD, D_ATT), lambda b, s: (b, 0, 0)),
                      pl.BlockSpec((None, KC_PAD, D_KV), lambda b, s: (b, 0, 0)),
                      pl.BlockSpec((None, KC_PAD, D_KV), lambda b, s: (b, 0, 0))],
            out_specs=pl.BlockSpec((None, T_PAD, D_ATT), lambda b, s: (b, 0, 0)),
        ),
        compiler_params=_params(("parallel",)),
        name="attn_sample",
    )(sinks, q, kc, vc)


def _conv_prompt_kernel(b_ref, c_ref, u_ref, w_ref, o_ref, st_ref):
    cu = c_ref[...] * u_ref[...]
    t = lax.broadcasted_iota(jnp.int32, cu.shape, 0)
    sh1 = jnp.where(t >= 1, pltpu.roll(cu, 1, axis=0), 0.0)
    sh2 = jnp.where(t >= 2, pltpu.roll(cu, 2, axis=0), 0.0)
    w = w_ref[...]
    y = sh2 * w[0:1] + sh1 * w[1:2] + cu * w[2:3]
    o_ref[...] = (b_ref[...] * y).astype(o_ref.dtype)
    st_ref[...] = cu[cu.shape[0] - 8:]


def conv_prompt(h, conv_w, layer_i, batch, seq):
    nj = D_CONV // COL_BLK
    col = lambda base: pl.BlockSpec((seq, COL_BLK), lambda b, j: (b, base + j))
    return pl.pallas_call(
        _conv_prompt_kernel,
        out_shape=(jax.ShapeDtypeStruct((h.shape[0], D_CONV), BF16),
                   jax.ShapeDtypeStruct((batch * 8, D_CONV), F32)),
        grid=(batch, nj),
        in_specs=[col(B_BLK), col(C_BLK), col(U_BLK),
                  pl.BlockSpec((None, CONV_W, COL_BLK), lambda b, j: (layer_i, 0, j))],
        out_specs=(pl.BlockSpec((seq, COL_BLK), lambda b, j: (b, j)),
                   pl.BlockSpec((8, COL_BLK), lambda b, j: (b, j))),
        compiler_params=_params(("parallel", "parallel")),
        name="conv_prompt",
    )(h, h, h, conv_w)


def _conv_sample_kernel(b_ref, c_ref, u_ref, buf_ref, w_ref, o_ref, st_ref, *, n_new):
    w = w_ref[...]
    full = [buf_ref[i] for i in range(CONV_W - 1)] + [c_ref[t] * u_ref[t] for t in range(n_new)]
    for t in range(n_new):
        y = full[t] * w[0:1] + full[t + 1] * w[1:2] + full[t + 2] * w[2:3]
        o_ref[t] = (b_ref[t] * y).astype(o_ref.dtype)
    for i in range(CONV_W - 1):
        st_ref[i] = full[n_new + i]


def conv_sample(bg, cg, ug, buf, conv_w, layer_i):
    n_new, db, _ = bg.shape
    full = lambda a: pl.BlockSpec(a.shape, lambda g: (0,) * a.ndim)
    return pl.pallas_call(
        functools.partial(_conv_sample_kernel, n_new=n_new),
        out_shape=(jax.ShapeDtypeStruct((n_new, db, D_CONV), BF16),
                   jax.ShapeDtypeStruct((CONV_W - 1, db, D_CONV), F32)),
        grid=(1,),
        in_specs=[full(bg), full(cg), full(ug), full(buf),
                  pl.BlockSpec((None, CONV_W, D_CONV), lambda g: (layer_i, 0, 0))],
        out_specs=(pl.BlockSpec((n_new, db, D_CONV), lambda g: (0, 0, 0)),
                   pl.BlockSpec((CONV_W - 1, db, D_CONV), lambda g: (0, 0, 0))),
        compiler_params=_params(("arbitrary",)),
        name="conv_sample",
    )(bg, cg, ug, buf, conv_w)


PAIR = 2
PAIR_STATE = PAIR * SSM_STATE


def _s5_kernel(u_ref, w1_ref, w2t_ref, ap_ref, h0_ref, y_ref, hn_ref, *, nbatch, nchunk, width):
    rows = nbatch * nchunk
    z = jnp.dot(u_ref[...], w1_ref[...], preferred_element_type=F32)
    yi = z[:, :width]
    vr = z[:, width:width + PAIR_STATE]
    vi = z[:, width + PAIR_STATE:]
    h0r = h0_ref[0]
    h0i = h0_ref[1]
    if nchunk == 1:
        h0r_rows, h0i_rows = h0r, h0i
    else:
        h0r_rows = jnp.concatenate([jnp.broadcast_to(h0r[b:b + 1], (nchunk, PAIR_STATE)) for b in range(nbatch)], 0)
        h0i_rows = jnp.concatenate([jnp.broadcast_to(h0i[b:b + 1], (nchunk, PAIR_STATE)) for b in range(nbatch)], 0)
    krow = lax.broadcasted_iota(jnp.int32, (rows, PAIR_STATE), 0) & (nchunk - 1)
    ar = ap_ref[0, 0:1]
    ai = ap_ref[0, 1:2]
    first = krow == 0
    vr = vr + jnp.where(first, ar * h0r_rows - ai * h0i_rows, 0.0)
    vi = vi + jnp.where(first, ar * h0i_rows + ai * h0r_rows, 0.0)
    step = 0
    while (1 << step) < nchunk:
        d = 1 << step
        ar = ap_ref[step, 0:1]
        ai = ap_ref[step, 1:2]
        keep = krow >= d
        sr = jnp.where(keep, pltpu.roll(vr, d, axis=0), 0.0)
        si = jnp.where(keep, pltpu.roll(vi, d, axis=0), 0.0)
        vr, vi = vr + (ar * sr - ai * si), vi + (ar * si + ai * sr)
        step += 1
    if nchunk == 1:
        hr, hi = h0r_rows, h0i_rows
    else:
        hr = jnp.where(first, h0r_rows, pltpu.roll(vr, 1, axis=0))
        hi = jnp.where(first, h0i_rows, pltpu.roll(vi, 1, axis=0))
    hcat = jnp.concatenate([hr, hi], axis=1).astype(BF16)
    y_ref[...] = yi + lax.dot_general(hcat, w2t_ref[...], (((1,), (1,)), ((), ())), preferred_element_type=F32)
    if nchunk == 1:
        hn_ref[0] = vr
        hn_ref[1] = vi
    else:
        last = [b * nchunk + nchunk - 1 for b in range(nbatch)]
        hn_ref[0] = jnp.concatenate([vr[i:i + 1] for i in last], axis=0)
        hn_ref[1] = jnp.concatenate([vi[i:i + 1] for i in last], axis=0)


def s5_scan(u, w1, w2t, ap, h0, nbatch, nchunk):
    g2, rows, width = u.shape
    nstep = ap.shape[1]
    return pl.pallas_call(
        functools.partial(_s5_kernel, nbatch=nbatch, nchunk=nchunk, width=width),
        out_shape=(jax.ShapeDtypeStruct((g2, rows, width), F32),
                   jax.ShapeDtypeStruct((g2, 2, nbatch, PAIR_STATE), F32)),
        grid=(g2,),
        in_specs=[pl.BlockSpec((None, rows, width), lambda g: (g, 0, 0)),
                  pl.BlockSpec((None, width, width + 2 * PAIR_STATE), lambda g: (g, 0, 0)),
                  pl.BlockSpec((None, width, 2 * PAIR_STATE), lambda g: (g, 0, 0)),
                  pl.BlockSpec((None, nstep, 2, PAIR_STATE), lambda g: (g, 0, 0, 0)),
                  pl.BlockSpec((None, 2, nbatch, PAIR_STATE), lambda g: (g, 0, 0, 0))],
        out_specs=(pl.BlockSpec((None, rows, width), lambda g: (g, 0, 0)),
                   pl.BlockSpec((None, 2, nbatch, PAIR_STATE), lambda g: (g, 0, 0, 0))),
        compiler_params=_params(("parallel",)),
        name="s5_scan",
    )(u, w1, w2t, ap, h0)


def s5_apply(x, h0_re, h0_im, ssm, chunk):
    bsz, t, _ = x.shape
    nchunk = t // chunk
    g2 = N_SSM_GROUPS // PAIR
    width = PAIR * chunk * SSM_GROUP
    w1, w2t, ap, _ = s5_operators(*ssm, chunk, nchunk)
    u = x.astype(BF16).reshape(bsz, nchunk, chunk, g2, PAIR, SSM_GROUP)
    u = u.transpose(3, 0, 1, 4, 2, 5).reshape(g2, bsz * nchunk, width)
    pair_state = lambda s: s.reshape(bsz, g2, PAIR_STATE).transpose(1, 0, 2)
    h0 = jnp.stack([pair_state(h0_re), pair_state(h0_im)], axis=1)
    y, hn = s5_scan(u, w1, w2t, ap, h0, bsz, nchunk)
    y = y.reshape(g2, bsz, nchunk, PAIR, chunk, SSM_GROUP).transpose(1, 2, 4, 0, 3, 5).reshape(bsz, t, D_MODEL)
    unpair = lambda s: s.transpose(1, 0, 2).reshape(bsz, N_SSM_GROUPS, SSM_STATE)
    return y, unpair(hn[:, 0]), unpair(hn[:, 1])


def _gelu_tanh(v):
    inner = math.sqrt(2.0 / math.pi) * (v + 0.044715 * (v * v * v))
    return 0.5 * v * (1.0 + jnp.tanh(inner))


GROUPS_PER_TILE = LANES // SSM_GROUP
PAIRS_PER_TILE = GROUPS_PER_TILE // PAIR
S5_CHUNK = 16


def _block_transpose8(arrs):
    arrs = list(arrs)
    lane = lax.broadcasted_iota(jnp.int32, arrs[0].shape, 1)
    for d in (4, 2, 1):
        clear = (lane & (d * SSM_GROUP)) == 0
        nxt = list(arrs)
        for i in range(GROUPS_PER_TILE):
            if i & d:
                continue
            lo, hi = arrs[i], arrs[i + d]
            nxt[i] = jnp.where(clear, lo, pltpu.roll(hi, d * SSM_GROUP, axis=1))
            nxt[i + d] = jnp.where(clear, pltpu.roll(lo, LANES - d * SSM_GROUP, axis=1), hi)
        arrs = nxt
    return arrs


def _shift_lanes(a, k):
    if k == 0:
        return a
    return jnp.concatenate([jnp.zeros((a.shape[0], k), a.dtype), a[:, :a.shape[1] - k]], axis=1)


def _dot_nt_3pass(a, b):
    ah, al = _split_bf16(a)
    bh, bl = _split_bf16(b)
    nt = lambda u, v: lax.dot_general(u, v, (((1,), (1,)), ((), ())), preferred_element_type=F32)
    return nt(ah, bh) + (nt(ah, bl) + nt(al, bh))


def _s5_ops_kernel(cr_ref, ci_ref, btr_ref, bti_ref, pwr_ref, pwi_ref, w1_ref, w2t_ref, *, chunk):
    nl = chunk
    wg = nl * SSM_GROUP
    rep = lambda a, lo: jnp.concatenate(
        [jnp.broadcast_to(a[d:d + 1], (SSM_GROUP, SSM_STATE)) for d in range(lo, lo + nl)], axis=0)
    tile = lambda a: jnp.concatenate([a] * nl, axis=0)
    for q in range(PAIRS_PER_TILE):
        _s5_pair_operators(q, cr_ref, ci_ref, btr_ref, bti_ref, pwr_ref, pwi_ref, w1_ref, w2t_ref, nl, wg, rep, tile)


def _s5_pair_operators(q, cr_ref, ci_ref, btr_ref, bti_ref, pwr_ref, pwi_ref, w1_ref, w2t_ref, nl, wg, rep, tile):
    ky, ksr, ksi, khr, khi = [], [], [], [], []
    for g in range(PAIR * q, PAIR * q + PAIR):
        cr, ci = tile(cr_ref[g]), tile(ci_ref[g])
        btr, bti = btr_ref[g], bti_ref[g]
        pwr, pwi = pwr_ref[g], pwi_ref[g]
        p0r, p0i = rep(pwr, 0), rep(pwi, 0)
        m0r = cr * p0r - ci * p0i
        m0i = cr * p0i + ci * p0r
        r = _dot_nt_3pass(btr, m0r) - _dot_nt_3pass(bti, m0i)
        ky.append(jnp.concatenate([_shift_lanes(r, s * SSM_GROUP) for s in range(nl)], axis=0))
        sr, si = [], []
        for s in range(nl):
            pr = pwr[nl - 1 - s:nl - s]
            pi = pwi[nl - 1 - s:nl - s]
            sr.append(btr * pr - bti * pi)
            si.append(btr * pi + bti * pr)
        ksr.append(jnp.concatenate(sr, axis=0))
        ksi.append(jnp.concatenate(si, axis=0))
        p1r, p1i = rep(pwr, 1), rep(pwi, 1)
        khr.append(cr * p1r - ci * p1i)
        khi.append(-(cr * p1i + ci * p1r))
    zy = jnp.zeros((wg, wg), F32)
    zs = jnp.zeros((wg, SSM_STATE), F32)
    top = jnp.concatenate([ky[0], zy, ksr[0], zs, ksi[0], zs], axis=1)
    bot = jnp.concatenate([zy, ky[1], zs, ksr[1], zs, ksi[1]], axis=1)
    w1_ref[q] = jnp.concatenate([top, bot], axis=0).astype(BF16)
    w2t_ref[q] = jnp.concatenate([jnp.concatenate([khr[0], zs, khi[0], zs], axis=1),
                                  jnp.concatenate([zs, khr[1], zs, khi[1]], axis=1)], axis=0).astype(BF16)


def s5_operators(lam_re, lam_im, log_dt, b_re, b_im, c_re, c_im, chunk, nchunk):
    g, p = lam_re.shape
    g2 = g // PAIR
    wg = chunk * SSM_GROUP
    ldt = lax.complex(lam_re, lam_im) * jnp.exp(log_dt)[:, None]
    a_bar = jnp.exp(ldt)
    b_bar = ((a_bar - 1.0) / lax.complex(lam_re, lam_im))[..., None] * lax.complex(b_re, b_im)
    pw = jnp.exp(ldt[:, None, :] * jnp.arange(chunk + 1, dtype=F32)[None, :, None])
    bt = b_bar.transpose(0, 2, 1)
    grp = lambda a: pl.BlockSpec((GROUPS_PER_TILE,) + a.shape[1:], lambda j: (j, 0, 0))
    args = (c_re, c_im, bt.real, bt.imag, pw.real, pw.imag)
    w1, w2t = pl.pallas_call(
        functools.partial(_s5_ops_kernel, chunk=chunk),
        out_shape=(jax.ShapeDtypeStruct((g2, PAIR * wg, PAIR * wg + 2 * PAIR_STATE), BF16),
                   jax.ShapeDtypeStruct((g2, PAIR * wg, 2 * PAIR_STATE), BF16)),
        grid=(g2 // PAIRS_PER_TILE,),
        in_specs=[grp(a) for a in args],
        out_specs=(pl.BlockSpec((PAIRS_PER_TILE, PAIR * wg, PAIR * wg + 2 * PAIR_STATE), lambda j: (j, 0, 0)),
                   pl.BlockSpec((PAIRS_PER_TILE, PAIR * wg, 2 * PAIR_STATE), lambda j: (j, 0, 0))),
        compiler_params=_params(("parallel",)),
        name="s5_operators",
    )(*args)
    nstep = max(1, (nchunk - 1).bit_length())
    mult = (chunk * (2 ** jnp.arange(nstep))).astype(F32)
    ap = jnp.exp(ldt[:, None, :] * mult[None, :, None])
    pair = lambda t: t.reshape(g2, PAIR, -1, p).transpose(0, 2, 1, 3).reshape(g2, -1, PAIR_STATE)
    ap8 = jnp.exp(ldt[:, None, :] * (chunk * jnp.arange(1, SUBLANES + 1, dtype=F32))[None, :, None])
    return (w1, w2t, jnp.stack([pair(ap.real), pair(ap.imag)], axis=2),
            jnp.stack([pair(ap8.real), pair(ap8.imag)], axis=1))


SUBLANES = 8


def _scan_chunk_rows(vr, vi, ap_ref, ap8_ref, q, nbatch, nchunk):
    row = lax.broadcasted_iota(jnp.int32, (SUBLANES, PAIR_STATE), 0)
    p8r, p8i = ap8_ref[q, 0], ap8_ref[q, 1]
    tiles_r, tiles_i = [], []
    for b in range(nbatch):
        cr = ci = None
        for t in range(nchunk // SUBLANES):
            r0 = b * nchunk + t * SUBLANES
            xr, xi = vr[r0:r0 + SUBLANES], vi[r0:r0 + SUBLANES]
            for step in range(3):
                d = 1 << step
                ar, ai = ap_ref[q, step, 0:1], ap_ref[q, step, 1:2]
                keep = row >= d
                sr = jnp.where(keep, pltpu.roll(xr, d, axis=0), 0.0)
                si = jnp.where(keep, pltpu.roll(xi, d, axis=0), 0.0)
                xr, xi = xr + (ar * sr - ai * si), xi + (ar * si + ai * sr)
            if cr is not None:
                xr, xi = xr + (p8r * cr - p8i * ci), xi + (p8r * ci + p8i * cr)
            cr = jnp.broadcast_to(xr[SUBLANES - 1:], (SUBLANES, PAIR_STATE))
            ci = jnp.broadcast_to(xi[SUBLANES - 1:], (SUBLANES, PAIR_STATE))
            tiles_r.append(xr)
            tiles_i.append(xi)
    return jnp.concatenate(tiles_r, axis=0), jnp.concatenate(tiles_i, axis=0)


def _s5_prompt_kernel(x_ref, w1_ref, w2t_ref, ap_ref, ap8_ref, d_ref, g_ref, gb_ref, hn_ref, *, nbatch, nchunk):
    rows = nbatch * nchunk
    half = S5_CHUNK // 2
    xs = [x_ref[pl.ds(s, rows, stride=S5_CHUNK), :] for s in range(S5_CHUNK)]
    v0 = _block_transpose8(xs[:half])
    v1 = _block_transpose8(xs[half:])
    krow = lax.broadcasted_iota(jnp.int32, (rows, PAIR_STATE), 0) & (nchunk - 1)
    width = PAIR * S5_CHUNK * SSM_GROUP
    y0 = [None] * GROUPS_PER_TILE
    y1 = [None] * GROUPS_PER_TILE
    for q in range(PAIRS_PER_TILE):
        ga, gb = PAIR * q, PAIR * q + 1
        u = jnp.concatenate([v0[ga], v1[ga], v0[gb], v1[gb]], axis=1).astype(BF16)
        z = jnp.dot(u, w1_ref[q], preferred_element_type=F32)
        vr = z[:, width:width + PAIR_STATE]
        vi = z[:, width + PAIR_STATE:]
        vr, vi = _scan_chunk_rows(vr, vi, ap_ref, ap8_ref, q, nbatch, nchunk)
        first = krow == 0
        hr = jnp.where(first, 0.0, pltpu.roll(vr, 1, axis=0))
        hi = jnp.where(first, 0.0, pltpu.roll(vi, 1, axis=0))
        hcat = jnp.concatenate([hr, hi], axis=1).astype(BF16)
        y = z[:, :width] + lax.dot_general(hcat, w2t_ref[q], (((1,), (1,)), ((), ())), preferred_element_type=F32)
        y0[ga], y1[ga] = y[:, 0:LANES], y[:, LANES:2 * LANES]
        y0[gb], y1[gb] = y[:, 2 * LANES:3 * LANES], y[:, 3 * LANES:]
        last = [b * nchunk + nchunk - 1 for b in range(nbatch)]
        hn_ref[q, 0] = jnp.concatenate([vr[i:i + 1] for i in last], axis=0)
        hn_ref[q, 1] = jnp.concatenate([vi[i:i + 1] for i in last], axis=0)
    ys = _block_transpose8(y0) + _block_transpose8(y1)
    dskip = d_ref[...]
    for s in range(S5_CHUNK):
        g_ref[pl.ds(s, rows, stride=S5_CHUNK), :] = _gelu_tanh(ys[s] + dskip * xs[s])
    gb_ref[...] = g_ref[...].astype(BF16)


def s5_prompt(x, w1, w2t, ap, ap8, d, layer_i, nbatch, seq):
    n, dm = x.shape
    nchunk = seq // S5_CHUNK
    n_p = nbatch * seq
    nstep = ap.shape[1]
    g2 = w1.shape[0]
    width = PAIR * S5_CHUNK * SSM_GROUP
    tile = pl.BlockSpec((n_p, LANES), lambda j: (0, j))
    return pl.pallas_call(
        functools.partial(_s5_prompt_kernel, nbatch=nbatch, nchunk=nchunk),
        out_shape=(jax.ShapeDtypeStruct((n, dm), F32), jax.ShapeDtypeStruct((n, dm), BF16),
                   jax.ShapeDtypeStruct((g2, 2, nbatch, PAIR_STATE), F32)),
        grid=(dm // LANES,),
        in_specs=[tile,
                  pl.BlockSpec((PAIRS_PER_TILE, width, width + 2 * PAIR_STATE), lambda j: (j, 0, 0)),
                  pl.BlockSpec((PAIRS_PER_TILE, width, 2 * PAIR_STATE), lambda j: (j, 0, 0)),
                  pl.BlockSpec((PAIRS_PER_TILE, nstep, 2, PAIR_STATE), lambda j: (j, 0, 0, 0)),
                  pl.BlockSpec((PAIRS_PER_TILE, 2, SUBLANES, PAIR_STATE), lambda j: (j, 0, 0, 0)),
                  pl.BlockSpec((None, 1, LANES), lambda j: (layer_i, 0, j))],
        out_specs=(tile, tile,
                   pl.BlockSpec((PAIRS_PER_TILE, 2, nbatch, PAIR_STATE), lambda j: (j, 0, 0, 0))),
        compiler_params=_params(("parallel",)),
        name="s5_prompt",
    )(x, w1, w2t, ap, ap8, d.reshape(d.shape[0], 1, dm))


def _gelu_skip_kernel(y_ref, x_ref, d_ref, o_ref):
    o_ref[...] = _gelu_tanh(y_ref[...] + d_ref[...] * x_ref[...])


def gelu_skip(y, x, d, layer_i, *, tm):
    n, dm = x.shape
    row = pl.BlockSpec((tm, dm), lambda i: (i, 0))
    return pl.pallas_call(
        _gelu_skip_kernel,
        out_shape=jax.ShapeDtypeStruct((n, dm), F32),
        grid=(n // tm,),
        in_specs=[row, row, pl.BlockSpec((None, 1, dm), lambda i: (layer_i, 0, 0))],
        out_specs=row,
        compiler_params=_params(("parallel",)),
        name="gelu_skip",
    )(y, x, d.reshape(d.shape[0], 1, dm))


ROUTE_COLS = LANES


def _split_bf16(v):
    hi = v.astype(BF16)
    lo = (v - hi.astype(F32)).astype(BF16)
    return hi, lo


def _route(x, w):
    xh, xl = _split_bf16(x)
    wh, wl = _split_bf16(w)
    dot = lambda a, b: jnp.dot(a, b, preferred_element_type=F32)
    logits = dot(xh, wh) + (dot(xh, wl) + dot(xl, wh))
    lane = lax.broadcasted_iota(jnp.int32, logits.shape, 1).astype(F32)
    neg = -jnp.inf
    big = float(ROUTE_COLS)
    lg = jnp.where(lane < N_EXPERT_GROUPS, logits, neg)
    m = jnp.max(lg, axis=1, keepdims=True)
    grp = jnp.min(jnp.where(lg == m, lane, big), axis=1, keepdims=True)
    gate_g = 1.0 / jnp.sum(jnp.exp(lg - m), axis=1, keepdims=True)
    lo = N_EXPERT_GROUPS + grp * EXPERTS_PER_GROUP
    le = jnp.where((lane >= lo) & (lane < lo + EXPERTS_PER_GROUP), logits, neg)
    t1 = jnp.max(le, axis=1, keepdims=True)
    i1 = jnp.min(jnp.where(le == t1, lane, big), axis=1, keepdims=True)
    le2 = jnp.where(lane == i1, neg, le)
    t2 = jnp.max(le2, axis=1, keepdims=True)
    i2 = jnp.min(jnp.where(le2 == t2, lane, big), axis=1, keepdims=True)
    e = jnp.exp(t2 - t1)
    p1 = 1.0 / (1.0 + e)
    p2 = e / (1.0 + e)
    gates = jnp.where(lane == 0.0, gate_g * p1, jnp.where(lane == 1.0, gate_g * p2, 0.0))
    eid = jnp.where(lane == 0.0, i1, jnp.where(lane == 1.0, i2, float(N_EXPERT_GROUPS))) - N_EXPERT_GROUPS
    return gates, eid.astype(jnp.int32)


MOE_SUB = 128
MOE_NSUB = 6
MOE_KC = 1024
MOE_TM = MOE_SUB * MOE_NSUB
MOE_TF = 256
MOE_TN = 1024
MOE_NF = D_EXPERT // MOE_TF
MOE_NN = D_MODEL // MOE_TN


def _moe_kernel(be_ref, ns_ref, bi_ref, x_ref, wg_ref, wu_ref, wd_ref, o_ref, xb_ref, hid_ref):
    blk = pl.program_id(0)
    s = pl.program_id(1)
    nsub = ns_ref[blk]
    live = nsub > 0

    @pl.when(live & (s == 0))
    def _():
        words = x_ref[...]
        xb_ref[:, :HALF_D] = pltpu.bitcast(words << 16, F32).astype(BF16)
        xb_ref[:, HALF_D:] = pltpu.bitcast(words & -65536, F32).astype(BF16)

    for n in range(1, MOE_NSUB + 1):
        rows = n * MOE_SUB

        @pl.when((nsub == n) & (s < MOE_NF))
        def _():
            gate = up = None
            for kc in range(D_MODEL // MOE_KC):
                ks = slice(kc * MOE_KC, (kc + 1) * MOE_KC)
                xr = xb_ref[0:rows, ks]
                gp = jnp.dot(xr, wg_ref[ks, :].astype(BF16), preferred_element_type=F32)
                upp = jnp.dot(xr, wu_ref[ks, :].astype(BF16), preferred_element_type=F32)
                gate = gp if gate is None else gate + gp
                up = upp if up is None else up + upp
            hid_ref[s, 0:rows, :] = ((gate * _sigmoid(gate)) * up).astype(BF16)

        @pl.when((nsub == n) & (s >= MOE_NF))
        def _():
            out = None
            for f in range(MOE_NF):
                fs = slice(f * MOE_TF, (f + 1) * MOE_TF)
                part = jnp.dot(hid_ref[f, 0:rows, :], wd_ref[fs, :].astype(BF16), preferred_element_type=F32)
                out = part if out is None else out + part
            o_ref[0:rows, :] = out
            if rows < MOE_TM:
                o_ref[rows:, :] = jnp.zeros((MOE_TM - rows, MOE_TN), F32)


def moe_experts(xs, blk_e, blk_nsub, blk_idx, w_gate, w_up, w_down, layer):
    p = xs.shape[0]
    d = D_MODEL
    nblk = p // MOE_TM
    up_idx = lambda s, ns, b: jnp.where(ns[b] > 0, jnp.minimum(s, MOE_NF - 1), MOE_NF - 1)
    dn_idx = lambda s, ns, b: jnp.where(ns[b] > 0, jnp.maximum(s - MOE_NF, 0), MOE_NN - 1)
    x_blk = lambda s, b: jnp.minimum(b + (s >= MOE_NF).astype(jnp.int32), nblk - 1)
    return pl.pallas_call(
        _moe_kernel,
        out_shape=jax.ShapeDtypeStruct((p, d), F32),
        grid_spec=pltpu.PrefetchScalarGridSpec(
            num_scalar_prefetch=3,
            grid=(nblk, MOE_NF + MOE_NN),
            in_specs=[
                pl.BlockSpec((MOE_TM, HALF_D), lambda b, s, be, ns, bi: (bi[x_blk(s, b)], 0)),
                pl.BlockSpec((None, None, d, MOE_TF), lambda b, s, be, ns, bi: (layer, be[b], 0, up_idx(s, ns, b))),
                pl.BlockSpec((None, None, d, MOE_TF), lambda b, s, be, ns, bi: (layer, be[b], 0, up_idx(s, ns, b))),
                pl.BlockSpec((None, None, D_EXPERT, MOE_TN),
                             lambda b, s, be, ns, bi: (layer, be[b], 0, dn_idx(s, ns, b)))],
            out_specs=pl.BlockSpec((MOE_TM, MOE_TN), lambda b, s, be, ns, bi: (bi[b], dn_idx(s, ns, b))),
            scratch_shapes=[pltpu.VMEM((MOE_TM, d), BF16), pltpu.VMEM((MOE_NF, MOE_TM, MOE_TF), BF16)],
        ),
        compiler_params=_params(("arbitrary", "arbitrary")),
        name="moe_experts",
    )(blk_e, blk_nsub, blk_idx, xs, w_gate, w_up, w_down)


def moe_dispatch(eids, n):
    nslot = n * 2
    eid = eids.reshape(-1)
    experts = jnp.arange(N_EXPERTS, dtype=jnp.int32)
    onehot = (eid[:, None] == experts[None, :]).astype(jnp.int32)
    seen = jnp.cumsum(onehot, axis=0)
    rank = jnp.sum(seen * onehot, axis=1) - 1
    counts = seen[-1]
    nblk = (counts + MOE_TM - 1) // MOE_TM
    bend = jnp.cumsum(nblk)
    bstart = bend - nblk
    dest = jnp.sum(onehot * bstart[None, :], axis=1) * MOE_TM + rank
    n_blocks = nslot // MOE_TM + N_EXPERTS
    row_tok = jnp.zeros((n_blocks * MOE_TM,), jnp.int32).at[dest].set(jnp.arange(nslot, dtype=jnp.int32) // 2)
    pos = dest.reshape(n, 2)
    blk = jnp.arange(n_blocks, dtype=jnp.int32)
    n_used = bend[-1]
    blk_idx = jnp.minimum(blk, n_used - 1)
    blk_e = jnp.sum((blk_idx[:, None] >= bend[None, :]).astype(jnp.int32), axis=1)
    rows_left = counts[blk_e] - (blk_idx - bstart[blk_e]) * MOE_TM
    nsub = (jnp.clip(rows_left, 0, MOE_TM) + MOE_SUB - 1) // MOE_SUB
    blk_nsub = jnp.where(blk < n_used, nsub, 0).astype(jnp.int32)
    return pos, row_tok, blk_e.astype(jnp.int32), blk_nsub, blk_idx


def hier_moe_ln(x, x_packed, gates, eids, layer, w_gate, w_up, w_down, ln_g, ln_b, *, tm, split_tail=False):
    n = x.shape[0]
    pos, row_tok, blk_e, blk_nsub, blk_idx = moe_dispatch(eids[:, :2], n)
    xs = x_packed[row_tok]
    ys = moe_experts(xs, blk_e, blk_nsub, blk_idx, w_gate, w_up, w_down, layer)
    return ln_moe_combine(x, ys[pos[:, 0]], ys[pos[:, 1]], gates, ln_g, ln_b, layer, tm=tm, split_tail=split_tail)


ROW_TILE = 320
MM_TM = 2080
MM_TN = 256


def _rows_after(full, tail, start):
    return lax.dynamic_update_slice(full, tail.astype(full.dtype), (start, 0))


def kernel(x_prompt, x_sample, cache_k, cache_v, state_conv, state_ssm_re, state_ssm_im, w_in, w_out, attn_sinks, conv_w, ssm_lambda_re, ssm_lambda_im, ssm_log_dt, ssm_b_re, ssm_b_im, ssm_c_re, ssm_c_im, ssm_d, w_glu, w_o, ln1_g, ln1_b, ln2_g, ln2_b, w_router_group, w_router_expert, w_gate, w_up, w_down):
    batch, seq, _ = x_prompt.shape
    db, n_new, _ = x_sample.shape
    n_p = batch * seq
    n_s = db * n_new
    n = n_p + n_s
    assert n % ROW_TILE == 0 and n % MM_TM == 0 and n_new <= T_PAD and n_p % n_s == 0 and n_s % SUBLANES == 0

    w_route = jnp.concatenate(
        [w_router_group, w_router_expert,
         jnp.zeros((DEPTH, D_MODEL, ROUTE_COLS - N_EXPERT_GROUPS - N_EXPERTS), F32)], axis=-1)

    x_head, x_tail = x_prompt.reshape(n_p, D_MODEL), x_sample.reshape(n_s, D_MODEL)
    xb = jnp.concatenate([x_head.astype(BF16), x_tail.astype(BF16)], axis=0)
    k_p, v_p, conv_p, re_p, im_p = [], [], [], [], []
    k_s, v_s, conv_s, re_s, im_s = [], [], [], [], []
    for layer in range(DEPTH):
        i = layer // 2
        if layer % 2 == 0:
            h = matmul([xb], w_in, i, tm=MM_TM, tn=MM_TN, out_dtype=F32)
            attn = attn_prompt(h, attn_sinks[i], batch, seq)
            gconv, cu_tail = conv_prompt(h, conv_w, i, batch, seq)
            tails = [h[b * seq + seq - WINDOW:(b + 1) * seq, D_ATT:D_ATT + 2 * D_KV] for b in range(batch)]
            kv_tail = jnp.stack(tails)
            k_p.append(kv_tail[:, :, :D_KV].reshape(batch, WINDOW, N_KV_HEADS, HEAD_DIM))
            v_p.append(kv_tail[:, :, D_KV:].reshape(batch, WINDOW, N_KV_HEADS, HEAD_DIM))
            conv_p.append(cu_tail.reshape(batch, 8, D_CONV)[:, 8 - (CONV_W - 1):])
            hs = h[n_p:].reshape(db, n_new, D_IN_EVEN)
            k_new = hs[:, :, D_ATT:D_ATT + D_KV]
            v_new = hs[:, :, D_ATT + D_KV:D_ATT + 2 * D_KV]
            pad_rows = jnp.zeros((db, KC_PAD - WINDOW - n_new, D_KV), F32)
            kc = jnp.concatenate([cache_k[i].reshape(db, WINDOW, D_KV), k_new, pad_rows], axis=1)
            vc = jnp.concatenate([cache_v[i].reshape(db, WINDOW, D_KV), v_new, pad_rows], axis=1)
            q_s = jnp.pad(hs[:, :, :D_ATT], ((0, 0), (0, T_PAD - n_new), (0, 0)))
            attn_s = attn_sample(q_s, kc, vc, attn_sinks[i], n_new)[:, :n_new].reshape(n_s, D_ATT)
            k_s.append(kc[:, n_new:n_new + WINDOW].reshape(db, WINDOW, N_KV_HEADS, HEAD_DIM))
            v_s.append(vc[:, n_new:n_new + WINDOW].reshape(db, WINDOW, N_KV_HEADS, HEAD_DIM))
            off = D_ATT + 2 * D_KV
            tmaj = lambda a: a.transpose(1, 0, 2)
            gconv_s, st_s = conv_sample(tmaj(hs[:, :, off:off + D_CONV]),
                                        tmaj(hs[:, :, off + D_CONV:off + 2 * D_CONV]),
                                        tmaj(hs[:, :, off + 2 * D_CONV:]),
                                        tmaj(state_conv[i]), conv_w, i)
            conv_s.append(tmaj(st_s))
            attn = _rows_after(attn, attn_s, n_p)
            gconv = _rows_after(gconv, tmaj(gconv_s).reshape(n_s, D_CONV), n_p)
            mixed = matmul([attn, gconv], w_out, i, tm=MM_TM, tn=MM_TN, out_dtype=F32)
        else:
            ssm = (ssm_lambda_re[i], ssm_lambda_im[i], ssm_log_dt[i], ssm_b_re[i], ssm_b_im[i],
                   ssm_c_re[i], ssm_c_im[i])
            w1, w2t, ap, ap8 = s5_operators(*ssm, S5_CHUNK, seq // S5_CHUNK)
            g, gb, hn = s5_prompt(x, w1, w2t, ap, ap8, ssm_d, i, batch, seq)
            unpair = lambda s: s.transpose(1, 0, 2).reshape(batch, N_SSM_GROUPS, SSM_STATE)
            re_p.append(unpair(hn[:, 0]))
            im_p.append(unpair(hn[:, 1]))
            x_s = x[n_p:]
            y_s, nre_s, nim_s = s5_apply(x_s.reshape(db, n_new, D_MODEL), state_ssm_re[i], state_ssm_im[i],
                                         ssm, n_new)
            re_s.append(nre_s)
            im_s.append(nim_s)
            g_s = gelu_skip(y_s.reshape(n_s, D_MODEL), x_s, ssm_d, i, tm=n_s)
            g = _rows_after(g, g_s, n_p)
            gb = _rows_after(gb, g_s, n_p)
            z = matmul([gb], w_glu, i, tm=MM_TM, tn=MM_TN, out_dtype=BF16, glu_gate=g)
            mixed = matmul([z], w_o, i, tm=MM_TM, tn=MM_TN, out_dtype=F32)
        if layer == 0:
            x, x_packed, gates, eids = ln_residual_route(x_head, mixed, ln1_g, ln1_b, w_route, layer, tm=n_s,
                                                         x_tail=x_tail)
        else:
            x, x_packed, gates, eids = ln_residual_route(x, mixed, ln1_g, ln1_b, w_route, layer, tm=ROW_TILE)
        last = layer == DEPTH - 1
        x, xb = hier_moe_ln(x, x_packed, gates, eids, layer, w_gate, w_up, w_down, ln2_g, ln2_b,
                            tm=n_s if last else ROW_TILE, split_tail=last)

    y_prompt = x.reshape(batch, seq, D_MODEL)
    y_sample = xb.reshape(db, n_new, D_MODEL)
    st = jnp.stack
    return (y_prompt, y_sample, st(k_p), st(v_p), st(conv_p), st(re_p), st(im_p),
            st(k_s), st(v_s), st(conv_s), st(re_s), st(im_s))
```

```python
import functools
import math

import jax
import jax.numpy as jnp
from jax import lax
from jax.experimental import pallas as pl
from jax.experimental.pallas import tpu as pltpu

F32 = jnp.float32
BF16 = jnp.bfloat16

D_MODEL = 4096
DEPTH = 4
N_HEADS = 32
N_KV_HEADS = 4
Q_PER_KV = N_HEADS // N_KV_HEADS
HEAD_DIM = 64
WINDOW = 128
D_ATT = N_HEADS * HEAD_DIM
D_KV = N_KV_HEADS * HEAD_DIM
D_CONV = D_MODEL // 2
CONV_W = 3
D_IN_EVEN = D_ATT + 2 * D_KV + 3 * D_CONV
SSM_GROUP = 16
N_SSM_GROUPS = D_MODEL // SSM_GROUP
SSM_STATE = 64
N_EXPERT_GROUPS = 4
EXPERTS_PER_GROUP = 8
N_EXPERTS = N_EXPERT_GROUPS * EXPERTS_PER_GROUP
D_EXPERT = D_MODEL // 4
ALPHA = (2 * DEPTH) ** 0.25
LN_EPS = 1e-5

LANES = 128
VMEM_LIMIT = 58 * 1024 * 1024

COL_BLK = 256
K_BLK = D_ATT // COL_BLK
V_BLK = (D_ATT + D_KV) // COL_BLK
B_BLK = (D_ATT + 2 * D_KV) // COL_BLK
C_BLK = B_BLK + D_CONV // COL_BLK
U_BLK = C_BLK + D_CONV // COL_BLK


def _params(sem):
    return pltpu.CompilerParams(dimension_semantics=sem, vmem_limit_bytes=VMEM_LIMIT)


def _sigmoid(x):
    return 1.0 / (1.0 + jnp.exp(-x))


def _mm_kernel(*refs, n_a, glu):
    a_refs = refs[:n_a]
    w_ref = refs[n_a]
    g_ref = refs[n_a + 1] if glu else None
    o_ref = refs[-1]
    wb = w_ref[...].astype(BF16)
    acc = None
    off = 0
    for a_ref in a_refs:
        kp = a_ref.shape[1]
        part = jnp.dot(a_ref[...], wb[off:off + kp, :], preferred_element_type=F32)
        acc = part if acc is None else acc + part
        off += kp
    if glu:
        acc = g_ref[...] * _sigmoid(acc)
    o_ref[...] = acc.astype(o_ref.dtype)


def matmul(a_parts, w, layer, *, tm, tn, out_dtype, glu_gate=None):
    m = a_parts[0].shape[0]
    kdim, n = w.shape[-2:]
    assert m % tm == 0 and n % tn == 0 and sum(a.shape[1] for a in a_parts) == kdim
    in_specs = [pl.BlockSpec((tm, a.shape[1]), lambda i, j: (i, 0), pipeline_mode=pl.Buffered(1)) for a in a_parts]
    in_specs.append(pl.BlockSpec((None, kdim, tn), lambda i, j: (layer, 0, j)))
    args = list(a_parts) + [w]
    if glu_gate is not None:
        in_specs.append(pl.BlockSpec((tm, tn), lambda i, j: (i, j)))
        args.append(glu_gate)
    return pl.pallas_call(
        functools.partial(_mm_kernel, n_a=len(a_parts), glu=glu_gate is not None),
        out_shape=jax.ShapeDtypeStruct((m, n), out_dtype),
        grid=(m // tm, n // tn),
        in_specs=in_specs,
        out_specs=pl.BlockSpec((tm, tn), lambda i, j: (i, j)),
        compiler_params=_params(("parallel", "parallel")),
        name="matmul",
    )(*args)


def _layer_norm_rows(v, g, b):
    mu = jnp.mean(v, axis=-1, keepdims=True)
    d = v - mu
    var = jnp.mean(d * d, axis=-1, keepdims=True)
    return d * lax.rsqrt(var + LN_EPS) * g + b


HALF_D = D_MODEL // 2


def _ln_kernel(*refs, tail):
    if tail:
        xh_ref, xt_ref, y_ref, g_ref, b_ref, wr_ref, o_ref, op_ref, gate_ref, eid_ref = refs
        x = jnp.where(pl.program_id(0) < pl.num_programs(0) - 1, xh_ref[...], xt_ref[...])
    else:
        x_ref, y_ref, g_ref, b_ref, wr_ref, o_ref, op_ref, gate_ref, eid_ref = refs
        x = x_ref[...]
    out = _layer_norm_rows(ALPHA * x + y_ref[...], g_ref[...], b_ref[...])
    o_ref[...] = out
    gate_ref[...], eid_ref[...] = _route(out, wr_ref[...])
    bits = pltpu.bitcast(out.astype(BF16).astype(F32), jnp.int32)
    op_ref[...] = bits[:, HALF_D:] | lax.shift_right_logical(bits[:, :HALF_D], 16)


def ln_residual_route(x, y, g, b, w_route, layer, *, tm, x_tail=None):
    n, d = y.shape
    row = pl.BlockSpec((tm, d), lambda i: (i, 0))
    par = pl.BlockSpec((None, 1, d), lambda i: (layer, 0, 0))
    lanes = pl.BlockSpec((tm, ROUTE_COLS), lambda i: (i, 0))
    if x_tail is None:
        x_specs, x_args = [row], [x]
    else:
        assert x_tail.shape[0] == tm and x.shape[0] == n - tm
        head_tiles = x.shape[0] // tm
        x_specs = [pl.BlockSpec((tm, d), lambda i: (jnp.minimum(i, head_tiles - 1), 0)),
                   pl.BlockSpec((tm, d), lambda i: (0, 0))]
        x_args = [x, x_tail]
    return pl.pallas_call(
        functools.partial(_ln_kernel, tail=x_tail is not None),
        out_shape=(jax.ShapeDtypeStruct((n, d), F32), jax.ShapeDtypeStruct((n, HALF_D), jnp.int32),
                   jax.ShapeDtypeStruct((n, ROUTE_COLS), F32), jax.ShapeDtypeStruct((n, ROUTE_COLS), jnp.int32)),
        grid=(n // tm,),
        in_specs=x_specs + [row, par, par, pl.BlockSpec((None, d, ROUTE_COLS), lambda i: (layer, 0, 0))],
        out_specs=(row, pl.BlockSpec((tm, HALF_D), lambda i: (i, 0)), lanes, lanes),
        compiler_params=_params(("arbitrary",)),
        name="ln_residual_route",
    )(*x_args, y, g.reshape(DEPTH, 1, d), b.reshape(DEPTH, 1, d), w_route)


def _ln_moe_kernel(x_ref, y1_ref, y2_ref, gate_ref, g_ref, b_ref, *o_refs, split):
    gate = gate_ref[...]
    ffn = y1_ref[...] * gate[:, 0:1] + y2_ref[...] * gate[:, 1:2]
    out = _layer_norm_rows(ALPHA * x_ref[...] + ffn, g_ref[...], b_ref[...])
    if split:
        head_ref, tail_ref = o_refs
        last = pl.num_programs(0) - 1

        @pl.when(pl.program_id(0) < last)
        def _():
            head_ref[...] = out

        @pl.when(pl.program_id(0) == last)
        def _():
            tail_ref[...] = out
    else:
        o_ref, ob_ref = o_refs
        o_ref[...] = out
        ob_ref[...] = out.astype(BF16)


def ln_moe_combine(x, y1, y2, gates, g, b, layer, *, tm, split_tail=False):
    n, d = x.shape
    row = pl.BlockSpec((tm, d), lambda i: (i, 0))
    par = pl.BlockSpec((None, 1, d), lambda i: (layer, 0, 0))
    if split_tail:
        head_tiles = n // tm - 1
        out_shape = (jax.ShapeDtypeStruct((n - tm, d), F32), jax.ShapeDtypeStruct((tm, d), F32))
        out_specs = (pl.BlockSpec((tm, d), lambda i: (jnp.minimum(i, head_tiles - 1), 0)),
                     pl.BlockSpec((tm, d), lambda i: (0, 0)))
    else:
        out_shape = (jax.ShapeDtypeStruct((n, d), F32), jax.ShapeDtypeStruct((n, d), BF16))
        out_specs = (row, row)
    return pl.pallas_call(
        functools.partial(_ln_moe_kernel, split=split_tail),
        out_shape=out_shape,
        grid=(n // tm,),
        in_specs=[row, row, row, pl.BlockSpec((tm, LANES), lambda i: (i, 0)), par, par],
        out_specs=out_specs,
        compiler_params=_params(("arbitrary",)),
        name="ln_moe_combine",
    )(x, y1, y2, gates, g.reshape(DEPTH, 1, d), b.reshape(DEPTH, 1, d))


def _softmax_sink_pv(s, valid, sink_col, vband):
    s = jnp.where(valid, s, -jnp.inf)
    m = jnp.maximum(jnp.max(s, axis=1, keepdims=True), sink_col)
    p = jnp.exp(s - m)
    denom = jnp.sum(p, axis=1, keepdims=True) + jnp.exp(sink_col - m)
    o = jnp.dot(p.astype(BF16), vband, preferred_element_type=F32)
    return o / denom


def _attn_prompt_kernel(sink_ref, q_ref, kc_ref, kp_ref, vc_ref, vp_ref, o_ref):
    nb = pl.program_id(1)
    pair_w = 2 * HEAD_DIM
    r = lax.broadcasted_iota(jnp.int32, (WINDOW, WINDOW), 0)
    c = lax.broadcasted_iota(jnp.int32, (WINDOW, WINDOW), 1)
    from_prev = c > r
    prev_live = from_prev & (nb > 0)
    first_head = lax.broadcasted_iota(jnp.int32, (WINDOW, pair_w), 1) < HEAD_DIM
    nt = lambda a, b: lax.dot_general(a, b, (((1,), (1,)), ((), ())), preferred_element_type=F32)

    def both_heads(t):
        z = jnp.zeros_like(t)
        return jnp.concatenate([jnp.concatenate([t, z], axis=1), jnp.concatenate([z, t], axis=1)], axis=0).astype(BF16)

    for j in range(N_KV_HEADS):
        cols = slice(j * HEAD_DIM, (j + 1) * HEAD_DIM)
        k2p, k2c = both_heads(kp_ref[:, cols]), both_heads(kc_ref[:, cols])
        v2p, v2c = both_heads(vp_ref[:, cols]), both_heads(vc_ref[:, cols])
        for hp in range(j * Q_PER_KV // 2, (j + 1) * Q_PER_KV // 2):
            lanes = slice(hp * pair_w, (hp + 1) * pair_w)
            q2 = q_ref[:, lanes].astype(BF16)
            s_prev = nt(q2, k2p) * (HEAD_DIM ** -0.5)
            s_cur = nt(q2, k2c) * (HEAD_DIM ** -0.5)
            p_prev, p_cur, inv = [], [], []
            for side in range(2):
                half = slice(side * WINDOW, (side + 1) * WINDOW)
                s = jnp.where(prev_live, s_prev[:, half], jnp.where(from_prev, -jnp.inf, s_cur[:, half]))
                sink = sink_ref[2 * hp + side]
                m = jnp.maximum(jnp.max(s, axis=1, keepdims=True), sink)
                p = jnp.exp(s - m)
                inv.append(1.0 / (jnp.sum(p, axis=1, keepdims=True) + jnp.exp(sink - m)))
                p_prev.append(jnp.where(from_prev, p, 0.0))
                p_cur.append(jnp.where(from_prev, 0.0, p))
            o = (jnp.dot(jnp.concatenate(p_prev, axis=1).astype(BF16), v2p, preferred_element_type=F32)
                 + jnp.dot(jnp.concatenate(p_cur, axis=1).astype(BF16), v2c, preferred_element_type=F32))
            o_ref[:, lanes] = (o * jnp.where(first_head, inv[0], inv[1])).astype(o_ref.dtype)


def attn_prompt(h, sinks, batch, seq):
    nb = seq // WINDOW
    cur = lambda col: pl.BlockSpec((WINDOW, COL_BLK), lambda b, n, s: (b * nb + n, col))
    prev = lambda col: pl.BlockSpec((WINDOW, COL_BLK), lambda b, n, s: (b * nb + jnp.maximum(n - 1, 0), col))
    return pl.pallas_call(
        _attn_prompt_kernel,
        out_shape=jax.ShapeDtypeStruct((h.shape[0], D_ATT), BF16),
        grid_spec=pltpu.PrefetchScalarGridSpec(
            num_scalar_prefetch=1,
            grid=(batch, nb),
            in_specs=[pl.BlockSpec((WINDOW, D_ATT), lambda b, n, s: (b * nb + n, 0)),
                      cur(K_BLK), prev(K_BLK), cur(V_BLK), prev(V_BLK)],
            out_specs=pl.BlockSpec((WINDOW, D_ATT), lambda b, n, s: (b * nb + n, 0)),
        ),
        compiler_params=_params(("parallel", "arbitrary")),
        name="attn_prompt",
    )(sinks, h, h, h, h, h)


T_PAD = 8
KC_PAD = WINDOW + T_PAD


def _attn_sample_kernel(sink_ref, q_ref, kc_ref, vc_ref, o_ref, *, n_new):
    rows = Q_PER_KV * T_PAD
    t = lax.broadcasted_iota(jnp.int32, (rows, KC_PAD), 0) & (T_PAD - 1)
    c = lax.broadcasted_iota(jnp.int32, (rows, KC_PAD), 1)
    valid = (c > t) & (c <= t + WINDOW) & (c < WINDOW + n_new)
    for j in range(N_KV_HEADS):
        heads = range(j * Q_PER_KV, (j + 1) * Q_PER_KV)
        qg = jnp.concatenate([q_ref[:, h * HEAD_DIM:(h + 1) * HEAD_DIM] for h in heads], axis=0).astype(BF16)
        cols = slice(j * HEAD_DIM, (j + 1) * HEAD_DIM)
        kband = kc_ref[:, cols].astype(BF16)
        vband = vc_ref[:, cols].astype(BF16)
        s = lax.dot_general(qg, kband, (((1,), (1,)), ((), ())), preferred_element_type=F32) * (HEAD_DIM ** -0.5)
        sink_col = jnp.concatenate([jnp.full((T_PAD, 1), sink_ref[h], F32) for h in heads], axis=0)
        o = _softmax_sink_pv(s, valid, sink_col, vband)
        o = jnp.concatenate([o[i * T_PAD:(i + 1) * T_PAD] for i in range(Q_PER_KV)], axis=1)
        o_ref[:, j * Q_PER_KV * HEAD_DIM:(j + 1) * Q_PER_KV * HEAD_DIM] = o.astype(o_ref.dtype)


def attn_sample(q, kc, vc, sinks, n_new):
    db = q.shape[0]
    return pl.pallas_call(
        functools.partial(_attn_sample_kernel, n_new=n_new),
        out_shape=jax.ShapeDtypeStruct((db, T_PAD, D_ATT), BF16),
        grid_spec=pltpu.PrefetchScalarGridSpec(
            num_scalar_prefetch=1,
            grid=(db,),
            in_specs=[pl.BlockSpec((None, T_PAD, D_ATT), lambda b, s: (b, 0, 0)),
                      pl.BlockSpec((None, KC_PAD, D_KV), lambda b, s: (b, 0, 0)),
                      pl.BlockSpec((None, KC_PAD, D_KV), lambda b, s: (b, 0, 0))],
            out_specs=pl.BlockSpec((None, T_PAD, D_ATT), lambda b, s: (b, 0, 0)),
        ),
        compiler_params=_params(("parallel",)),
        name="attn_sample",
    )(sinks, q, kc, vc)


def _conv_prompt_kernel(b_ref, c_ref, u_ref, w_ref, o_ref, st_ref):
    cu = c_ref[...] * u_ref[...]
    t = lax.broadcasted_iota(jnp.int32, cu.shape, 0)
    sh1 = jnp.where(t >= 1, pltpu.roll(cu, 1, axis=0), 0.0)
    sh2 = jnp.where(t >= 2, pltpu.roll(cu, 2, axis=0), 0.0)
    w = w_ref[...]
    y = sh2 * w[0:1] + sh1 * w[1:2] + cu * w[2:3]
    o_ref[...] = (b_ref[...] * y).astype(o_ref.dtype)
    st_ref[...] = cu[cu.shape[0] - 8:]


def conv_prompt(h, conv_w, layer_i, batch, seq):
    nj = D_CONV // COL_BLK
    col = lambda base: pl.BlockSpec((seq, COL_BLK), lambda b, j: (b, base + j))
    return pl.pallas_call(
        _conv_prompt_kernel,
        out_shape=(jax.ShapeDtypeStruct((h.shape[0], D_CONV), BF16),
                   jax.ShapeDtypeStruct((batch * 8, D_CONV), F32)),
        grid=(batch, nj),
        in_specs=[col(B_BLK), col(C_BLK), col(U_BLK),
                  pl.BlockSpec((None, CONV_W, COL_BLK), lambda b, j: (layer_i, 0, j))],
        out_specs=(pl.BlockSpec((seq, COL_BLK), lambda b, j: (b, j)),
                   pl.BlockSpec((8, COL_BLK), lambda b, j: (b, j))),
        compiler_params=_params(("parallel", "parallel")),
        name="conv_prompt",
    )(h, h, h, conv_w)


def _conv_sample_kernel(b_ref, c_ref, u_ref, buf_ref, w_ref, o_ref, st_ref, *, n_new):
    w = w_ref[...]
    full = [buf_ref[i] for i in range(CONV_W - 1)] + [c_ref[t] * u_ref[t] for t in range(n_new)]
    for t in range(n_new):
        y = full[t] * w[0:1] + full[t + 1] * w[1:2] + full[t + 2] * w[2:3]
        o_ref[t] = (b_ref[t] * y).astype(o_ref.dtype)
    for i in range(CONV_W - 1):
        st_ref[i] = full[n_new + i]


def conv_sample(bg, cg, ug, buf, conv_w, layer_i):
    n_new, db, _ = bg.shape
    full = lambda a: pl.BlockSpec(a.shape, lambda g: (0,) * a.ndim)
    return pl.pallas_call(
        functools.partial(_conv_sample_kernel, n_new=n_new),
        out_shape=(jax.ShapeDtypeStruct((n_new, db, D_CONV), BF16),
                   jax.ShapeDtypeStruct((CONV_W - 1, db, D_CONV), F32)),
        grid=(1,),
        in_specs=[full(bg), full(cg), full(ug), full(buf),
                  pl.BlockSpec((None, CONV_W, D_CONV), lambda g: (layer_i, 0, 0))],
        out_specs=(pl.BlockSpec((n_new, db, D_CONV), lambda g: (0, 0, 0)),
                   pl.BlockSpec((CONV_W - 1, db, D_CONV), lambda g: (0, 0, 0))),
        compiler_params=_params(("arbitrary",)),
        name="conv_sample",
    )(bg, cg, ug, buf, conv_w)


PAIR = 2
PAIR_STATE = PAIR * SSM_STATE


def _s5_kernel(u_ref, w1_ref, w2t_ref, ap_ref, h0_ref, y_ref, hn_ref, *, nbatch, nchunk, width):
    rows = nbatch * nchunk
    z = jnp.dot(u_ref[...], w1_ref[...], preferred_element_type=F32)
    yi = z[:, :width]
    vr = z[:, width:width + PAIR_STATE]
    vi = z[:, width + PAIR_STATE:]
    h0r = h0_ref[0]
    h0i = h0_ref[1]
    if nchunk == 1:
        h0r_rows, h0i_rows = h0r, h0i
    else:
        h0r_rows = jnp.concatenate([jnp.broadcast_to(h0r[b:b + 1], (nchunk, PAIR_STATE)) for b in range(nbatch)], 0)
        h0i_rows = jnp.concatenate([jnp.broadcast_to(h0i[b:b + 1], (nchunk, PAIR_STATE)) for b in range(nbatch)], 0)
    krow = lax.broadcasted_iota(jnp.int32, (rows, PAIR_STATE), 0) & (nchunk - 1)
    ar = ap_ref[0, 0:1]
    ai = ap_ref[0, 1:2]
    first = krow == 0
    vr = vr + jnp.where(first, ar * h0r_rows - ai * h0i_rows, 0.0)
    vi = vi + jnp.where(first, ar * h0i_rows + ai * h0r_rows, 0.0)
    step = 0
    while (1 << step) < nchunk:
        d = 1 << step
        ar = ap_ref[step, 0:1]
        ai = ap_ref[step, 1:2]
        keep = krow >= d
        sr = jnp.where(keep, pltpu.roll(vr, d, axis=0), 0.0)
        si = jnp.where(keep, pltpu.roll(vi, d, axis=0), 0.0)
        vr, vi = vr + (ar * sr - ai * si), vi + (ar * si + ai * sr)
        step += 1
    if nchunk == 1:
        hr, hi = h0r_rows, h0i_rows
    else:
        hr = jnp.where(first, h0r_rows, pltpu.roll(vr, 1, axis=0))
        hi = jnp.where(first, h0i_rows, pltpu.roll(vi, 1, axis=0))
    hcat = jnp.concatenate([hr, hi], axis=1).astype(BF16)
    y_ref[...] = yi + lax.dot_general(hcat, w2t_ref[...], (((1,), (1,)), ((), ())), preferred_element_type=F32)
    if nchunk == 1:
        hn_ref[0] = vr
        hn_ref[1] = vi
    else:
        last = [b * nchunk + nchunk - 1 for b in range(nbatch)]
        hn_ref[0] = jnp.concatenate([vr[i:i + 1] for i in last], axis=0)
        hn_ref[1] = jnp.concatenate([vi[i:i + 1] for i in last], axis=0)


def s5_scan(u, w1, w2t, ap, h0, nbatch, nchunk):
    g2, rows, width = u.shape
    nstep = ap.shape[1]
    return pl.pallas_call(
        functools.partial(_s5_kernel, nbatch=nbatch, nchunk=nchunk, width=width),
        out_shape=(jax.ShapeDtypeStruct((g2, rows, width), F32),
                   jax.ShapeDtypeStruct((g2, 2, nbatch, PAIR_STATE), F32)),
        grid=(g2,),
        in_specs=[pl.BlockSpec((None, rows, width), lambda g: (g, 0, 0)),
                  pl.BlockSpec((None, width, width + 2 * PAIR_STATE), lambda g: (g, 0, 0)),
                  pl.BlockSpec((None, width, 2 * PAIR_STATE), lambda g: (g, 0, 0)),
                  pl.BlockSpec((None, nstep, 2, PAIR_STATE), lambda g: (g, 0, 0, 0)),
                  pl.BlockSpec((None, 2, nbatch, PAIR_STATE), lambda g: (g, 0, 0, 0))],
        out_specs=(pl.BlockSpec((None, rows, width), lambda g: (g, 0, 0)),
                   pl.BlockSpec((None, 2, nbatch, PAIR_STATE), lambda g: (g, 0, 0, 0))),
        compiler_params=_params(("parallel",)),
        name="s5_scan",
    )(u, w1, w2t, ap, h0)


def s5_apply(x, h0_re, h0_im, ssm, chunk):
    bsz, t, _ = x.shape
    nchunk = t // chunk
    g2 = N_SSM_GROUPS // PAIR
    width = PAIR * chunk * SSM_GROUP
    w1, w2t, ap, _ = s5_operators(*ssm, chunk, nchunk)
    u = x.astype(BF16).reshape(bsz, nchunk, chunk, g2, PAIR, SSM_GROUP)
    u = u.transpose(3, 0, 1, 4, 2, 5).reshape(g2, bsz * nchunk, width)
    pair_state = lambda s: s.reshape(bsz, g2, PAIR_STATE).transpose(1, 0, 2)
    h0 = jnp.stack([pair_state(h0_re), pair_state(h0_im)], axis=1)
    y, hn = s5_scan(u, w1, w2t, ap, h0, bsz, nchunk)
    y = y.reshape(g2, bsz, nchunk, PAIR, chunk, SSM_GROUP).transpose(1, 2, 4, 0, 3, 5).reshape(bsz, t, D_MODEL)
    unpair = lambda s: s.transpose(1, 0, 2).reshape(bsz, N_SSM_GROUPS, SSM_STATE)
    return y, unpair(hn[:, 0]), unpair(hn[:, 1])


def _gelu_tanh(v):
    inner = math.sqrt(2.0 / math.pi) * (v + 0.044715 * (v * v * v))
    return 0.5 * v * (1.0 + jnp.tanh(inner))


GROUPS_PER_TILE = LANES // SSM_GROUP
PAIRS_PER_TILE = GROUPS_PER_TILE // PAIR
S5_CHUNK = 16


def _block_transpose8(arrs):
    arrs = list(arrs)
    lane = lax.broadcasted_iota(jnp.int32, arrs[0].shape, 1)
    for d in (4, 2, 1):
        clear = (lane & (d * SSM_GROUP)) == 0
        nxt = list(arrs)
        for i in range(GROUPS_PER_TILE):
            if i & d:
                continue
            lo, hi = arrs[i], arrs[i + d]
            nxt[i] = jnp.where(clear, lo, pltpu.roll(hi, d * SSM_GROUP, axis=1))
            nxt[i + d] = jnp.where(clear, pltpu.roll(lo, LANES - d * SSM_GROUP, axis=1), hi)
        arrs = nxt
    return arrs


def _shift_lanes(a, k):
    if k == 0:
        return a
    return jnp.concatenate([jnp.zeros((a.shape[0], k), a.dtype), a[:, :a.shape[1] - k]], axis=1)


def _dot_nt_3pass(a, b):
    ah, al = _split_bf16(a)
    bh, bl = _split_bf16(b)
    nt = lambda u, v: lax.dot_general(u, v, (((1,), (1,)), ((), ())), preferred_element_type=F32)
    return nt(ah, bh) + (nt(ah, bl) + nt(al, bh))


def _s5_ops_kernel(cr_ref, ci_ref, btr_ref, bti_ref, pwr_ref, pwi_ref, w1_ref, w2t_ref, *, chunk):
    nl = chunk
    wg = nl * SSM_GROUP
    rep = lambda a, lo: jnp.concatenate(
        [jnp.broadcast_to(a[d:d + 1], (SSM_GROUP, SSM_STATE)) for d in range(lo, lo + nl)], axis=0)
    tile = lambda a: jnp.concatenate([a] * nl, axis=0)
    for q in range(PAIRS_PER_TILE):
        _s5_pair_operators(q, cr_ref, ci_ref, btr_ref, bti_ref, pwr_ref, pwi_ref, w1_ref, w2t_ref, nl, wg, rep, tile)


def _s5_pair_operators(q, cr_ref, ci_ref, btr_ref, bti_ref, pwr_ref, pwi_ref, w1_ref, w2t_ref, nl, wg, rep, tile):
    ky, ksr, ksi, khr, khi = [], [], [], [], []
    for g in range(PAIR * q, PAIR * q + PAIR):
        cr, ci = tile(cr_ref[g]), tile(ci_ref[g])
        btr, bti = btr_ref[g], bti_ref[g]
        pwr, pwi = pwr_ref[g], pwi_ref[g]
        p0r, p0i = rep(pwr, 0), rep(pwi, 0)
        m0r = cr * p0r - ci * p0i
        m0i = cr * p0i + ci * p0r
        r = _dot_nt_3pass(btr, m0r) - _dot_nt_3pass(bti, m0i)
        ky.append(jnp.concatenate([_shift_lanes(r, s * SSM_GROUP) for s in range(nl)], axis=0))
        sr, si = [], []
        for s in range(nl):
            pr = pwr[nl - 1 - s:nl - s]
            pi = pwi[nl - 1 - s:nl - s]
            sr.append(btr * pr - bti * pi)
            si.append(btr * pi + bti * pr)
        ksr.append(jnp.concatenate(sr, axis=0))
        ksi.append(jnp.concatenate(si, axis=0))
        p1r, p1i = rep(pwr, 1), rep(pwi, 1)
        khr.append(cr * p1r - ci * p1i)
        khi.append(-(cr * p1i + ci * p1r))
    zy = jnp.zeros((wg, wg), F32)
    zs = jnp.zeros((wg, SSM_STATE), F32)
    top = jnp.concatenate([ky[0], zy, ksr[0], zs, ksi[0], zs], axis=1)
    bot = jnp.concatenate([zy, ky[1], zs, ksr[1], zs, ksi[1]], axis=1)
    w1_ref[q] = jnp.concatenate([top, bot], axis=0).astype(BF16)
    w2t_ref[q] = jnp.concatenate([jnp.concatenate([khr[0], zs, khi[0], zs], axis=1),
                                  jnp.concatenate([zs, khr[1], zs, khi[1]], axis=1)], axis=0).astype(BF16)


def s5_operators(lam_re, lam_im, log_dt, b_re, b_im, c_re, c_im, chunk, nchunk):
    g, p = lam_re.shape
    g2 = g // PAIR
    wg = chunk * SSM_GROUP
    ldt = lax.complex(lam_re, lam_im) * jnp.exp(log_dt)[:, None]
    a_bar = jnp.exp(ldt)
    b_bar = ((a_bar - 1.0) / lax.complex(lam_re, lam_im))[..., None] * lax.complex(b_re, b_im)
    pw = jnp.exp(ldt[:, None, :] * jnp.arange(chunk + 1, dtype=F32)[None, :, None])
    bt = b_bar.transpose(0, 2, 1)
    grp = lambda a: pl.BlockSpec((GROUPS_PER_TILE,) + a.shape[1:], lambda j: (j, 0, 0))
    args = (c_re, c_im, bt.real, bt.imag, pw.real, pw.imag)
    w1, w2t = pl.pallas_call(
        functools.partial(_s5_ops_kernel, chunk=chunk),
        out_shape=(jax.ShapeDtypeStruct((g2, PAIR * wg, PAIR * wg + 2 * PAIR_STATE), BF16),
                   jax.ShapeDtypeStruct((g2, PAIR * wg, 2 * PAIR_STATE), BF16)),
        grid=(g2 // PAIRS_PER_TILE,),
        in_specs=[grp(a) for a in args],
        out_specs=(pl.BlockSpec((PAIRS_PER_TILE, PAIR * wg, PAIR * wg + 2 * PAIR_STATE), lambda j: (j, 0, 0)),
                   pl.BlockSpec((PAIRS_PER_TILE, PAIR * wg, 2 * PAIR_STATE), lambda j: (j, 0, 0))),
        compiler_params=_params(("parallel",)),
        name="s5_operators",
    )(*args)
    nstep = max(1, (nchunk - 1).bit_length())
    mult = (chunk * (2 ** jnp.arange(nstep))).astype(F32)
    ap = jnp.exp(ldt[:, None, :] * mult[None, :, None])
    pair = lambda t: t.reshape(g2, PAIR, -1, p).transpose(0, 2, 1, 3).reshape(g2, -1, PAIR_STATE)
    ap8 = jnp.exp(ldt[:, None, :] * (chunk * jnp.arange(1, SUBLANES + 1, dtype=F32))[None, :, None])
    return (w1, w2t, jnp.stack([pair(ap.real), pair(ap.imag)], axis=2),
            jnp.stack([pair(ap8.real), pair(ap8.imag)], axis=1))


SUBLANES = 8


def _scan_chunk_rows(vr, vi, ap_ref, ap8_ref, q, nbatch, nchunk):
    row = lax.broadcasted_iota(jnp.int32, (SUBLANES, PAIR_STATE), 0)
    p8r, p8i = ap8_ref[q, 0], ap8_ref[q, 1]
    tiles_r, tiles_i = [], []
    for b in range(nbatch):
        cr = ci = None
        for t in range(nchunk // SUBLANES):
            r0 = b * nchunk + t * SUBLANES
            xr, xi = vr[r0:r0 + SUBLANES], vi[r0:r0 + SUBLANES]
            for step in range(3):
                d = 1 << step
                ar, ai = ap_ref[q, step, 0:1], ap_ref[q, step, 1:2]
                keep = row >= d
                sr = jnp.where(keep, pltpu.roll(xr, d, axis=0), 0.0)
                si = jnp.where(keep, pltpu.roll(xi, d, axis=0), 0.0)
                xr, xi = xr + (ar * sr - ai * si), xi + (ar * si + ai * sr)
            if cr is not None:
                xr, xi = xr + (p8r * cr - p8i * ci), xi + (p8r * ci + p8i * cr)
            cr = jnp.broadcast_to(xr[SUBLANES - 1:], (SUBLANES, PAIR_STATE))
            ci = jnp.broadcast_to(xi[SUBLANES - 1:], (SUBLANES, PAIR_STATE))
            tiles_r.append(xr)
            tiles_i.append(xi)
    return jnp.concatenate(tiles_r, axis=0), jnp.concatenate(tiles_i, axis=0)


def _s5_prompt_kernel(x_ref, w1_ref, w2t_ref, ap_ref, ap8_ref, d_ref, g_ref, gb_ref, hn_ref, *, nbatch, nchunk):
    rows = nbatch * nchunk
    half = S5_CHUNK // 2
    xs = [x_ref[pl.ds(s, rows, stride=S5_CHUNK), :] for s in range(S5_CHUNK)]
    v0 = _block_transpose8(xs[:half])
    v1 = _block_transpose8(xs[half:])
    krow = lax.broadcasted_iota(jnp.int32, (rows, PAIR_STATE), 0) & (nchunk - 1)
    width = PAIR * S5_CHUNK * SSM_GROUP
    y0 = [None] * GROUPS_PER_TILE
    y1 = [None] * GROUPS_PER_TILE
    for q in range(PAIRS_PER_TILE):
        ga, gb = PAIR * q, PAIR * q + 1
        u = jnp.concatenate([v0[ga], v1[ga], v0[gb], v1[gb]], axis=1).astype(BF16)
        z = jnp.dot(u, w1_ref[q], preferred_element_type=F32)
        vr = z[:, width:width + PAIR_STATE]
        vi = z[:, width + PAIR_STATE:]
        vr, vi = _scan_chunk_rows(vr, vi, ap_ref, ap8_ref, q, nbatch, nchunk)
        first = krow == 0
        hr = jnp.where(first, 0.0, pltpu.roll(vr, 1, axis=0))
        hi = jnp.where(first, 0.0, pltpu.roll(vi, 1, axis=0))
        hcat = jnp.concatenate([hr, hi], axis=1).astype(BF16)
        y = z[:, :width] + lax.dot_general(hcat, w2t_ref[q], (((1,), (1,)), ((), ())), preferred_element_type=F32)
        y0[ga], y1[ga] = y[:, 0:LANES], y[:, LANES:2 * LANES]
        y0[gb], y1[gb] = y[:, 2 * LANES:3 * LANES], y[:, 3 * LANES:]
        last = [b * nchunk + nchunk - 1 for b in range(nbatch)]
        hn_ref[q, 0] = jnp.concatenate([vr[i:i + 1] for i in last], axis=0)
        hn_ref[q, 1] = jnp.concatenate([vi[i:i + 1] for i in last], axis=0)
    ys = _block_transpose8(y0) + _block_transpose8(y1)
    dskip = d_ref[...]
    for s in range(S5_CHUNK):
        g_ref[pl.ds(s, rows, stride=S5_CHUNK), :] = _gelu_tanh(ys[s] + dskip * xs[s])
    gb_ref[...] = g_ref[...].astype(BF16)


def s5_prompt(x, w1, w2t, ap, ap8, d, layer_i, nbatch, seq):
    n, dm = x.shape
    nchunk = seq // S5_CHUNK
    n_p = nbatch * seq
    nstep = ap.shape[1]
    g2 = w1.shape[0]
    width = PAIR * S5_CHUNK * SSM_GROUP
    tile = pl.BlockSpec((n_p, LANES), lambda j: (0, j))
    return pl.pallas_call(
        functools.partial(_s5_prompt_kernel, nbatch=nbatch, nchunk=nchunk),
        out_shape=(jax.ShapeDtypeStruct((n, dm), F32), jax.ShapeDtypeStruct((n, dm), BF16),
                   jax.ShapeDtypeStruct((g2, 2, nbatch, PAIR_STATE), F32)),
        grid=(dm // LANES,),
        in_specs=[tile,
                  pl.BlockSpec((PAIRS_PER_TILE, width, width + 2 * PAIR_STATE), lambda j: (j, 0, 0)),
                  pl.BlockSpec((PAIRS_PER_TILE, width, 2 * PAIR_STATE), lambda j: (j, 0, 0)),
                  pl.BlockSpec((PAIRS_PER_TILE, nstep, 2, PAIR_STATE), lambda j: (j, 0, 0, 0)),
                  pl.BlockSpec((PAIRS_PER_TILE, 2, SUBLANES, PAIR_STATE), lambda j: (j, 0, 0, 0)),
                  pl.BlockSpec((None, 1, LANES), lambda j: (layer_i, 0, j))],
        out_specs=(tile, tile,
                   pl.BlockSpec((PAIRS_PER_TILE, 2, nbatch, PAIR_STATE), lambda j: (j, 0, 0, 0))),
        compiler_params=_params(("parallel",)),
        name="s5_prompt",
    )(x, w1, w2t, ap, ap8, d.reshape(d.shape[0], 1, dm))


def _gelu_skip_kernel(y_ref, x_ref, d_ref, o_ref):
    o_ref[...] = _gelu_tanh(y_ref[...] + d_ref[...] * x_ref[...])


def gelu_skip(y, x, d, layer_i, *, tm):
    n, dm = x.shape
    row = pl.BlockSpec((tm, dm), lambda i: (i, 0))
    return pl.pallas_call(
        _gelu_skip_kernel,
        out_shape=jax.ShapeDtypeStruct((n, dm), F32),
        grid=(n // tm,),
        in_specs=[row, row, pl.BlockSpec((None, 1, dm), lambda i: (layer_i, 0, 0))],
        out_specs=row,
        compiler_params=_params(("parallel",)),
        name="gelu_skip",
    )(y, x, d.reshape(d.shape[0], 1, dm))


ROUTE_COLS = LANES


def _split_bf16(v):
    hi = v.astype(BF16)
    lo = (v - hi.astype(F32)).astype(BF16)
    return hi, lo


def _route(x, w):
    xh, xl = _split_bf16(x)
    wh, wl = _split_bf16(w)
    dot = lambda a, b: jnp.dot(a, b, preferred_element_type=F32)
    logits = dot(xh, wh) + (dot(xh, wl) + dot(xl, wh))
    lane = lax.broadcasted_iota(jnp.int32, logits.shape, 1).astype(F32)
    neg = -jnp.inf
    big = float(ROUTE_COLS)
    lg = jnp.where(lane < N_EXPERT_GROUPS, logits, neg)
    m = jnp.max(lg, axis=1, keepdims=True)
    grp = jnp.min(jnp.where(lg == m, lane, big), axis=1, keepdims=True)
    gate_g = 1.0 / jnp.sum(jnp.exp(lg - m), axis=1, keepdims=True)
    lo = N_EXPERT_GROUPS + grp * EXPERTS_PER_GROUP
    le = jnp.where((lane >= lo) & (lane < lo + EXPERTS_PER_GROUP), logits, neg)
    t1 = jnp.max(le, axis=1, keepdims=True)
    i1 = jnp.min(jnp.where(le == t1, lane, big), axis=1, keepdims=True)
    le2 = jnp.where(lane == i1, neg, le)
    t2 = jnp.max(le2, axis=1, keepdims=True)
    i2 = jnp.min(jnp.where(le2 == t2, lane, big), axis=1, keepdims=True)
    e = jnp.exp(t2 - t1)
    p1 = 1.0 / (1.0 + e)
    p2 = e / (1.0 + e)
    gates = jnp.where(lane == 0.0, gate_g * p1, jnp.where(lane == 1.0, gate_g * p2, 0.0))
    eid = jnp.where(lane == 0.0, i1, jnp.where(lane == 1.0, i2, float(N_EXPERT_GROUPS))) - N_EXPERT_GROUPS
    return gates, eid.astype(jnp.int32)


MOE_SUB = 128
MOE_NSUB = 6
MOE_KC = 1024
MOE_SK = 256
MOE_HALF = D_EXPERT // 2
MOE_TM = MOE_SUB * MOE_NSUB
MOE_TF = 256
MOE_TN = 512
MOE_NK = D_MODEL // MOE_KC
MOE_NN = D_MODEL // MOE_TN


def _moe_kernel(be_ref, ns_ref, bi_ref, x_ref, wg_ref, wu_ref, wd_ref, o_ref, xb_ref, accg_ref, accu_ref, hid_ref):
    blk = pl.program_id(0)
    s = pl.program_id(1)
    nsub = ns_ref[blk]

    @pl.when((nsub > 0) & (s == 0))
    def _():
        words = x_ref[...]
        lo = pltpu.bitcast(words << 16, F32).astype(BF16)
        hi = pltpu.bitcast(words & -65536, F32).astype(BF16)
        for kc in range(MOE_NK):
            src = lo if kc < MOE_NK // 2 else hi
            off = (kc % (MOE_NK // 2)) * MOE_KC
            xb_ref[kc] = src[:, off:off + MOE_KC]

    for n in range(1, MOE_NSUB + 1):
        rows = n * MOE_SUB

        @pl.when((nsub == n) & (s < MOE_NK))
        def _():
            for half in range(D_EXPERT // MOE_HALF):
                cols = slice(half * MOE_HALF, (half + 1) * MOE_HALF)
                gate = up = None
                for kk in range(MOE_KC // MOE_SK):
                    kr = slice(kk * MOE_SK, (kk + 1) * MOE_SK)
                    xr = xb_ref[s, 0:rows, kr]
                    gp = jnp.dot(xr, wg_ref[kr, cols].astype(BF16), preferred_element_type=F32)
                    upp = jnp.dot(xr, wu_ref[kr, cols].astype(BF16), preferred_element_type=F32)
                    gate = gp if gate is None else gate + gp
                    up = upp if up is None else up + upp

                @pl.when(s == 0)
                def _():
                    accg_ref[0:rows, cols] = gate
                    accu_ref[0:rows, cols] = up

                @pl.when((s > 0) & (s < MOE_NK - 1))
                def _():
                    accg_ref[0:rows, cols] += gate
                    accu_ref[0:rows, cols] += up

                @pl.when(s == MOE_NK - 1)
                def _():
                    g = accg_ref[0:rows, cols] + gate
                    u = accu_ref[0:rows, cols] + up
                    hid_ref[0:rows, cols] = ((g * _sigmoid(g)) * u).astype(BF16)

        @pl.when((nsub == n) & (s >= MOE_NK))
        def _():
            out = None
            for f in range(D_EXPERT // MOE_TF):
                fs = slice(f * MOE_TF, (f + 1) * MOE_TF)
                part = jnp.dot(hid_ref[0:rows, fs], wd_ref[fs, :].astype(BF16), preferred_element_type=F32)
                out = part if out is None else out + part
            o_ref[0:rows, :] = out
            if rows < MOE_TM:
                o_ref[rows:, :] = jnp.zeros((MOE_TM - rows, MOE_TN), F32)


def moe_experts(xs, blk_e, blk_nsub, blk_idx, w_gate, w_up, w_down, layer):
    p = xs.shape[0]
    d = D_MODEL
    nblk = p // MOE_TM
    up_idx = lambda s, ns, b: jnp.where(ns[b] > 0, jnp.minimum(s, MOE_NK - 1), MOE_NK - 1)
    dn_idx = lambda s, ns, b: jnp.where(ns[b] > 0, jnp.maximum(s - MOE_NK, 0), MOE_NN - 1)
    x_blk = lambda s, b: jnp.minimum(b + (s >= MOE_NK).astype(jnp.int32), nblk - 1)
    return pl.pallas_call(
        _moe_kernel,
        out_shape=jax.ShapeDtypeStruct((p, d), F32),
        grid_spec=pltpu.PrefetchScalarGridSpec(
            num_scalar_prefetch=3,
            grid=(nblk, MOE_NK + MOE_NN),
            in_specs=[
                pl.BlockSpec((MOE_TM, HALF_D), lambda b, s, be, ns, bi: (bi[x_blk(s, b)], 0)),
                pl.BlockSpec((None, None, MOE_KC, D_EXPERT),
                             lambda b, s, be, ns, bi: (layer, be[b], up_idx(s, ns, b), 0)),
                pl.BlockSpec((None, None, MOE_KC, D_EXPERT),
                             lambda b, s, be, ns, bi: (layer, be[b], up_idx(s, ns, b), 0)),
                pl.BlockSpec((None, None, D_EXPERT, MOE_TN),
                             lambda b, s, be, ns, bi: (layer, be[b], 0, dn_idx(s, ns, b)))],
            out_specs=pl.BlockSpec((MOE_TM, MOE_TN), lambda b, s, be, ns, bi: (bi[b], dn_idx(s, ns, b))),
            scratch_shapes=[pltpu.VMEM((MOE_NK, MOE_TM, MOE_KC), BF16), pltpu.VMEM((MOE_TM, D_EXPERT), F32),
                            pltpu.VMEM((MOE_TM, D_EXPERT), F32), pltpu.VMEM((MOE_TM, D_EXPERT), BF16)],
        ),
        compiler_params=_params(("arbitrary", "arbitrary")),
        name="moe_experts",
    )(blk_e, blk_nsub, blk_idx, xs, w_gate, w_up, w_down)


def moe_dispatch(eids, n):
    nslot = n * 2
    eid = eids.reshape(-1)
    experts = jnp.arange(N_EXPERTS, dtype=jnp.int32)
    onehot = (eid[:, None] == experts[None, :]).astype(jnp.int32)
    seen = jnp.cumsum(onehot, axis=0)
    rank = jnp.sum(seen * onehot, axis=1) - 1
    counts = seen[-1]
    nblk = (counts + MOE_TM - 1) // MOE_TM
    bend = jnp.cumsum(nblk)
    bstart = bend - nblk
    dest = jnp.sum(onehot * bstart[None, :], axis=1) * MOE_TM + rank
    n_blocks = nslot // MOE_TM + N_EXPERTS
    row_tok = jnp.zeros((n_blocks * MOE_TM,), jnp.int32).at[dest].set(jnp.arange(nslot, dtype=jnp.int32) // 2)
    pos = dest.reshape(n, 2)
    blk = jnp.arange(n_blocks, dtype=jnp.int32)
    n_used = bend[-1]
    blk_idx = jnp.minimum(blk, n_used - 1)
    blk_e = jnp.sum((blk_idx[:, None] >= bend[None, :]).astype(jnp.int32), axis=1)
    rows_left = counts[blk_e] - (blk_idx - bstart[blk_e]) * MOE_TM
    nsub = (jnp.clip(rows_left, 0, MOE_TM) + MOE_SUB - 1) // MOE_SUB
    blk_nsub = jnp.where(blk < n_used, nsub, 0).astype(jnp.int32)
    return pos, row_tok, blk_e.astype(jnp.int32), blk_nsub, blk_idx


def hier_moe_ln(x, x_packed, gates, eids, layer, w_gate, w_up, w_down, ln_g, ln_b, *, tm, split_tail=False):
    n = x.shape[0]
    pos, row_tok, blk_e, blk_nsub, blk_idx = moe_dispatch(eids[:, :2], n)
    xs = x_packed[row_tok]
    ys = moe_experts(xs, blk_e, blk_nsub, blk_idx, w_gate, w_up, w_down, layer)
    return ln_moe_combine(x, ys[pos[:, 0]], ys[pos[:, 1]], gates, ln_g, ln_b, layer, tm=tm, split_tail=split_tail)


ROW_TILE = 320
MM_TM = 2080
MM_TN = 512


def _rows_after(full, tail, start):
    return lax.dynamic_update_slice(full, tail.astype(full.dtype), (start, 0))


def kernel(x_prompt, x_sample, cache_k, cache_v, state_conv, state_ssm_re, state_ssm_im, w_in, w_out, attn_sinks, conv_w, ssm_lambda_re, ssm_lambda_im, ssm_log_dt, ssm_b_re, ssm_b_im, ssm_c_re, ssm_c_im, ssm_d, w_glu, w_o, ln1_g, ln1_b, ln2_g, ln2_b, w_router_group, w_router_expert, w_gate, w_up, w_down):
    batch, seq, _ = x_prompt.shape
    db, n_new, _ = x_sample.shape
    n_p = batch * seq
    n_s = db * n_new
    n = n_p + n_s
    assert n % ROW_TILE == 0 and n % MM_TM == 0 and n_new <= T_PAD and n_p % n_s == 0 and n_s % SUBLANES == 0

    w_route = jnp.concatenate(
        [w_router_group, w_router_expert,
         jnp.zeros((DEPTH, D_MODEL, ROUTE_COLS - N_EXPERT_GROUPS - N_EXPERTS), F32)], axis=-1)

    x_head, x_tail = x_prompt.reshape(n_p, D_MODEL), x_sample.reshape(n_s, D_MODEL)
    xb = jnp.concatenate([x_head.astype(BF16), x_tail.astype(BF16)], axis=0)
    k_p, v_p, conv_p, re_p, im_p = [], [], [], [], []
    k_s, v_s, conv_s, re_s, im_s = [], [], [], [], []
    for layer in range(DEPTH):
        i = layer // 2
        if layer % 2 == 0:
            h = matmul([xb], w_in, i, tm=MM_TM, tn=MM_TN, out_dtype=F32)
            attn = attn_prompt(h, attn_sinks[i], batch, seq)
            gconv, cu_tail = conv_prompt(h, conv_w, i, batch, seq)
            tails = [h[b * seq + seq - WINDOW:(b + 1) * seq, D_ATT:D_ATT + 2 * D_KV] for b in range(batch)]
            kv_tail = jnp.stack(tails)
            k_p.append(kv_tail[:, :, :D_KV].reshape(batch, WINDOW, N_KV_HEADS, HEAD_DIM))
            v_p.append(kv_tail[:, :, D_KV:].reshape(batch, WINDOW, N_KV_HEADS, HEAD_DIM))
            conv_p.append(cu_tail.reshape(batch, 8, D_CONV)[:, 8 - (CONV_W - 1):])
            hs = h[n_p:].reshape(db, n_new, D_IN_EVEN)
            k_new = hs[:, :, D_ATT:D_ATT + D_KV]
            v_new = hs[:, :, D_ATT + D_KV:D_ATT + 2 * D_KV]
            pad_rows = jnp.zeros((db, KC_PAD - WINDOW - n_new, D_KV), F32)
            kc = jnp.concatenate([cache_k[i].reshape(db, WINDOW, D_KV), k_new, pad_rows], axis=1)
            vc = jnp.concatenate([cache_v[i].reshape(db, WINDOW, D_KV), v_new, pad_rows], axis=1)
            q_s = jnp.pad(hs[:, :, :D_ATT], ((0, 0), (0, T_PAD - n_new), (0, 0)))
            attn_s = attn_sample(q_s, kc, vc, attn_sinks[i], n_new)[:, :n_new].reshape(n_s, D_ATT)
            k_s.append(kc[:, n_new:n_new + WINDOW].reshape(db, WINDOW, N_KV_HEADS, HEAD_DIM))
            v_s.append(vc[:, n_new:n_new + WINDOW].reshape(db, WINDOW, N_KV_HEADS, HEAD_DIM))
            off = D_ATT + 2 * D_KV
            tmaj = lambda a: a.transpose(1, 0, 2)
            gconv_s, st_s = conv_sample(tmaj(hs[:, :, off:off + D_CONV]),
                                        tmaj(hs[:, :, off + D_CONV:off + 2 * D_CONV]),
                                        tmaj(hs[:, :, off + 2 * D_CONV:]),
                                        tmaj(state_conv[i]), conv_w, i)
            conv_s.append(tmaj(st_s))
            attn = _rows_after(attn, attn_s, n_p)
            gconv = _rows_after(gconv, tmaj(gconv_s).reshape(n_s, D_CONV), n_p)
            mixed = matmul([attn, gconv], w_out, i, tm=MM_TM, tn=MM_TN, out_dtype=F32)
        else:
            ssm = (ssm_lambda_re[i], ssm_lambda_im[i], ssm_log_dt[i], ssm_b_re[i], ssm_b_im[i],
                   ssm_c_re[i], ssm_c_im[i])
            w1, w2t, ap, ap8 = s5_operators(*ssm, S5_CHUNK, seq // S5_CHUNK)
            g, gb, hn = s5_prompt(x, w1, w2t, ap, ap8, ssm_d, i, batch, seq)
            unpair = lambda s: s.transpose(1, 0, 2).reshape(batch, N_SSM_GROUPS, SSM_STATE)
            re_p.append(unpair(hn[:, 0]))
            im_p.append(unpair(hn[:, 1]))
            x_s = x[n_p:]
            y_s, nre_s, nim_s = s5_apply(x_s.reshape(db, n_new, D_MODEL), state_ssm_re[i], state_ssm_im[i],
                                         ssm, n_new)
            re_s.append(nre_s)
            im_s.append(nim_s)
            g_s = gelu_skip(y_s.reshape(n_s, D_MODEL), x_s, ssm_d, i, tm=n_s)
            g = _rows_after(g, g_s, n_p)
            gb = _rows_after(gb, g_s, n_p)
            z = matmul([gb], w_glu, i, tm=MM_TM, tn=MM_TN, out_dtype=BF16, glu_gate=g)
            mixed = matmul([z], w_o, i, tm=MM_TM, tn=MM_TN, out_dtype=F32)
        if layer == 0:
            x, x_packed, gates, eids = ln_residual_route(x_head, mixed, ln1_g, ln1_b, w_route, layer, tm=n_s,
                                                         x_tail=x_tail)
        else:
            x, x_packed, gates, eids = ln_residual_route(x, mixed, ln1_g, ln1_b, w_route, layer, tm=ROW_TILE)
        last = layer == DEPTH - 1
        x, xb = hier_moe_ln(x, x_packed, gates, eids, layer, w_gate, w_up, w_down, ln2_g, ln2_b,
                            tm=n_s if last else ROW_TILE, split_tail=last)

    y_prompt = x.reshape(batch, seq, D_MODEL)
    y_sample = xb.reshape(db, n_new, D_MODEL)
    st = jnp.stack
    return (y_prompt, y_sample, st(k_p), st(v_p), st(conv_p), st(re_p), st(im_p),
            st(k_s), st(v_s), st(conv_s), st(re_s), st(im_s))
```

```python
import functools
import math

import jax
import jax.numpy as jnp
from jax import lax
from jax.experimental import pallas as pl
from jax.experimental.pallas import tpu as pltpu

F32 = jnp.float32
BF16 = jnp.bfloat16

D_MODEL = 4096
DEPTH = 4
N_HEADS = 32
N_KV_HEADS = 4
Q_PER_KV = N_HEADS // N_KV_HEADS
HEAD_DIM = 64
WINDOW = 128
D_ATT = N_HEADS * HEAD_DIM
D_KV = N_KV_HEADS * HEAD_DIM
D_CONV = D_MODEL // 2
CONV_W = 3
D_IN_EVEN = D_ATT + 2 * D_KV + 3 * D_CONV
SSM_GROUP = 16
N_SSM_GROUPS = D_MODEL // SSM_GROUP
SSM_STATE = 64
N_EXPERT_GROUPS = 4
EXPERTS_PER_GROUP = 8
N_EXPERTS = N_EXPERT_GROUPS * EXPERTS_PER_GROUP
D_EXPERT = D_MODEL // 4
ALPHA = (2 * DEPTH) ** 0.25
LN_EPS = 1e-5

LANES = 128
VMEM_LIMIT = 58 * 1024 * 1024

COL_BLK = 256
K_BLK = D_ATT // COL_BLK
V_BLK = (D_ATT + D_KV) // COL_BLK
B_BLK = (D_ATT + 2 * D_KV) // COL_BLK
C_BLK = B_BLK + D_CONV // COL_BLK
U_BLK = C_BLK + D_CONV // COL_BLK


def _params(sem):
    return pltpu.CompilerParams(dimension_semantics=sem, vmem_limit_bytes=VMEM_LIMIT)


def _sigmoid(x):
    return 1.0 / (1.0 + jnp.exp(-x))


def _mm_kernel(*refs, n_a, glu):
    a_refs = refs[:n_a]
    w_ref = refs[n_a]
    g_ref = refs[n_a + 1] if glu else None
    o_ref = refs[-1]
    wb = w_ref[...].astype(BF16)
    acc = None
    off = 0
    for a_ref in a_refs:
        kp = a_ref.shape[1]
        part = jnp.dot(a_ref[...], wb[off:off + kp, :], preferred_element_type=F32)
        acc = part if acc is None else acc + part
        off += kp
    if glu:
        acc = g_ref[...] * _sigmoid(acc)
    o_ref[...] = acc.astype(o_ref.dtype)


def matmul(a_parts, w, layer, *, tm, tn, out_dtype, glu_gate=None):
    m = a_parts[0].shape[0]
    kdim, n = w.shape[-2:]
    assert m % tm == 0 and n % tn == 0 and sum(a.shape[1] for a in a_parts) == kdim
    in_specs = [pl.BlockSpec((tm, a.shape[1]), lambda i, j: (i, 0), pipeline_mode=pl.Buffered(1)) for a in a_parts]
    in_specs.append(pl.BlockSpec((None, kdim, tn), lambda i, j: (layer, 0, j)))
    args = list(a_parts) + [w]
    if glu_gate is not None:
        in_specs.append(pl.BlockSpec((tm, tn), lambda i, j: (i, j)))
        args.append(glu_gate)
    return pl.pallas_call(
        functools.partial(_mm_kernel, n_a=len(a_parts), glu=glu_gate is not None),
        out_shape=jax.ShapeDtypeStruct((m, n), out_dtype),
        grid=(m // tm, n // tn),
        in_specs=in_specs,
        out_specs=pl.BlockSpec((tm, tn), lambda i, j: (i, j)),
        compiler_params=_params(("parallel", "parallel")),
        name="matmul",
    )(*args)


def _layer_norm_rows(v, g, b):
    mu = jnp.mean(v, axis=-1, keepdims=True)
    d = v - mu
    var = jnp.mean(d * d, axis=-1, keepdims=True)
    return d * lax.rsqrt(var + LN_EPS) * g + b


HALF_D = D_MODEL // 2


def _ln_kernel(*refs, tail):
    if tail:
        xh_ref, xt_ref, y_ref, g_ref, b_ref, wr_ref, o_ref, op_ref, gate_ref, eid_ref = refs
        x = jnp.where(pl.program_id(0) < pl.num_programs(0) - 1, xh_ref[...], xt_ref[...])
    else:
        x_ref, y_ref, g_ref, b_ref, wr_ref, o_ref, op_ref, gate_ref, eid_ref = refs
        x = x_ref[...]
    out = _layer_norm_rows(ALPHA * x + y_ref[...], g_ref[...], b_ref[...])
    o_ref[...] = out
    gate_ref[...], eid_ref[...] = _route(out, wr_ref[...])
    bits = pltpu.bitcast(out.astype(BF16).astype(F32), jnp.int32)
    op_ref[...] = bits[:, HALF_D:] | lax.shift_right_logical(bits[:, :HALF_D], 16)


def ln_residual_route(x, y, g, b, w_route, layer, *, tm, x_tail=None):
    n, d = y.shape
    row = pl.BlockSpec((tm, d), lambda i: (i, 0))
    par = pl.BlockSpec((None, 1, d), lambda i: (layer, 0, 0))
    lanes = pl.BlockSpec((tm, ROUTE_COLS), lambda i: (i, 0))
    if x_tail is None:
        x_specs, x_args = [row], [x]
    else:
        assert x_tail.shape[0] == tm and x.shape[0] == n - tm
        head_tiles = x.shape[0] // tm
        x_specs = [pl.BlockSpec((tm, d), lambda i: (jnp.minimum(i, head_tiles - 1), 0)),
                   pl.BlockSpec((tm, d), lambda i: (0, 0))]
        x_args = [x, x_tail]
    return pl.pallas_call(
        functools.partial(_ln_kernel, tail=x_tail is not None),
        out_shape=(jax.ShapeDtypeStruct((n, d), F32), jax.ShapeDtypeStruct((n, HALF_D), jnp.int32),
                   jax.ShapeDtypeStruct((n, ROUTE_COLS), F32), jax.ShapeDtypeStruct((n, ROUTE_COLS), jnp.int32)),
        grid=(n // tm,),
        in_specs=x_specs + [row, par, par, pl.BlockSpec((None, d, ROUTE_COLS), lambda i: (layer, 0, 0))],
        out_specs=(row, pl.BlockSpec((tm, HALF_D), lambda i: (i, 0)), lanes, lanes),
        compiler_params=_params(("arbitrary",)),
        name="ln_residual_route",
    )(*x_args, y, g.reshape(DEPTH, 1, d), b.reshape(DEPTH, 1, d), w_route)


def _ln_moe_kernel(x_ref, y1_ref, y2_ref, gate_ref, g_ref, b_ref, *o_refs, split):
    gate = gate_ref[...]
    ffn = y1_ref[...] * gate[:, 0:1] + y2_ref[...] * gate[:, 1:2]
    out = _layer_norm_rows(ALPHA * x_ref[...] + ffn, g_ref[...], b_ref[...])
    if split:
        head_ref, tail_ref = o_refs
        last = pl.num_programs(0) - 1

        @pl.when(pl.program_id(0) < last)
        def _():
            head_ref[...] = out

        @pl.when(pl.program_id(0) == last)
        def _():
            tail_ref[...] = out
    else:
        o_ref, ob_ref = o_refs
        o_ref[...] = out
        ob_ref[...] = out.astype(BF16)


def ln_moe_combine(x, y1, y2, gates, g, b, layer, *, tm, split_tail=False):
    n, d = x.shape
    row = pl.BlockSpec((tm, d), lambda i: (i, 0))
    par = pl.BlockSpec((None, 1, d), lambda i: (layer, 0, 0))
    if split_tail:
        head_tiles = n // tm - 1
        out_shape = (jax.ShapeDtypeStruct((n - tm, d), F32), jax.ShapeDtypeStruct((tm, d), F32))
        out_specs = (pl.BlockSpec((tm, d), lambda i: (jnp.minimum(i, head_tiles - 1), 0)),
                     pl.BlockSpec((tm, d), lambda i: (0, 0)))
    else:
        out_shape = (jax.ShapeDtypeStruct((n, d), F32), jax.ShapeDtypeStruct((n, d), BF16))
        out_specs = (row, row)
    return pl.pallas_call(
        functools.partial(_ln_moe_kernel, split=split_tail),
        out_shape=out_shape,
        grid=(n // tm,),
        in_specs=[row, row, row, pl.BlockSpec((tm, LANES), lambda i: (i, 0)), par, par],
        out_specs=out_specs,
        compiler_params=_params(("arbitrary",)),
        name="ln_moe_combine",
    )(x, y1, y2, gates, g.reshape(DEPTH, 1, d), b.reshape(DEPTH, 1, d))


def _softmax_sink_pv(s, valid, sink_col, vband):
    s = jnp.where(valid, s, -jnp.inf)
    m = jnp.maximum(jnp.max(s, axis=1, keepdims=True), sink_col)
    p = jnp.exp(s - m)
    denom = jnp.sum(p, axis=1, keepdims=True) + jnp.exp(sink_col - m)
    o = jnp.dot(p.astype(BF16), vband, preferred_element_type=F32)
    return o / denom


def _attn_prompt_kernel(sink_ref, q_ref, kc_ref, kp_ref, vc_ref, vp_ref, o_ref):
    nb = pl.program_id(1)
    pair_w = 2 * HEAD_DIM
    r = lax.broadcasted_iota(jnp.int32, (WINDOW, WINDOW), 0)
    c = lax.broadcasted_iota(jnp.int32, (WINDOW, WINDOW), 1)
    from_prev = c > r
    prev_live = from_prev & (nb > 0)
    first_head = lax.broadcasted_iota(jnp.int32, (WINDOW, pair_w), 1) < HEAD_DIM
    nt = lambda a, b: lax.dot_general(a, b, (((1,), (1,)), ((), ())), preferred_element_type=F32)

    def both_heads(t):
        z = jnp.zeros_like(t)
        return jnp.concatenate([jnp.concatenate([t, z], axis=1), jnp.concatenate([z, t], axis=1)], axis=0).astype(BF16)

    for j in range(N_KV_HEADS):
        cols = slice(j * HEAD_DIM, (j + 1) * HEAD_DIM)
        k2p, k2c = both_heads(kp_ref[:, cols]), both_heads(kc_ref[:, cols])
        v2p, v2c = both_heads(vp_ref[:, cols]), both_heads(vc_ref[:, cols])
        for hp in range(j * Q_PER_KV // 2, (j + 1) * Q_PER_KV // 2):
            lanes = slice(hp * pair_w, (hp + 1) * pair_w)
            q2 = q_ref[:, lanes].astype(BF16)
            s_prev = nt(q2, k2p) * (HEAD_DIM ** -0.5)
            s_cur = nt(q2, k2c) * (HEAD_DIM ** -0.5)
            p_prev, p_cur, inv = [], [], []
            for side in range(2):
                half = slice(side * WINDOW, (side + 1) * WINDOW)
                s = jnp.where(prev_live, s_prev[:, half], jnp.where(from_prev, -jnp.inf, s_cur[:, half]))
                sink = sink_ref[2 * hp + side]
                m = jnp.maximum(jnp.max(s, axis=1, keepdims=True), sink)
                p = jnp.exp(s - m)
                inv.append(1.0 / (jnp.sum(p, axis=1, keepdims=True) + jnp.exp(sink - m)))
                p_prev.append(jnp.where(from_prev, p, 0.0))
                p_cur.append(jnp.where(from_prev, 0.0, p))
            o = (jnp.dot(jnp.concatenate(p_prev, axis=1).astype(BF16), v2p, preferred_element_type=F32)
                 + jnp.dot(jnp.concatenate(p_cur, axis=1).astype(BF16), v2c, preferred_element_type=F32))
            o_ref[:, lanes] = (o * jnp.where(first_head, inv[0], inv[1])).astype(o_ref.dtype)


def attn_prompt(h, sinks, batch, seq):
    nb = seq // WINDOW
    cur = lambda col: pl.BlockSpec((WINDOW, COL_BLK), lambda b, n, s: (b * nb + n, col))
    prev = lambda col: pl.BlockSpec((WINDOW, COL_BLK), lambda b, n, s: (b * nb + jnp.maximum(n - 1, 0), col))
    return pl.pallas_call(
        _attn_prompt_kernel,
        out_shape=jax.ShapeDtypeStruct((h.shape[0], D_ATT), BF16),
        grid_spec=pltpu.PrefetchScalarGridSpec(
            num_scalar_prefetch=1,
            grid=(batch, nb),
            in_specs=[pl.BlockSpec((WINDOW, D_ATT), lambda b, n, s: (b * nb + n, 0)),
                      cur(K_BLK), prev(K_BLK), cur(V_BLK), prev(V_BLK)],
            out_specs=pl.BlockSpec((WINDOW, D_ATT), lambda b, n, s: (b * nb + n, 0)),
        ),
        compiler_params=_params(("parallel", "arbitrary")),
        name="attn_prompt",
    )(sinks, h, h, h, h, h)


T_PAD = 8
KC_PAD = WINDOW + T_PAD


def _attn_sample_kernel(sink_ref, q_ref, kc_ref, vc_ref, o_ref, *, n_new):
    rows = Q_PER_KV * T_PAD
    t = lax.broadcasted_iota(jnp.int32, (rows, KC_PAD), 0) & (T_PAD - 1)
    c = lax.broadcasted_iota(jnp.int32, (rows, KC_PAD), 1)
    valid = (c > t) & (c <= t + WINDOW) & (c < WINDOW + n_new)
    for j in range(N_KV_HEADS):
        heads = range(j * Q_PER_KV, (j + 1) * Q_PER_KV)
        qg = jnp.concatenate([q_ref[:, h * HEAD_DIM:(h + 1) * HEAD_DIM] for h in heads], axis=0).astype(BF16)
        cols = slice(j * HEAD_DIM, (j + 1) * HEAD_DIM)
        kband = kc_ref[:, cols].astype(BF16)
        vband = vc_ref[:, cols].astype(BF16)
        s = lax.dot_general(qg, kband, (((1,), (1,)), ((), ())), preferred_element_type=F32) * (HEAD_DIM ** -0.5)
        sink_col = jnp.concatenate([jnp.full((T_PAD, 1), sink_ref[h], F32) for h in heads], axis=0)
        o = _softmax_sink_pv(s, valid, sink_col, vband)
        o = jnp.concatenate([o[i * T_PAD:(i + 1) * T_PAD] for i in range(Q_PER_KV)], axis=1)
        o_ref[:, j * Q_PER_KV * HEAD_DIM:(j + 1) * Q_PER_KV * HEAD_DIM] = o.astype(o_ref.dtype)


def attn_sample(q, kc, vc, sinks, n_new):
    db = q.shape[0]
    return pl.pallas_call(
        functools.partial(_attn_sample_kernel, n_new=n_new),
        out_shape=jax.ShapeDtypeStruct((db, T_PAD, D_ATT), BF16),
        grid_spec=pltpu.PrefetchScalarGridSpec(
            num_scalar_prefetch=1,
            grid=(db,),
            in_specs=[pl.BlockSpec((None, T_PAD, D_ATT), lambda b, s: (b, 0, 0)),
                      pl.BlockSpec((None, KC_PAD, D_KV), lambda b, s: (b, 0, 0)),
                      pl.BlockSpec((None, KC_PAD, D_KV), lambda b, s: (b, 0, 0))],
            out_specs=pl.BlockSpec((None, T_PAD, D_ATT), lambda b, s: (b, 0, 0)),
        ),
        compiler_params=_params(("parallel",)),
        name="attn_sample",
    )(sinks, q, kc, vc)


def _conv_prompt_kernel(b_ref, c_ref, u_ref, w_ref, o_ref, st_ref):
    cu = c_ref[...] * u_ref[...]
    t = lax.broadcasted_iota(jnp.int32, cu.shape, 0)
    sh1 = jnp.where(t >= 1, pltpu.roll(cu, 1, axis=0), 0.0)
    sh2 = jnp.where(t >= 2, pltpu.roll(cu, 2, axis=0), 0.0)
    w = w_ref[...]
    y = sh2 * w[0:1] + sh1 * w[1:2] + cu * w[2:3]
    o_ref[...] = (b_ref[...] * y).astype(o_ref.dtype)
    st_ref[...] = cu[cu.shape[0] - 8:]


def conv_prompt(h, conv_w, layer_i, batch, seq):
    nj = D_CONV // COL_BLK
    col = lambda base: pl.BlockSpec((seq, COL_BLK), lambda b, j: (b, base + j))
    return pl.pallas_call(
        _conv_prompt_kernel,
        out_shape=(jax.ShapeDtypeStruct((h.shape[0], D_CONV), BF16),
                   jax.ShapeDtypeStruct((batch * 8, D_CONV), F32)),
        grid=(batch, nj),
        in_specs=[col(B_BLK), col(C_BLK), col(U_BLK),
                  pl.BlockSpec((None, CONV_W, COL_BLK), lambda b, j: (layer_i, 0, j))],
        out_specs=(pl.BlockSpec((seq, COL_BLK), lambda b, j: (b, j)),
                   pl.BlockSpec((8, COL_BLK), lambda b, j: (b, j))),
        compiler_params=_params(("parallel", "parallel")),
        name="conv_prompt",
    )(h, h, h, conv_w)


def _conv_sample_kernel(b_ref, c_ref, u_ref, buf_ref, w_ref, o_ref, st_ref, *, n_new):
    w = w_ref[...]
    full = [buf_ref[i] for i in range(CONV_W - 1)] + [c_ref[t] * u_ref[t] for t in range(n_new)]
    for t in range(n_new):
        y = full[t] * w[0:1] + full[t + 1] * w[1:2] + full[t + 2] * w[2:3]
        o_ref[t] = (b_ref[t] * y).astype(o_ref.dtype)
    for i in range(CONV_W - 1):
        st_ref[i] = full[n_new + i]


def conv_sample(bg, cg, ug, buf, conv_w, layer_i):
    n_new, db, _ = bg.shape
    full = lambda a: pl.BlockSpec(a.shape, lambda g: (0,) * a.ndim)
    return pl.pallas_call(
        functools.partial(_conv_sample_kernel, n_new=n_new),
        out_shape=(jax.ShapeDtypeStruct((n_new, db, D_CONV), BF16),
                   jax.ShapeDtypeStruct((CONV_W - 1, db, D_CONV), F32)),
        grid=(1,),
        in_specs=[full(bg), full(cg), full(ug), full(buf),
                  pl.BlockSpec((None, CONV_W, D_CONV), lambda g: (layer_i, 0, 0))],
        out_specs=(pl.BlockSpec((n_new, db, D_CONV), lambda g: (0, 0, 0)),
                   pl.BlockSpec((CONV_W - 1, db, D_CONV), lambda g: (0, 0, 0))),
        compiler_params=_params(("arbitrary",)),
        name="conv_sample",
    )(bg, cg, ug, buf, conv_w)


PAIR = 2
PAIR_STATE = PAIR * SSM_STATE


def _s5_kernel(u_ref, w1_ref, w2t_ref, ap_ref, h0_ref, y_ref, hn_ref, *, nbatch, nchunk, width):
    rows = nbatch * nchunk
    z = jnp.dot(u_ref[...], w1_ref[...], preferred_element_type=F32)
    yi = z[:, :width]
    vr = z[:, width:width + PAIR_STATE]
    vi = z[:, width + PAIR_STATE:]
    h0r = h0_ref[0]
    h0i = h0_ref[1]
    if nchunk == 1:
        h0r_rows, h0i_rows = h0r, h0i
    else:
        h0r_rows = jnp.concatenate([jnp.broadcast_to(h0r[b:b + 1], (nchunk, PAIR_STATE)) for b in range(nbatch)], 0)
        h0i_rows = jnp.concatenate([jnp.broadcast_to(h0i[b:b + 1], (nchunk, PAIR_STATE)) for b in range(nbatch)], 0)
    krow = lax.broadcasted_iota(jnp.int32, (rows, PAIR_STATE), 0) & (nchunk - 1)
    ar = ap_ref[0, 0:1]
    ai = ap_ref[0, 1:2]
    first = krow == 0
    vr = vr + jnp.where(first, ar * h0r_rows - ai * h0i_rows, 0.0)
    vi = vi + jnp.where(first, ar * h0i_rows + ai * h0r_rows, 0.0)
    step = 0
    while (1 << step) < nchunk:
        d = 1 << step
        ar = ap_ref[step, 0:1]
        ai = ap_ref[step, 1:2]
        keep = krow >= d
        sr = jnp.where(keep, pltpu.roll(vr, d, axis=0), 0.0)
        si = jnp.where(keep, pltpu.roll(vi, d, axis=0), 0.0)
        vr, vi = vr + (ar * sr - ai * si), vi + (ar * si + ai * sr)
        step += 1
    if nchunk == 1:
        hr, hi = h0r_rows, h0i_rows
    else:
        hr = jnp.where(first, h0r_rows, pltpu.roll(vr, 1, axis=0))
        hi = jnp.where(first, h0i_rows, pltpu.roll(vi, 1, axis=0))
    hcat = jnp.concatenate([hr, hi], axis=1).astype(BF16)
    y_ref[...] = yi + lax.dot_general(hcat, w2t_ref[...], (((1,), (1,)), ((), ())), preferred_element_type=F32)
    if nchunk == 1:
        hn_ref[0] = vr
        hn_ref[1] = vi
    else:
        last = [b * nchunk + nchunk - 1 for b in range(nbatch)]
        hn_ref[0] = jnp.concatenate([vr[i:i + 1] for i in last], axis=0)
        hn_ref[1] = jnp.concatenate([vi[i:i + 1] for i in last], axis=0)


def s5_scan(u, w1, w2t, ap, h0, nbatch, nchunk):
    g2, rows, width = u.shape
    nstep = ap.shape[1]
    return pl.pallas_call(
        functools.partial(_s5_kernel, nbatch=nbatch, nchunk=nchunk, width=width),
        out_shape=(jax.ShapeDtypeStruct((g2, rows, width), F32),
                   jax.ShapeDtypeStruct((g2, 2, nbatch, PAIR_STATE), F32)),
        grid=(g2,),
        in_specs=[pl.BlockSpec((None, rows, width), lambda g: (g, 0, 0)),
                  pl.BlockSpec((None, width, width + 2 * PAIR_STATE), lambda g: (g, 0, 0)),
                  pl.BlockSpec((None, width, 2 * PAIR_STATE), lambda g: (g, 0, 0)),
                  pl.BlockSpec((None, nstep, 2, PAIR_STATE), lambda g: (g, 0, 0, 0)),
                  pl.BlockSpec((None, 2, nbatch, PAIR_STATE), lambda g: (g, 0, 0, 0))],
        out_specs=(pl.BlockSpec((None, rows, width), lambda g: (g, 0, 0)),
                   pl.BlockSpec((None, 2, nbatch, PAIR_STATE), lambda g: (g, 0, 0, 0))),
        compiler_params=_params(("parallel",)),
        name="s5_scan",
    )(u, w1, w2t, ap, h0)


def s5_apply(x, h0_re, h0_im, ssm, chunk):
    bsz, t, _ = x.shape
    nchunk = t // chunk
    g2 = N_SSM_GROUPS // PAIR
    width = PAIR * chunk * SSM_GROUP
    w1, w2t, ap, _ = s5_operators(*ssm, chunk, nchunk)
    u = x.astype(BF16).reshape(bsz, nchunk, chunk, g2, PAIR, SSM_GROUP)
    u = u.transpose(3, 0, 1, 4, 2, 5).reshape(g2, bsz * nchunk, width)
    pair_state = lambda s: s.reshape(bsz, g2, PAIR_STATE).transpose(1, 0, 2)
    h0 = jnp.stack([pair_state(h0_re), pair_state(h0_im)], axis=1)
    y, hn = s5_scan(u, w1, w2t, ap, h0, bsz, nchunk)
    y = y.reshape(g2, bsz, nchunk, PAIR, chunk, SSM_GROUP).transpose(1, 2, 4, 0, 3, 5).reshape(bsz, t, D_MODEL)
    unpair = lambda s: s.transpose(1, 0, 2).reshape(bsz, N_SSM_GROUPS, SSM_STATE)
    return y, unpair(hn[:, 0]), unpair(hn[:, 1])


def _gelu_tanh(v):
    inner = math.sqrt(2.0 / math.pi) * (v + 0.044715 * (v * v * v))
    return 0.5 * v * (1.0 + jnp.tanh(inner))


GROUPS_PER_TILE = LANES // SSM_GROUP
PAIRS_PER_TILE = GROUPS_PER_TILE // PAIR
S5_CHUNK = 16


def _block_transpose8(arrs):
    arrs = list(arrs)
    lane = lax.broadcasted_iota(jnp.int32, arrs[0].shape, 1)
    for d in (4, 2, 1):
        clear = (lane & (d * SSM_GROUP)) == 0
        nxt = list(arrs)
        for i in range(GROUPS_PER_TILE):
            if i & d:
                continue
            lo, hi = arrs[i], arrs[i + d]
            nxt[i] = jnp.where(clear, lo, pltpu.roll(hi, d * SSM_GROUP, axis=1))
            nxt[i + d] = jnp.where(clear, pltpu.roll(lo, LANES - d * SSM_GROUP, axis=1), hi)
        arrs = nxt
    return arrs


def _shift_lanes(a, k):
    if k == 0:
        return a
    return jnp.concatenate([jnp.zeros((a.shape[0], k), a.dtype), a[:, :a.shape[1] - k]], axis=1)


def _dot_nt_3pass(a, b):
    ah, al = _split_bf16(a)
    bh, bl = _split_bf16(b)
    nt = lambda u, v: lax.dot_general(u, v, (((1,), (1,)), ((), ())), preferred_element_type=F32)
    return nt(ah, bh) + (nt(ah, bl) + nt(al, bh))


def _s5_ops_kernel(cr_ref, ci_ref, btr_ref, bti_ref, pwr_ref, pwi_ref, w1_ref, w2t_ref, *, chunk):
    nl = chunk
    wg = nl * SSM_GROUP
    rep = lambda a, lo: jnp.concatenate(
        [jnp.broadcast_to(a[d:d + 1], (SSM_GROUP, SSM_STATE)) for d in range(lo, lo + nl)], axis=0)
    tile = lambda a: jnp.concatenate([a] * nl, axis=0)
    for q in range(PAIRS_PER_TILE):
        _s5_pair_operators(q, cr_ref, ci_ref, btr_ref, bti_ref, pwr_ref, pwi_ref, w1_ref, w2t_ref, nl, wg, rep, tile)


def _s5_pair_operators(q, cr_ref, ci_ref, btr_ref, bti_ref, pwr_ref, pwi_ref, w1_ref, w2t_ref, nl, wg, rep, tile):
    ky, ksr, ksi, khr, khi = [], [], [], [], []
    for g in range(PAIR * q, PAIR * q + PAIR):
        cr, ci = tile(cr_ref[g]), tile(ci_ref[g])
        btr, bti = btr_ref[g], bti_ref[g]
        pwr, pwi = pwr_ref[g], pwi_ref[g]
        p0r, p0i = rep(pwr, 0), rep(pwi, 0)
        m0r = cr * p0r - ci * p0i
        m0i = cr * p0i + ci * p0r
        r = _dot_nt_3pass(btr, m0r) - _dot_nt_3pass(bti, m0i)
        ky.append(jnp.concatenate([_shift_lanes(r, s * SSM_GROUP) for s in range(nl)], axis=0))
        sr, si = [], []
        for s in range(nl):
            pr = pwr[nl - 1 - s:nl - s]
            pi = pwi[nl - 1 - s:nl - s]
            sr.append(btr * pr - bti * pi)
            si.append(btr * pi + bti * pr)
        ksr.append(jnp.concatenate(sr, axis=0))
        ksi.append(jnp.concatenate(si, axis=0))
        p1r, p1i = rep(pwr, 1), rep(pwi, 1)
        khr.append(cr * p1r - ci * p1i)
        khi.append(-(cr * p1i + ci * p1r))
    zy = jnp.zeros((wg, wg), F32)
    zs = jnp.zeros((wg, SSM_STATE), F32)
    top = jnp.concatenate([ky[0], zy, ksr[0], zs, ksi[0], zs], axis=1)
    bot = jnp.concatenate([zy, ky[1], zs, ksr[1], zs, ksi[1]], axis=1)
    w1_ref[q] = jnp.concatenate([top, bot], axis=0).astype(BF16)
    w2t_ref[q] = jnp.concatenate([jnp.concatenate([khr[0], zs, khi[0], zs], axis=1),
                                  jnp.concatenate([zs, khr[1], zs, khi[1]], axis=1)], axis=0).astype(BF16)


def s5_operators(lam_re, lam_im, log_dt, b_re, b_im, c_re, c_im, chunk, nchunk):
    g, p = lam_re.shape
    g2 = g // PAIR
    wg = chunk * SSM_GROUP
    ldt = lax.complex(lam_re, lam_im) * jnp.exp(log_dt)[:, None]
    a_bar = jnp.exp(ldt)
    b_bar = ((a_bar - 1.0) / lax.complex(lam_re, lam_im))[..., None] * lax.complex(b_re, b_im)
    pw = jnp.exp(ldt[:, None, :] * jnp.arange(chunk + 1, dtype=F32)[None, :, None])
    bt = b_bar.transpose(0, 2, 1)
    grp = lambda a: pl.BlockSpec((GROUPS_PER_TILE,) + a.shape[1:], lambda j: (j, 0, 0))
    args = (c_re, c_im, bt.real, bt.imag, pw.real, pw.imag)
    w1, w2t = pl.pallas_call(
        functools.partial(_s5_ops_kernel, chunk=chunk),
        out_shape=(jax.ShapeDtypeStruct((g2, PAIR * wg, PAIR * wg + 2 * PAIR_STATE), BF16),
                   jax.ShapeDtypeStruct((g2, PAIR * wg, 2 * PAIR_STATE), BF16)),
        grid=(g2 // PAIRS_PER_TILE,),
        in_specs=[grp(a) for a in args],
        out_specs=(pl.BlockSpec((PAIRS_PER_TILE, PAIR * wg, PAIR * wg + 2 * PAIR_STATE), lambda j: (j, 0, 0)),
                   pl.BlockSpec((PAIRS_PER_TILE, PAIR * wg, 2 * PAIR_STATE), lambda j: (j, 0, 0))),
        compiler_params=_params(("parallel",)),
        name="s5_operators",
    )(*args)
    nstep = max(1, (nchunk - 1).bit_length())
    mult = (chunk * (2 ** jnp.arange(nstep))).astype(F32)
    ap = jnp.exp(ldt[:, None, :] * mult[None, :, None])
    pair = lambda t: t.reshape(g2, PAIR, -1, p).transpose(0, 2, 1, 3).reshape(g2, -1, PAIR_STATE)
    ap8 = jnp.exp(ldt[:, None, :] * (chunk * jnp.arange(1, SUBLANES + 1, dtype=F32))[None, :, None])
    return (w1, w2t, jnp.stack([pair(ap.real), pair(ap.imag)], axis=2),
            jnp.stack([pair(ap8.real), pair(ap8.imag)], axis=1))


SUBLANES = 8


def _scan_chunk_rows(vr, vi, ap_ref, ap8_ref, q, nbatch, nchunk):
    row = lax.broadcasted_iota(jnp.int32, (SUBLANES, PAIR_STATE), 0)
    p8r, p8i = ap8_ref[q, 0], ap8_ref[q, 1]
    tiles_r, tiles_i = [], []
    for b in range(nbatch):
        cr = ci = None
        for t in range(nchunk // SUBLANES):
            r0 = b * nchunk + t * SUBLANES
            xr, xi = vr[r0:r0 + SUBLANES], vi[r0:r0 + SUBLANES]
            for step in range(3):
                d = 1 << step
                ar, ai = ap_ref[q, step, 0:1], ap_ref[q, step, 1:2]
                keep = row >= d
                sr = jnp.where(keep, pltpu.roll(xr, d, axis=0), 0.0)
                si = jnp.where(keep, pltpu.roll(xi, d, axis=0), 0.0)
                xr, xi = xr + (ar * sr - ai * si), xi + (ar * si + ai * sr)
            if cr is not None:
                xr, xi = xr + (p8r * cr - p8i * ci), xi + (p8r * ci + p8i * cr)
            cr = jnp.broadcast_to(xr[SUBLANES - 1:], (SUBLANES, PAIR_STATE))
            ci = jnp.broadcast_to(xi[SUBLANES - 1:], (SUBLANES, PAIR_STATE))
            tiles_r.append(xr)
            tiles_i.append(xi)
    return jnp.concatenate(tiles_r, axis=0), jnp.concatenate(tiles_i, axis=0)


def _s5_prompt_kernel(x_ref, w1_ref, w2t_ref, ap_ref, ap8_ref, d_ref, g_ref, gb_ref, hn_ref, *, nbatch, nchunk):
    rows = nbatch * nchunk
    half = S5_CHUNK // 2
    xs = [x_ref[pl.ds(s, rows, stride=S5_CHUNK), :] for s in range(S5_CHUNK)]
    v0 = _block_transpose8(xs[:half])
    v1 = _block_transpose8(xs[half:])
    krow = lax.broadcasted_iota(jnp.int32, (rows, PAIR_STATE), 0) & (nchunk - 1)
    width = PAIR * S5_CHUNK * SSM_GROUP
    y0 = [None] * GROUPS_PER_TILE
    y1 = [None] * GROUPS_PER_TILE
    for q in range(PAIRS_PER_TILE):
        ga, gb = PAIR * q, PAIR * q + 1
        u = jnp.concatenate([v0[ga], v1[ga], v0[gb], v1[gb]], axis=1).astype(BF16)
        z = jnp.dot(u, w1_ref[q], preferred_element_type=F32)
        vr = z[:, width:width + PAIR_STATE]
        vi = z[:, width + PAIR_STATE:]
        vr, vi = _scan_chunk_rows(vr, vi, ap_ref, ap8_ref, q, nbatch, nchunk)
        first = krow == 0
        hr = jnp.where(first, 0.0, pltpu.roll(vr, 1, axis=0))
        hi = jnp.where(first, 0.0, pltpu.roll(vi, 1, axis=0))
        hcat = jnp.concatenate([hr, hi], axis=1).astype(BF16)
        y = z[:, :width] + lax.dot_general(hcat, w2t_ref[q], (((1,), (1,)), ((), ())), preferred_element_type=F32)
        y0[ga], y1[ga] = y[:, 0:LANES], y[:, LANES:2 * LANES]
        y0[gb], y1[gb] = y[:, 2 * LANES:3 * LANES], y[:, 3 * LANES:]
        last = [b * nchunk + nchunk - 1 for b in range(nbatch)]
        hn_ref[q, 0] = jnp.concatenate([vr[i:i + 1] for i in last], axis=0)
        hn_ref[q, 1] = jnp.concatenate([vi[i:i + 1] for i in last], axis=0)
    ys = _block_transpose8(y0) + _block_transpose8(y1)
    dskip = d_ref[...]
    for s in range(S5_CHUNK):
        g_ref[pl.ds(s, rows, stride=S5_CHUNK), :] = _gelu_tanh(ys[s] + dskip * xs[s])
    gb_ref[...] = g_ref[...].astype(BF16)


def s5_prompt(x, w1, w2t, ap, ap8, d, layer_i, nbatch, seq):
    n, dm = x.shape
    nchunk = seq // S5_CHUNK
    n_p = nbatch * seq
    nstep = ap.shape[1]
    g2 = w1.shape[0]
    width = PAIR * S5_CHUNK * SSM_GROUP
    tile = pl.BlockSpec((n_p, LANES), lambda j: (0, j))
    return pl.pallas_call(
        functools.partial(_s5_prompt_kernel, nbatch=nbatch, nchunk=nchunk),
        out_shape=(jax.ShapeDtypeStruct((n, dm), F32), jax.ShapeDtypeStruct((n, dm), BF16),
                   jax.ShapeDtypeStruct((g2, 2, nbatch, PAIR_STATE), F32)),
        grid=(dm // LANES,),
        in_specs=[tile,
                  pl.BlockSpec((PAIRS_PER_TILE, width, width + 2 * PAIR_STATE), lambda j: (j, 0, 0)),
                  pl.BlockSpec((PAIRS_PER_TILE, width, 2 * PAIR_STATE), lambda j: (j, 0, 0)),
                  pl.BlockSpec((PAIRS_PER_TILE, nstep, 2, PAIR_STATE), lambda j: (j, 0, 0, 0)),
                  pl.BlockSpec((PAIRS_PER_TILE, 2, SUBLANES, PAIR_STATE), lambda j: (j, 0, 0, 0)),
                  pl.BlockSpec((None, 1, LANES), lambda j: (layer_i, 0, j))],
        out_specs=(tile, tile,
                   pl.BlockSpec((PAIRS_PER_TILE, 2, nbatch, PAIR_STATE), lambda j: (j, 0, 0, 0))),
        compiler_params=_params(("parallel",)),
        name="s5_prompt",
    )(x, w1, w2t, ap, ap8, d.reshape(d.shape[0], 1, dm))


def _gelu_skip_kernel(y_ref, x_ref, d_ref, o_ref):
    o_ref[...] = _gelu_tanh(y_ref[...] + d_ref[...] * x_ref[...])


def gelu_skip(y, x, d, layer_i, *, tm):
    n, dm = x.shape
    row = pl.BlockSpec((tm, dm), lambda i: (i, 0))
    return pl.pallas_call(
        _gelu_skip_kernel,
        out_shape=jax.ShapeDtypeStruct((n, dm), F32),
        grid=(n // tm,),
        in_specs=[row, row, pl.BlockSpec((None, 1, dm), lambda i: (layer_i, 0, 0))],
        out_specs=row,
        compiler_params=_params(("parallel",)),
        name="gelu_skip",
    )(y, x, d.reshape(d.shape[0], 1, dm))


ROUTE_COLS = LANES


def _split_bf16(v):
    hi = v.astype(BF16)
    lo = (v - hi.astype(F32)).astype(BF16)
    return hi, lo


def _route(x, w):
    xh, xl = _split_bf16(x)
    wh, wl = _split_bf16(w)
    dot = lambda a, b: jnp.dot(a, b, preferred_element_type=F32)
    logits = dot(xh, wh) + (dot(xh, wl) + dot(xl, wh))
    lane = lax.broadcasted_iota(jnp.int32, logits.shape, 1).astype(F32)
    neg = -jnp.inf
    big = float(ROUTE_COLS)
    lg = jnp.where(lane < N_EXPERT_GROUPS, logits, neg)
    m = jnp.max(lg, axis=1, keepdims=True)
    grp = jnp.min(jnp.where(lg == m, lane, big), axis=1, keepdims=True)
    gate_g = 1.0 / jnp.sum(jnp.exp(lg - m), axis=1, keepdims=True)
    lo = N_EXPERT_GROUPS + grp * EXPERTS_PER_GROUP
    le = jnp.where((lane >= lo) & (lane < lo + EXPERTS_PER_GROUP), logits, neg)
    t1 = jnp.max(le, axis=1, keepdims=True)
    i1 = jnp.min(jnp.where(le == t1, lane, big), axis=1, keepdims=True)
    le2 = jnp.where(lane == i1, neg, le)
    t2 = jnp.max(le2, axis=1, keepdims=True)
    i2 = jnp.min(jnp.where(le2 == t2, lane, big), axis=1, keepdims=True)
    e = jnp.exp(t2 - t1)
    p1 = 1.0 / (1.0 + e)
    p2 = e / (1.0 + e)
    gates = jnp.where(lane == 0.0, gate_g * p1, jnp.where(lane == 1.0, gate_g * p2, 0.0))
    eid = jnp.where(lane == 0.0, i1, jnp.where(lane == 1.0, i2, float(N_EXPERT_GROUPS))) - N_EXPERT_GROUPS
    return gates, eid.astype(jnp.int32)


MOE_SUB = 128
MOE_NSUB = 6
MOE_KC = 1024
MOE_TM = MOE_SUB * MOE_NSUB
MOE_TF = 256
MOE_TN = 1024
MOE_NF = D_EXPERT // MOE_TF
MOE_NN = D_MODEL // MOE_TN


def _moe_kernel(be_ref, ns_ref, bi_ref, x_ref, wg_ref, wu_ref, wd_ref, o_ref, xb_ref, hid_ref):
    blk = pl.program_id(0)
    s = pl.program_id(1)
    nsub = ns_ref[blk]
    live = nsub > 0

    @pl.when(live & (s == 0))
    def _():
        words = x_ref[...]
        xb_ref[:, :HALF_D] = pltpu.bitcast(words << 16, F32).astype(BF16)
        xb_ref[:, HALF_D:] = pltpu.bitcast(words & -65536, F32).astype(BF16)

    for n in range(1, MOE_NSUB + 1):
        rows = n * MOE_SUB

        @pl.when((nsub == n) & (s < MOE_NF))
        def _():
            gate = up = None
            for kc in range(D_MODEL // MOE_KC):
                ks = slice(kc * MOE_KC, (kc + 1) * MOE_KC)
                xr = xb_ref[0:rows, ks]
                gp = jnp.dot(xr, wg_ref[ks, :].astype(BF16), preferred_element_type=F32)
                upp = jnp.dot(xr, wu_ref[ks, :].astype(BF16), preferred_element_type=F32)
                gate = gp if gate is None else gate + gp
                up = upp if up is None else up + upp
            hid_ref[s, 0:rows, :] = ((gate * _sigmoid(gate)) * up).astype(BF16)

        @pl.when((nsub == n) & (s >= MOE_NF))
        def _():
            out = None
            for f in range(MOE_NF):
                fs = slice(f * MOE_TF, (f + 1) * MOE_TF)
                part = jnp.dot(hid_ref[f, 0:rows, :], wd_ref[fs, :].astype(BF16), preferred_element_type=F32)
                out = part if out is None else out + part
            o_ref[0:rows, :] = out
            if rows < MOE_TM:
                o_ref[rows:, :] = jnp.zeros((MOE_TM - rows, MOE_TN), F32)


def moe_experts(xs, blk_e, blk_nsub, blk_idx, w_gate, w_up, w_down, layer):
    p = xs.shape[0]
    d = D_MODEL
    nblk = p // MOE_TM
    up_idx = lambda s, ns, b: jnp.where(ns[b] > 0, jnp.minimum(s, MOE_NF - 1), MOE_NF - 1)
    dn_idx = lambda s, ns, b: jnp.where(ns[b] > 0, jnp.maximum(s - MOE_NF, 0), MOE_NN - 1)
    x_blk = lambda s, b: jnp.minimum(b + (s >= MOE_NF).astype(jnp.int32), nblk - 1)
    return pl.pallas_call(
        _moe_kernel,
        out_shape=jax.ShapeDtypeStruct((p, d), F32),
        grid_spec=pltpu.PrefetchScalarGridSpec(
            num_scalar_prefetch=3,
            grid=(nblk, MOE_NF + MOE_NN),
            in_specs=[
                pl.BlockSpec((MOE_TM, HALF_D), lambda b, s, be, ns, bi: (bi[x_blk(s, b)], 0)),
                pl.BlockSpec((None, None, d, MOE_TF), lambda b, s, be, ns, bi: (layer, be[b], 0, up_idx(s, ns, b))),
                pl.BlockSpec((None, None, d, MOE_TF), lambda b, s, be, ns, bi: (layer, be[b], 0, up_idx(s, ns, b))),
                pl.BlockSpec((None, None, D_EXPERT, MOE_TN),
                             lambda b, s, be, ns, bi: (layer, be[b], 0, dn_idx(s, ns, b)))],
            out_specs=pl.BlockSpec((MOE_TM, MOE_TN), lambda b, s, be, ns, bi: (bi[b], dn_idx(s, ns, b))),
            scratch_shapes=[pltpu.VMEM((MOE_TM, d), BF16), pltpu.VMEM((MOE_NF, MOE_TM, MOE_TF), BF16)],
        ),
        compiler_params=_params(("arbitrary", "arbitrary")),
        name="moe_experts",
    )(blk_e, blk_nsub, blk_idx, xs, w_gate, w_up, w_down)


def moe_dispatch(eids, n):
    nslot = n * 2
    eid = eids.reshape(-1)
    experts = jnp.arange(N_EXPERTS, dtype=jnp.int32)
    onehot = (eid[:, None] == experts[None, :]).astype(jnp.int32)
    seen = jnp.cumsum(onehot, axis=0)
    rank = jnp.sum(seen * onehot, axis=1) - 1
    counts = seen[-1]
    nblk = (counts + MOE_TM - 1) // MOE_TM
    bend = jnp.cumsum(nblk)
    bstart = bend - nblk
    dest = jnp.sum(onehot * bstart[None, :], axis=1) * MOE_TM + rank
    n_blocks = nslot // MOE_TM + N_EXPERTS
    row_tok = jnp.zeros((n_blocks * MOE_TM,), jnp.int32).at[dest].set(jnp.arange(nslot, dtype=jnp.int32) // 2)
    pos = dest.reshape(n, 2)
    blk = jnp.arange(n_blocks, dtype=jnp.int32)
    n_used = bend[-1]
    blk_idx = jnp.minimum(blk, n_used - 1)
    blk_e = jnp.sum((blk_idx[:, None] >= bend[None, :]).astype(jnp.int32), axis=1)
    rows_left = counts[blk_e] - (blk_idx - bstart[blk_e]) * MOE_TM
    nsub = (jnp.clip(rows_left, 0, MOE_TM) + MOE_SUB - 1) // MOE_SUB
    blk_nsub = jnp.where(blk < n_used, nsub, 0).astype(jnp.int32)
    return pos, row_tok, blk_e.astype(jnp.int32), blk_nsub, blk_idx


def hier_moe_ln(x, x_packed, gates, eids, layer, w_gate, w_up, w_down, ln_g, ln_b, *, tm, split_tail=False):
    n = x.shape[0]
    pos, row_tok, blk_e, blk_nsub, blk_idx = moe_dispatch(eids[:, :2], n)
    xs = x_packed[row_tok]
    ys = moe_experts(xs, blk_e, blk_nsub, blk_idx, w_gate, w_up, w_down, layer)
    return ln_moe_combine(x, ys[pos[:, 0]], ys[pos[:, 1]], gates, ln_g, ln_b, layer, tm=tm, split_tail=split_tail)


ROW_TILE = 320
MM_TM = 2080
MM_TN = 512


def _rows_after(full, tail, start):
    return lax.dynamic_update_slice(full, tail.astype(full.dtype), (start, 0))


def kernel(x_prompt, x_sample, cache_k, cache_v, state_conv, state_ssm_re, state_ssm_im, w_in, w_out, attn_sinks, conv_w, ssm_lambda_re, ssm_lambda_im, ssm_log_dt, ssm_b_re, ssm_b_im, ssm_c_re, ssm_c_im, ssm_d, w_glu, w_o, ln1_g, ln1_b, ln2_g, ln2_b, w_router_group, w_router_expert, w_gate, w_up, w_down):
    batch, seq, _ = x_prompt.shape
    db, n_new, _ = x_sample.shape
    n_p = batch * seq
    n_s = db * n_new
    n = n_p + n_s
    assert n % ROW_TILE == 0 and n % MM_TM == 0 and n_new <= T_PAD and n_p % n_s == 0 and n_s % SUBLANES == 0

    w_route = jnp.concatenate(
        [w_router_group, w_router_expert,
         jnp.zeros((DEPTH, D_MODEL, ROUTE_COLS - N_EXPERT_GROUPS - N_EXPERTS), F32)], axis=-1)

    x_head, x_tail = x_prompt.reshape(n_p, D_MODEL), x_sample.reshape(n_s, D_MODEL)
    xb = jnp.concatenate([x_head.astype(BF16), x_tail.astype(BF16)], axis=0)
    k_p, v_p, conv_p, re_p, im_p = [], [], [], [], []
    k_s, v_s, conv_s, re_s, im_s = [], [], [], [], []
    for layer in range(DEPTH):
        i = layer // 2
        if layer % 2 == 0:
            h = matmul([xb], w_in, i, tm=MM_TM, tn=MM_TN, out_dtype=F32)
            attn = attn_prompt(h, attn_sinks[i], batch, seq)
            gconv, cu_tail = conv_prompt(h, conv_w, i, batch, seq)
            tails = [h[b * seq + seq - WINDOW:(b + 1) * seq, D_ATT:D_ATT + 2 * D_KV] for b in range(batch)]
            kv_tail = jnp.stack(tails)
            k_p.append(kv_tail[:, :, :D_KV].reshape(batch, WINDOW, N_KV_HEADS, HEAD_DIM))
            v_p.append(kv_tail[:, :, D_KV:].reshape(batch, WINDOW, N_KV_HEADS, HEAD_DIM))
            conv_p.append(cu_tail.reshape(batch, 8, D_CONV)[:, 8 - (CONV_W - 1):])
            hs = h[n_p:].reshape(db, n_new, D_IN_EVEN)
            k_new = hs[:, :, D_ATT:D_ATT + D_KV]
            v_new = hs[:, :, D_ATT + D_KV:D_ATT + 2 * D_KV]
            pad_rows = jnp.zeros((db, KC_PAD - WINDOW - n_new, D_KV), F32)
            kc = jnp.concatenate([cache_k[i].reshape(db, WINDOW, D_KV), k_new, pad_rows], axis=1)
            vc = jnp.concatenate([cache_v[i].reshape(db, WINDOW, D_KV), v_new, pad_rows], axis=1)
            q_s = jnp.pad(hs[:, :, :D_ATT], ((0, 0), (0, T_PAD - n_new), (0, 0)))
            attn_s = attn_sample(q_s, kc, vc, attn_sinks[i], n_new)[:, :n_new].reshape(n_s, D_ATT)
            k_s.append(kc[:, n_new:n_new + WINDOW].reshape(db, WINDOW, N_KV_HEADS, HEAD_DIM))
            v_s.append(vc[:, n_new:n_new + WINDOW].reshape(db, WINDOW, N_KV_HEADS, HEAD_DIM))
            off = D_ATT + 2 * D_KV
            tmaj = lambda a: a.transpose(1, 0, 2)
            gconv_s, st_s = conv_sample(tmaj(hs[:, :, off:off + D_CONV]),
                                        tmaj(hs[:, :, off + D_CONV:off + 2 * D_CONV]),
                                        tmaj(hs[:, :, off + 2 * D_CONV:]),
                                        tmaj(state_conv[i]), conv_w, i)
            conv_s.append(tmaj(st_s))
            attn = _rows_after(attn, attn_s, n_p)
            gconv = _rows_after(gconv, tmaj(gconv_s).reshape(n_s, D_CONV), n_p)
            mixed = matmul([attn, gconv], w_out, i, tm=MM_TM, tn=MM_TN, out_dtype=F32)
        else:
            ssm = (ssm_lambda_re[i], ssm_lambda_im[i], ssm_log_dt[i], ssm_b_re[i], ssm_b_im[i],
                   ssm_c_re[i], ssm_c_im[i])
            w1, w2t, ap, ap8 = s5_operators(*ssm, S5_CHUNK, seq // S5_CHUNK)
            g, gb, hn = s5_prompt(x, w1, w2t, ap, ap8, ssm_d, i, batch, seq)
            unpair = lambda s: s.transpose(1, 0, 2).reshape(batch, N_SSM_GROUPS, SSM_STATE)
            re_p.append(unpair(hn[:, 0]))
            im_p.append(unpair(hn[:, 1]))
            x_s = x[n_p:]
            y_s, nre_s, nim_s = s5_apply(x_s.reshape(db, n_new, D_MODEL), state_ssm_re[i], state_ssm_im[i],
                                         ssm, n_new)
            re_s.append(nre_s)
            im_s.append(nim_s)
            g_s = gelu_skip(y_s.reshape(n_s, D_MODEL), x_s, ssm_d, i, tm=n_s)
            g = _rows_after(g, g_s, n_p)
            gb = _rows_after(gb, g_s, n_p)
            z = matmul([gb], w_glu, i, tm=MM_TM, tn=MM_TN, out_dtype=BF16, glu_gate=g)
            mixed = matmul([z], w_o, i, tm=MM_TM, tn=MM_TN, out_dtype=F32)
        if layer == 0:
            x, x_packed, gates, eids = ln_residual_route(x_head, mixed, ln1_g, ln1_b, w_route, layer, tm=n_s,
                                                         x_tail=x_tail)
        else:
            x, x_packed, gates, eids = ln_residual_route(x, mixed, ln1_g, ln1_b, w_route, layer, tm=ROW_TILE)
        last = layer == DEPTH - 1
        x, xb = hier_moe_ln(x, x_packed, gates, eids, layer, w_gate, w_up, w_down, ln2_g, ln2_b,
                            tm=n_s if last else ROW_TILE, split_tail=last)

    y_prompt = x.reshape(batch, seq, D_MODEL)
    y_sample = xb.reshape(db, n_new, D_MODEL)
    st = jnp.stack
    return (y_prompt, y_sample, st(k_p), st(v_p), st(conv_p), st(re_p), st(im_p),
            st(k_s), st(v_s), st(conv_s), st(re_s), st(im_s))
```
